```python
import jax, jax.numpy as jnp
from jax import lax
import numpy as np

D_MODEL = 1024
BATCH = 8
SEQ = 4096
DEPTH = 4
DEC_BATCH = 16
DEC_SEQ = 32
PAST_LEN = 1024

CHUNK = 64
Q_BLOCK = 128
N_MEM = 256
EPS = 1e-5
ALPHA = (2 * DEPTH) ** 0.25
BETA = (8 * DEPTH) ** -0.25

POOL_WINDOWS = (2, 4, 8, 16)
POOL_GROUP = D_MODEL // 16
POOL_WIDTH = len(POOL_WINDOWS) * POOL_GROUP
POOL_HIST = max(POOL_WINDOWS) - 1

MLA_HEADS = 8
MLA_Q_RANK = D_MODEL // 4
MLA_KV_RANK = D_MODEL // 4
MLA_NOPE = 64
MLA_ROPE = 32
MLA_V = 64
MLA_SCALE = (MLA_NOPE + MLA_ROPE) ** -0.5
ROPE_BASE = 10000.0

GLA_HEADS = 4
GLA_DK = 32
GLA_DV = 64
GLA_GATE_RANK = 16
GLA_GATE_TAU = 16.0

MEM_HEADS = 4
MEM_HEAD_DIM = D_MODEL // MEM_HEADS
D_FF = 2816

IN_SPLITS = (POOL_WIDTH, MLA_Q_RANK, MLA_KV_RANK, MLA_ROPE,
             GLA_HEADS * GLA_DK, GLA_HEADS * GLA_DK, GLA_HEADS * GLA_DV,
             GLA_GATE_RANK, GLA_HEADS * GLA_DV, 3 * D_MODEL)
IN_COLS = sum(IN_SPLITS)

kernel_name = 'hybrid_streaming_encoder_step'


def layer_norm(x, g, b):
    xf = x.astype(jnp.float32)
    mu = jnp.mean(xf, -1, keepdims=True)
    var = jnp.mean(jnp.square(xf - mu), -1, keepdims=True)
    return ((xf - mu) * lax.rsqrt(var + EPS) * g + b).astype(x.dtype)


def rms_norm(x, g):
    xf = x.astype(jnp.float32)
    return (xf * lax.rsqrt(jnp.mean(xf * xf, -1, keepdims=True) + EPS) * g).astype(x.dtype)


def swiglu(x, w_in, w_out):
    gate, up = jnp.split(x @ w_in, 2, axis=-1)
    return (jax.nn.silu(gate) * up) @ w_out


def split_in(h):
    idx, acc = [], 0
    for w in IN_SPLITS[:-1]:
        acc += w
        idx.append(acc)
    return jnp.split(h, idx, axis=-1)


def rope(x, pos):
    half = MLA_ROPE // 2
    inv = ROPE_BASE ** (-jnp.arange(half, dtype=jnp.float32) / half)
    ang = pos.astype(jnp.float32)[:, None] * inv
    cos = jnp.cos(ang)[:, None, :]
    sin = jnp.sin(ang)[:, None, :]
    xf = x.astype(jnp.float32)
    x1, x2 = xf[..., :half], xf[..., half:]
    return jnp.concatenate([x1 * cos - x2 * sin, x2 * cos + x1 * sin], -1).astype(x.dtype)


def pool_mixer(u, hist, pos, pool_w, pool_scale):
    B, S, _ = u.shape
    ext = jnp.concatenate([hist, u], 1).astype(jnp.float32)
    cs = jnp.pad(jnp.cumsum(ext, axis=1), ((0, 0), (1, 0), (0, 0)))
    outs = []
    for gi, w in enumerate(POOL_WINDOWS):
        sl = slice(gi * POOL_GROUP, (gi + 1) * POOL_GROUP)
        hi = cs[:, POOL_HIST + 1:, sl]
        lo = cs[:, POOL_HIST + 1 - w:POOL_HIST + 1 - w + S, sl]
        cnt = jnp.minimum(pos + 1, w).astype(jnp.float32)[None, :, None]
        outs.append((hi - lo) / cnt)
    pooled = jnp.concatenate(outs, -1) - ext[:, POOL_HIST:]
    pooled = pooled.reshape(B, S, len(POOL_WINDOWS), POOL_GROUP)
    y = jnp.einsum('bsgc,gcd->bsgd', pooled, pool_w).reshape(B, S, POOL_WIDTH) * pool_scale
    return y.astype(u.dtype), ext[:, -POOL_HIST:].astype(u.dtype)


def mla_attend(q_abs, q_rope, ckv, kr, mask):
    s = jnp.einsum('bqhc,bkc->bhqk', q_abs, ckv) + jnp.einsum('bqhr,bkr->bhqk', q_rope, kr)
    s = s.astype(jnp.float32) * MLA_SCALE
    if mask is not None:
        s = jnp.where(mask, s, -jnp.inf)
    p = jax.nn.softmax(s, axis=-1).astype(ckv.dtype)
    return jnp.einsum('bhqk,bkc->bqhc', p, ckv)


def mla_prompt_attend(q_abs, q_rope, ckv, kr):
    B, S = q_abs.shape[:2]
    nb = S // Q_BLOCK
    qa = q_abs.reshape(B, nb, Q_BLOCK, MLA_HEADS, MLA_KV_RANK).swapaxes(0, 1)
    qr = q_rope.reshape(B, nb, Q_BLOCK, MLA_HEADS, MLA_ROPE).swapaxes(0, 1)
    key_chunk = jnp.arange(S) // CHUNK

    def blk(args):
        i, qa_i, qr_i = args
        q_chunk = (i * Q_BLOCK + jnp.arange(Q_BLOCK)) // CHUNK
        return mla_attend(qa_i, qr_i, ckv, kr, key_chunk[None, :] <= q_chunk[:, None])

    lat = lax.map(blk, (jnp.arange(nb), qa, qr))
    return lat.swapaxes(0, 1).reshape(B, S, MLA_HEADS, MLA_KV_RANK)


def gla_block(S0, xs):
    q, k, v, la = xs
    C = q.shape[2]
    b = jnp.cumsum(la, axis=2)
    o_inter = jnp.einsum('bhtk,bhkv->bhtv', q * jnp.exp(b), S0)
    causal = jnp.tril(jnp.ones((C, C), bool))[:, :, None]
    diff = b[:, :, :, None, :] - b[:, :, None, :, :]
    decay = jnp.exp(jnp.where(causal, diff, -jnp.inf))
    A = jnp.einsum('bhtk,bhtsk,bhsk->bhts', q, decay, k)
    o = o_inter + jnp.einsum('bhts,bhsv->bhtv', A, v)
    bC = b[:, :, -1]
    S_new = jnp.exp(bC)[..., None] * S0 + jnp.einsum('bhsk,bhsv->bhkv', k * jnp.exp(bC[:, :, None] - b), v)
    return S_new, o


def gla_mixer(gq, gk, gv, ga, gr, S0, w_a2, b_a, norm_g):
    B, S, _ = gq.shape
    f32 = jnp.float32

    def heads(t, d):
        return t.reshape(B, S, GLA_HEADS, d).transpose(0, 2, 1, 3).astype(f32)

    q = heads(gq, GLA_DK) * GLA_DK ** -0.5
    k = heads(gk, GLA_DK)
    v = heads(gv, GLA_DV)
    la = heads(jax.nn.log_sigmoid((ga @ w_a2 + b_a).astype(f32)) / GLA_GATE_TAU, GLA_DK)
    C = min(S, CHUNK)
    nc = S // C

    def to_blocks(t):
        return t.reshape(B, GLA_HEADS, nc, C, t.shape[-1]).transpose(2, 0, 1, 3, 4)

    S_fin, o = lax.scan(gla_block, S0.astype(f32), (to_blocks(q), to_blocks(k), to_blocks(v), to_blocks(la)))
    o = o.transpose(1, 2, 0, 3, 4).reshape(B, GLA_HEADS, S, GLA_DV).transpose(0, 2, 1, 3)
    o = rms_norm(o, norm_g).reshape(B, S, GLA_HEADS * GLA_DV)
    y = o * jax.nn.silu(gr.astype(f32))
    return y.astype(gq.dtype), S_fin.astype(gq.dtype)


def token_mixing(x, pos0, pool_hist, gla_state, ckv_past, kr_past, lw):
    B, S, _ = x.shape
    pos = pos0 + jnp.arange(S)
    u_pool, q_lat, c_kv, kr_raw, gq, gk, gv, ga, gr, gates = split_in(x @ lw['w_in'])
    y_pool, new_pool = pool_mixer(u_pool, pool_hist, pos, lw['pool_w'], lw['pool_scale'])
    q = (rms_norm(q_lat, lw['mla_q_norm']) @ lw['mla_w_uq']).reshape(B, S, MLA_HEADS, MLA_NOPE + MLA_ROPE)
    q_nope, q_rope = q[..., :MLA_NOPE], rope(q[..., MLA_NOPE:], pos)
    q_abs = jnp.einsum('bshn,chn->bshc', q_nope, lw['mla_w_uk'])
    ckv = rms_norm(c_kv, lw['mla_kv_norm'])
    kr = rope(kr_raw[:, :, None, :], pos)[:, :, 0]
    if ckv_past is None:
        lat = mla_prompt_attend(q_abs, q_rope, ckv, kr)
    else:
        lat = mla_attend(q_abs, q_rope, jnp.concatenate([ckv_past, ckv], 1),
                         jnp.concatenate([kr_past, kr], 1), None)
    y_mla = jnp.einsum('bshc,chv->bshv', lat, lw['mla_w_uv']).reshape(B, S, MLA_HEADS * MLA_V)
    y_gla, new_gla = gla_mixer(gq, gk, gv, ga, gr, gla_state, lw['gla_w_a2'], lw['gla_b_a'], lw['gla_norm'])
    g_a, g_b, g_c = jnp.split(jax.nn.sigmoid(gates), 3, axis=-1)
    m = g_a * (y_pool @ lw['w_branch_pool']) + g_b * (y_mla @ lw['w_branch_mla']) + g_c * (y_gla @ lw['w_branch_gla'])
    return m @ lw['w_out'], new_pool, ckv, kr, new_gla


def memory_kv(mem, w_k, w_v):
    B, M, _ = mem.shape
    k = (mem @ w_k).reshape(B, M, MEM_HEADS, MEM_HEAD_DIM)
    v = (mem @ w_v).reshape(B, M, MEM_HEADS, MEM_HEAD_DIM)
    return k, v


def memory_attend(x, k, v, w_q, w_o):
    B, S, _ = x.shape
    q = (x @ w_q).reshape(B, S, MEM_HEADS, MEM_HEAD_DIM)
    s = jnp.einsum('bqhd,bkhd->bhqk', q, k).astype(jnp.float32) * MEM_HEAD_DIM ** -0.5
    p = jax.nn.softmax(s, axis=-1).astype(v.dtype)
    o = jnp.einsum('bhqk,bkhd->bqhd', p, v).reshape(B, S, D_MODEL)
    return o @ w_o


def trunk_layer(x, pos0, pool_hist, gla_state, ckv_past, kr_past, mem_k, mem_v, lw):
    g, b = lw['ln_g'], lw['ln_b']
    x = layer_norm(ALPHA * x + 0.5 * swiglu(x, lw['w_ffn1_in'], lw['w_ffn1_out']), g[0], b[0])
    mix, new_pool, ckv, kr, new_gla = token_mixing(x, pos0, pool_hist, gla_state, ckv_past, kr_past, lw)
    x = layer_norm(ALPHA * x + mix, g[1], b[1])
    x = layer_norm(ALPHA * x + memory_attend(x, mem_k, mem_v, lw['mem_w_q'], lw['mem_w_o']), g[2], b[2])
    x = layer_norm(ALPHA * x + 0.5 * swiglu(x, lw['w_ffn2_in'], lw['w_ffn2_out']), g[3], b[3])
    return x, new_pool, ckv, kr, new_gla


def setup_inputs(seed: int = 0) -> dict:
    key = jax.random.key(seed)
    ks = iter(jax.random.split(key, 48))
    L, D = DEPTH, D_MODEL

    def nrm(shape, scale=1.0):
        return jax.random.normal(next(ks), shape, jnp.float32) * scale

    inp = {}
    inp['x_prompt'] = nrm((BATCH, SEQ, D))
    inp['x_sample'] = nrm((DEC_BATCH, DEC_SEQ, D))
    inp['mem_prompt'] = nrm((BATCH, N_MEM, D))
    inp['cache_pool'] = nrm((L, DEC_BATCH, POOL_HIST, POOL_WIDTH))
    inp['cache_mla_ckv'] = nrm((L, DEC_BATCH, PAST_LEN, MLA_KV_RANK))
    inp['cache_mla_krope'] = nrm((L, DEC_BATCH, PAST_LEN, MLA_ROPE))
    inp['state_gla'] = nrm((L, DEC_BATCH, GLA_HEADS, GLA_DK, GLA_DV), 0.5)
    inp['cache_mem_k'] = nrm((L, DEC_BATCH, N_MEM, MEM_HEADS, MEM_HEAD_DIM))
    inp['cache_mem_v'] = nrm((L, DEC_BATCH, N_MEM, MEM_HEADS, MEM_HEAD_DIM))
    inp['ln_g'] = 1.0 + nrm((L, 4, D), 0.02)
    inp['ln_b'] = nrm((L, 4, D), 0.02)
    inp['w_ffn1_in'] = nrm((L, D, 2 * D_FF), D ** -0.5)
    inp['w_ffn1_out'] = nrm((L, D_FF, D), BETA * D_FF ** -0.5)
    inp['w_in'] = nrm((L, D, IN_COLS), D ** -0.5)
    inp['pool_w'] = nrm((L, len(POOL_WINDOWS), POOL_GROUP, POOL_GROUP), POOL_GROUP ** -0.5)
    inp['pool_scale'] = 1.0 + nrm((L, POOL_WIDTH), 0.02)
    inp['mla_q_norm'] = 1.0 + nrm((L, MLA_Q_RANK), 0.02)
    inp['mla_kv_norm'] = 1.0 + nrm((L, MLA_KV_RANK), 0.02)
    inp['mla_w_uq'] = nrm((L, MLA_Q_RANK, MLA_HEADS * (MLA_NOPE + MLA_ROPE)), MLA_Q_RANK ** -0.5)
    inp['mla_w_uk'] = nrm((L, MLA_KV_RANK, MLA_HEADS, MLA_NOPE), MLA_KV_RANK ** -0.5)
    inp['mla_w_uv'] = nrm((L, MLA_KV_RANK, MLA_HEADS, MLA_V), MLA_KV_RANK ** -0.5)
    inp['gla_w_a2'] = nrm((L, GLA_GATE_RANK, GLA_HEADS * GLA_DK), GLA_GATE_RANK ** -0.5)
    inp['gla_b_a'] = nrm((L, GLA_HEADS * GLA_DK), 0.1)
    inp['gla_norm'] = 1.0 + nrm((L, GLA_DV), 0.02)
    inp['w_branch_pool'] = nrm((L, POOL_WIDTH, D), POOL_WIDTH ** -0.5)
    inp['w_branch_mla'] = nrm((L, MLA_HEADS * MLA_V, D), (MLA_HEADS * MLA_V) ** -0.5)
    inp['w_branch_gla'] = nrm((L, GLA_HEADS * GLA_DV, D), (GLA_HEADS * GLA_DV) ** -0.5)
    inp['w_out'] = nrm((L, D, D), BETA * D ** -0.5)
    inp['mem_w_q'] = nrm((L, D, D), D ** -0.5)
    inp['mem_w_k'] = nrm((L, D, D), D ** -0.5)
    inp['mem_w_v'] = nrm((L, D, D), D ** -0.5)
    inp['mem_w_o'] = nrm((L, D, D), BETA * D ** -0.5)
    inp['w_ffn2_in'] = nrm((L, D, 2 * D_FF), D ** -0.5)
    inp['w_ffn2_out'] = nrm((L, D_FF, D), BETA * D_FF ** -0.5)
    return inp


def reference(x_prompt, x_sample, mem_prompt, cache_pool, cache_mla_ckv, cache_mla_krope, state_gla,
              cache_mem_k, cache_mem_v, ln_g, ln_b, w_ffn1_in, w_ffn1_out, w_in, pool_w, pool_scale,
              mla_q_norm, mla_kv_norm, mla_w_uq, mla_w_uk, mla_w_uv, gla_w_a2, gla_b_a, gla_norm,
              w_branch_pool, w_branch_mla, w_branch_gla, w_out, mem_w_q, mem_w_k, mem_w_v, mem_w_o,
              w_ffn2_in, w_ffn2_out):
    B = x_prompt.shape[0]
    past_len = cache_mla_ckv.shape[2]
    pool_hist0 = jnp.zeros((B, POOL_HIST, POOL_WIDTH), x_prompt.dtype)
    gla0 = jnp.zeros((B, GLA_HEADS, GLA_DK, GLA_DV), jnp.float32)
    xp, xs = x_prompt, x_sample
    pool_p, pool_s, ckv_p, ckv_s, kr_p, kr_s, gla_p, gla_s, mk_list, mv_list = ([] for _ in range(10))
    for l in range(DEPTH):
        lw = {'ln_g': ln_g[l], 'ln_b': ln_b[l], 'w_ffn1_in': w_ffn1_in[l], 'w_ffn1_out': w_ffn1_out[l],
              'w_in': w_in[l], 'pool_w': pool_w[l], 'pool_scale': pool_scale[l],
              'mla_q_norm': mla_q_norm[l], 'mla_kv_norm': mla_kv_norm[l], 'mla_w_uq': mla_w_uq[l],
              'mla_w_uk': mla_w_uk[l], 'mla_w_uv': mla_w_uv[l], 'gla_w_a2': gla_w_a2[l],
              'gla_b_a': gla_b_a[l], 'gla_norm': gla_norm[l], 'w_branch_pool': w_branch_pool[l],
              'w_branch_mla': w_branch_mla[l], 'w_branch_gla': w_branch_gla[l], 'w_out': w_out[l],
              'mem_w_q': mem_w_q[l], 'mem_w_o': mem_w_o[l],
              'w_ffn2_in': w_ffn2_in[l], 'w_ffn2_out': w_ffn2_out[l]}
        mk, mv = memory_kv(mem_prompt, mem_w_k[l], mem_w_v[l])
        xp, np_, nc_, nk_, ng_ = trunk_layer(xp, 0, pool_hist0, gla0, None, None, mk, mv, lw)
        pool_p.append(np_); ckv_p.append(nc_); kr_p.append(nk_); gla_p.append(ng_)
        mk_list.append(mk); mv_list.append(mv)
        xs, ns_, cs_, ks_, gs_ = trunk_layer(xs, past_len, cache_pool[l], state_gla[l], cache_mla_ckv[l],
                                             cache_mla_krope[l], cache_mem_k[l], cache_mem_v[l], lw)
        pool_s.append(ns_); ckv_s.append(cs_); kr_s.append(ks_); gla_s.append(gs_)
    return (xp, xs,
            jnp.stack(pool_p, 0), jnp.stack(pool_s, 0),
            jnp.stack(ckv_p, 0), jnp.stack(ckv_s, 0),
            jnp.stack(kr_p, 0), jnp.stack(kr_s, 0),
            jnp.stack(gla_p, 0), jnp.stack(gla_s, 0),
            jnp.stack(mk_list, 0), jnp.stack(mv_list, 0))
```

```python
import functools
import math

import jax
import jax.numpy as jnp
from jax import lax
from jax.experimental import pallas as pl
from jax.experimental.pallas import tpu as pltpu

F32 = jnp.float32
BF16 = jnp.bfloat16

D_MODEL = 1024
DEPTH = 4
CHUNK = 64
EPS = 1e-5
ALPHA = (2 * DEPTH) ** 0.25

POOL_WINDOWS = (2, 4, 8, 16)
POOL_GROUP = D_MODEL // 16
POOL_WIDTH = len(POOL_WINDOWS) * POOL_GROUP
POOL_HIST = max(POOL_WINDOWS) - 1

MLA_HEADS = 8
MLA_Q_RANK = D_MODEL // 4
MLA_KV_RANK = D_MODEL // 4
MLA_NOPE = 64
MLA_ROPE = 32
MLA_V = 64
MLA_SCALE = (MLA_NOPE + MLA_ROPE) ** -0.5
ROPE_BASE = 10000.0
MLA_HEAD_PAD = 128
MLA_PAIR = 2

GLA_HEADS = 4
GLA_DK = 32
GLA_DV = 64
GLA_GATE_RANK = 16
GLA_GATE_TAU = 16.0
GLA_K = GLA_HEADS * GLA_DK
GLA_V = GLA_HEADS * GLA_DV

MEM_HEADS = 4
MEM_HEAD_DIM = D_MODEL // MEM_HEADS
D_FF = 2816

V7X_LANES = 128
V7X_SUBLANES = 8
V7X_VMEM_BYTES = 64 * 1024 * 1024
VMEM_LIMIT = V7X_VMEM_BYTES - 8 * 1024 * 1024

A_POOL = 0
A_QLAT = 256
A_CKV = 512
A_M1 = 768
A_M2 = 896
A_GLA = 1024
A_COLS = 1792
MISC_W = 128
GLA_IN_W = 2 * GLA_K + 2 * GLA_V


def _cparams(sem):
    return pltpu.CompilerParams(dimension_semantics=sem, vmem_limit_bytes=VMEM_LIMIT)


def _resident(shape, index_map):
    return pl.BlockSpec(shape, index_map, pipeline_mode=pl.Buffered(1))


def _layer_norm(y, g, b):
    mu = jnp.mean(y, axis=-1, keepdims=True)
    d = y - mu
    var = jnp.mean(d * d, axis=-1, keepdims=True)
    return d * lax.rsqrt(var + EPS) * g + b


def _rms_norm(y, g):
    return y * lax.rsqrt(jnp.mean(y * y, axis=-1, keepdims=True) + EPS) * g


def _dot(a, b):
    return jnp.dot(a, b, preferred_element_type=F32)


def _dot_nt(a, b):
    return lax.dot_general(a, b, (((1,), (1,)), ((), ())), preferred_element_type=F32)


def _silu(x):
    return x * jax.nn.sigmoid(x)


FFN_CHUNKS = ((0, 1024), (1024, 1024), (2048, 768))


def _ffn_kernel(x_ref, win_ref, wout_ref, g_ref, b_ref, o_ref, act_ref):
    x = x_ref[...]
    xb = x.astype(BF16)
    for c0, cw in FFN_CHUNKS:
        gate = _dot(xb, win_ref[:, c0:c0 + cw])
        up = _dot(xb, win_ref[:, D_FF + c0:D_FF + c0 + cw])
        act_ref[:, c0:c0 + cw] = (_silu(gate) * up).astype(BF16)
    y = _dot(act_ref[...], wout_ref[...])
    o_ref[...] = _layer_norm(ALPHA * x + 0.5 * y, g_ref[...], b_ref[...])


def _ffn(x, w_in, w_out, ln_g, ln_b, layer, which, tm):
    t = x.shape[0]
    return pl.pallas_call(
        _ffn_kernel,
        grid=(t // tm,),
        in_specs=[
            pl.BlockSpec((tm, D_MODEL), lambda i: (i, 0)),
            _resident((None, D_MODEL, 2 * D_FF), lambda i: (layer, 0, 0)),
            _resident((None, D_FF, D_MODEL), lambda i: (layer, 0, 0)),
            _resident((None, None, 1, D_MODEL), lambda i: (layer, which, 0, 0)),
            _resident((None, None, 1, D_MODEL), lambda i: (layer, which, 0, 0)),
        ],
        out_specs=pl.BlockSpec((tm, D_MODEL), lambda i: (i, 0)),
        out_shape=jax.ShapeDtypeStruct((t, D_MODEL), F32),
        scratch_shapes=[pltpu.VMEM((tm, D_FF), BF16)],
        compiler_params=_cparams(("parallel",)),
        name="ffn",
    )(x, w_in, w_out, ln_g, ln_b)


def _inproj_kernel(x_ref, tab_ref, wa_ref, wq_ref, wkv_ref, qn_ref, kvn_ref,
                   pool_ref, ckv_ref, misc_ref, q_ref, k_ref, v_ref, gla_ref):
    xb = x_ref[...].astype(BF16)
    h = _dot(xb, wa_ref[:, 0:A_GLA])
    gla_ref[...] = _dot(xb, wa_ref[:, A_GLA:A_COLS])
    pool_ref[...] = h[:, A_POOL:A_POOL + POOL_WIDTH]

    cm = tab_ref[:, 0:128]
    sm = tab_ref[:, 128:256]
    misc = h[:, A_M1:A_M1 + MISC_W] * cm + h[:, A_M2:A_M2 + MISC_W] * sm
    misc_ref[...] = misc

    ckv = _rms_norm(h[:, A_CKV:A_CKV + MLA_KV_RANK], kvn_ref[...])
    ckv_ref[...] = ckv
    ckvb = ckv.astype(BF16)
    miscb = misc.astype(BF16)
    kv = _dot(ckvb, wkv_ref[0:MLA_KV_RANK, :]) + _dot(miscb, wkv_ref[MLA_KV_RANK:, :])
    kw = MLA_HEADS * MLA_HEAD_PAD
    k_ref[...] = kv[:, 0:kw].astype(BF16)
    v_ref[...] = kv[:, kw:].astype(BF16)

    qn = _rms_norm(h[:, A_QLAT:A_QLAT + MLA_Q_RANK], qn_ref[...]).astype(BF16)
    q2 = _dot(qn, wq_ref[...])
    cq = tab_ref[:, 256:384]
    sq = tab_ref[:, 384:512]
    for hd in range(MLA_HEADS):
        a = hd * MLA_HEAD_PAD
        q_ref[:, a:a + MLA_HEAD_PAD] = (
            q2[:, a:a + MLA_HEAD_PAD] * cq + q2[:, kw + a:kw + a + MLA_HEAD_PAD] * sq).astype(BF16)


def _inproj(x, tab, w_a, w_q2, w_kv, qnorm, kvnorm, layer, tm, tab_period):
    t = x.shape[0]
    kw = MLA_HEADS * MLA_HEAD_PAD
    vw = MLA_HEADS * MLA_V
    row = lambda i: (i, 0)
    out_shapes = (
        jax.ShapeDtypeStruct((t, POOL_WIDTH), F32),
        jax.ShapeDtypeStruct((t, MLA_KV_RANK), F32),
        jax.ShapeDtypeStruct((t, MISC_W), F32),
        jax.ShapeDtypeStruct((t, kw), BF16),
        jax.ShapeDtypeStruct((t, kw), BF16),
        jax.ShapeDtypeStruct((t, vw), BF16),
        jax.ShapeDtypeStruct((t, GLA_IN_W), F32),
    )
    return pl.pallas_call(
        _inproj_kernel,
        grid=(t // tm,),
        in_specs=[
            pl.BlockSpec((tm, D_MODEL), row),
            pl.BlockSpec((tm, 512), lambda i: (i % tab_period, 0)),
            _resident((None, D_MODEL, A_COLS), lambda i: (layer, 0, 0)),
            _resident((None, MLA_Q_RANK, 2 * kw), lambda i: (layer, 0, 0)),
            _resident((None, MLA_KV_RANK + MISC_W, kw + vw), lambda i: (layer, 0, 0)),
            _resident((None, 1, MLA_Q_RANK), lambda i: (layer, 0, 0)),
            _resident((None, 1, MLA_KV_RANK), lambda i: (layer, 0, 0)),
        ],
        out_specs=[pl.BlockSpec((tm, s.shape[1]), row) for s in out_shapes],
        out_shape=out_shapes,
        compiler_params=_cparams(("parallel",)),
        name="inproj",
    )(x, tab, w_a, w_q2, w_kv, qnorm, kvnorm)


def _kvproj_kernel(ckv_ref, kr_ref, wkv_ref, k_ref, v_ref):
    kv = (_dot(ckv_ref[...].astype(BF16), wkv_ref[0:MLA_KV_RANK, :])
          + _dot(kr_ref[...].astype(BF16), wkv_ref[MLA_KV_RANK:MLA_KV_RANK + MLA_ROPE, :]))
    kw = MLA_HEADS * MLA_HEAD_PAD
    k_ref[...] = kv[:, 0:kw].astype(BF16)
    v_ref[...] = kv[:, kw:].astype(BF16)


def _kvproj(ckv, kr, w_kv, tm):
    nl, rows, _ = ckv.shape
    kw = MLA_HEADS * MLA_HEAD_PAD
    vw = MLA_HEADS * MLA_V
    return pl.pallas_call(
        _kvproj_kernel,
        grid=(nl, rows // tm),
        in_specs=[
            pl.BlockSpec((None, tm, MLA_KV_RANK), lambda l, i: (l, i, 0)),
            pl.BlockSpec((None, tm, MLA_ROPE), lambda l, i: (l, i, 0)),
            pl.BlockSpec((None, MLA_KV_RANK + MISC_W, kw + vw), lambda l, i: (l, 0, 0)),
        ],
        out_specs=[pl.BlockSpec((None, tm, kw), lambda l, i: (l, i, 0)),
                   pl.BlockSpec((None, tm, vw), lambda l, i: (l, i, 0))],
        out_shape=(jax.ShapeDtypeStruct((nl, rows, kw), BF16),
                   jax.ShapeDtypeStruct((nl, rows, vw), BF16)),
        compiler_params=_cparams(("parallel", "parallel")),
        name="kvproj",
    )(ckv, kr, w_kv)


POOL_PAD = 16


def _pool_kernel(u_ref, hist_ref, w_ref, scale_ref, y_ref, ext_ref, *, tp, pos0):
    j = pl.program_id(1)

    @pl.when(j == 0)
    def _():
        ext_ref[0:POOL_PAD, :] = hist_ref[...]

    ext_ref[POOL_PAD:POOL_PAD + tp, :] = u_ref[...]
    tok = ext_ref[POOL_PAD:POOL_PAD + tp, :]
    acc = tok
    sums = {}
    for k in range(1, max(POOL_WINDOWS)):
        acc = acc + ext_ref[POOL_PAD - k:POOL_PAD - k + tp, :]
        if k + 1 in POOL_WINDOWS:
            sums[k + 1] = acc
    lane = lax.broadcasted_iota(jnp.int32, (tp, POOL_WIDTH), 1)
    pos = pos0 + j * tp + lax.broadcasted_iota(jnp.int32, (tp, POOL_WIDTH), 0)
    pooled = None
    for gi, w in enumerate(POOL_WINDOWS):
        cnt = jnp.minimum(pos + 1, w).astype(F32)
        val = sums[w] / cnt
        pooled = val if pooled is None else jnp.where(lane >= gi * POOL_GROUP, val, pooled)
    pooled = pooled - tok
    y = _dot(pooled.astype(BF16), w_ref[...]) * scale_ref[...]
    y_ref[...] = y.astype(BF16)
    ext_ref[0:POOL_PAD, :] = ext_ref[tp:tp + POOL_PAD, :]


def _pool(u, hist16, w_bd, scale, layer, nb, seq, tp, pos0):
    nt = seq // tp
    return pl.pallas_call(
        functools.partial(_pool_kernel, tp=tp, pos0=pos0),
        grid=(nb, nt),
        in_specs=[
            pl.BlockSpec((tp, POOL_WIDTH), lambda b, j: (b * nt + j, 0)),
            pl.BlockSpec((None, POOL_PAD, POOL_WIDTH), lambda b, j: (b, 0, 0)),
            _resident((None, POOL_WIDTH, POOL_WIDTH), lambda b, j: (layer, 0, 0)),
            _resident((None, 1, POOL_WIDTH), lambda b, j: (layer, 0, 0)),
        ],
        out_specs=pl.BlockSpec((tp, POOL_WIDTH), lambda b, j: (b * nt + j, 0)),
        out_shape=jax.ShapeDtypeStruct((nb * seq, POOL_WIDTH), BF16),
        scratch_shapes=[pltpu.VMEM((POOL_PAD + tp, POOL_WIDTH), F32)],
        compiler_params=_cparams(("parallel", "arbitrary")),
        name="pool",
    )(u, hist16, w_bd, scale)


def _softmax_step(s, m_ref, l_ref, acc_ref, v, h):
    m_prev = m_ref[h]
    m_new = jnp.maximum(m_prev, jnp.max(s, axis=-1, keepdims=True))
    corr = jnp.exp(m_prev - m_new)
    p = jnp.exp(s - m_new)
    l_ref[h] = corr * l_ref[h] + jnp.sum(p, axis=-1, keepdims=True)
    acc_ref[h] = corr * acc_ref[h] + _dot(p.astype(BF16), v)
    m_ref[h] = m_new


def _mla_causal_kernel(q_ref, k_ref, v_ref, o_ref, m_ref, l_ref, acc_ref, *, tq):
    i = pl.program_id(2)
    m_ref[...] = jnp.full(m_ref.shape, -jnp.inf, F32)
    l_ref[...] = jnp.zeros(l_ref.shape, F32)
    acc_ref[...] = jnp.zeros(acc_ref.shape, F32)

    def tile(kt, masked):
        r0 = pl.multiple_of(kt * tq, tq)
        v = v_ref[pl.ds(r0, tq), :]
        for h in range(MLA_PAIR):
            a = h * MLA_HEAD_PAD
            s = _dot_nt(q_ref[:, a:a + MLA_HEAD_PAD], k_ref[pl.ds(r0, tq), a:a + MLA_HEAD_PAD])
            if masked:
                qc = lax.broadcasted_iota(jnp.int32, (tq, tq), 0) // CHUNK
                kc = lax.broadcasted_iota(jnp.int32, (tq, tq), 1) // CHUNK
                s = jnp.where(kc <= qc, s, -jnp.inf)
            _softmax_step(s, m_ref, l_ref, acc_ref, v, h)

    def body(kt, carry):
        tile(kt, False)
        return carry

    lax.fori_loop(0, i, body, 0)
    tile(i, True)

    lane = lax.broadcasted_iota(jnp.int32, (tq, MLA_PAIR * MLA_V), 1)
    o0 = acc_ref[0] / l_ref[0]
    o1 = acc_ref[1] / l_ref[1]
    o_ref[...] = jnp.where(lane < MLA_V, o0, o1).astype(BF16)


def _mla_causal(q, k, v, nb, seq, tq):
    nq = seq // tq
    npair = MLA_HEADS // MLA_PAIR
    pw = MLA_PAIR * MLA_HEAD_PAD
    vw = MLA_PAIR * MLA_V
    return pl.pallas_call(
        functools.partial(_mla_causal_kernel, tq=tq),
        grid=(nb, npair, nq),
        in_specs=[
            pl.BlockSpec((tq, pw), lambda b, p, i: (b * nq + i, p)),
            pl.BlockSpec((seq, pw), lambda b, p, i: (b, p)),
            pl.BlockSpec((seq, vw), lambda b, p, i: (b, p)),
        ],
        out_specs=pl.BlockSpec((tq, vw), lambda b, p, i: (b * nq + i, p)),
        out_shape=jax.ShapeDtypeStruct((nb * seq, MLA_HEADS * MLA_V), BF16),
        scratch_shapes=[pltpu.VMEM((MLA_PAIR, tq, 1), F32),
                        pltpu.VMEM((MLA_PAIR, tq, 1), F32),
                        pltpu.VMEM((MLA_PAIR, tq, vw), F32)],
        compiler_params=_cparams(("parallel", "parallel", "arbitrary")),
        name="mla_causal",
    )(q, k, v)


def _mla_cached_kernel(q_ref, kp_ref, vp_ref, kn_ref, vn_ref, o_ref):
    outs = []
    for h in range(MLA_PAIR):
        a = h * MLA_HEAD_PAD
        q = q_ref[:, a:a + MLA_HEAD_PAD]
        s1 = _dot_nt(q, kp_ref[:, a:a + MLA_HEAD_PAD])
        s2 = _dot_nt(q, kn_ref[:, a:a + MLA_HEAD_PAD])
        m = jnp.maximum(jnp.max(s1, axis=-1, keepdims=True), jnp.max(s2, axis=-1, keepdims=True))
        p1 = jnp.exp(s1 - m)
        p2 = jnp.exp(s2 - m)
        l = jnp.sum(p1, axis=-1, keepdims=True) + jnp.sum(p2, axis=-1, keepdims=True)
        o = _dot(p1.astype(BF16), vp_ref[...]) + _dot(p2.astype(BF16), vn_ref[...])
        outs.append(o / l)
    lane = lax.broadcasted_iota(jnp.int32, outs[0].shape, 1)
    o_ref[...] = jnp.where(lane < MLA_V, outs[0], outs[1]).astype(BF16)


def _mla_cached(q, k_new, v_new, k_past, v_past, layer, nb, seq, past):
    npair = MLA_HEADS // MLA_PAIR
    pw = MLA_PAIR * MLA_HEAD_PAD
    vw = MLA_PAIR * MLA_V
    return pl.pallas_call(
        _mla_cached_kernel,
        grid=(nb, npair),
        in_specs=[
            pl.BlockSpec((seq, pw), lambda b, p: (b, p)),
            pl.BlockSpec((None, past, pw), lambda b, p: (layer, b, p)),
            pl.BlockSpec((None, past, vw), lambda b, p: (layer, b, p)),
            pl.BlockSpec((seq, pw), lambda b, p: (b, p)),
            pl.BlockSpec((seq, vw), lambda b, p: (b, p)),
        ],
        out_specs=pl.BlockSpec((seq, vw), lambda b, p: (b, p)),
        out_shape=jax.ShapeDtypeStruct((nb * seq, MLA_HEADS * MLA_V), BF16),
        compiler_params=_cparams(("parallel", "parallel")),
        name="mla_cached",
    )(q, k_past, v_past, k_new, v_new)


def _gla_kernel(gq_ref, gk_ref, gv_ref, gr_ref, misc_ref, wa2_ref, ba_ref, ng_ref, hsel_ref,
                hblk_ref, s0_ref, y_ref, sfin_ref, st_ref, *, tt):
    j = pl.program_id(1)

    @pl.when(j == 0)
    def _():
        st_ref[...] = s0_ref[...]

    q = gq_ref[...] * (GLA_DK ** -0.5)
    k = gk_ref[...]
    v = gv_ref[...]
    vb = v.astype(BF16)
    z = _dot(misc_ref[...].astype(BF16), wa2_ref[...]) + ba_ref[...]
    la = jax.nn.log_sigmoid(z) / GLA_GATE_TAU

    row = lax.broadcasted_iota(jnp.int32, (tt, GLA_K), 0)
    lane_k = lax.broadcasted_iota(jnp.int32, (tt, GLA_K), 1)
    lane_v = lax.broadcasted_iota(jnp.int32, (tt, GLA_V), 1)
    trow = lax.broadcasted_iota(jnp.int32, (tt, tt), 0)
    tcol = lax.broadcasted_iota(jnp.int32, (tt, tt), 1)
    head_k = [lane_k // GLA_DK == h for h in range(GLA_HEADS)]
    head_v = [lane_v // GLA_DV == h for h in range(GLA_HEADS)]

    scores = [jnp.zeros((tt, tt), F32) for _ in range(GLA_HEADS)]
    p = la
    g = la
    m = 1
    while m < tt:
        second = (row & m) != 0
        qe = jnp.where(second, q * jnp.exp(p), 0.0)
        ke = jnp.where(second, 0.0, k * jnp.exp(g - p)).astype(BF16)
        same = (trow // (2 * m)) == (tcol // (2 * m))
        for h in range(GLA_HEADS):
            a = _dot_nt(jnp.where(head_k[h], qe, 0.0).astype(BF16), ke)
            scores[h] = scores[h] + jnp.where(same, a, 0.0)
        up = pltpu.roll(g, m, 0)
        dn = pltpu.roll(g, tt - m, 0)
        p = p + jnp.where(second, up, 0.0)
        g = g + jnp.where(second, up, dn)
        m *= 2

    st = st_ref[...]
    o = _dot_nt((q * jnp.exp(p)).astype(BF16), st.astype(BF16))
    o = o + _dot((q * k).astype(BF16), hsel_ref[...]) * v
    for h in range(GLA_HEADS):
        o = o + jnp.where(head_v[h], _dot(scores[h].astype(BF16), vb), 0.0)

    kdec = (k * jnp.exp(g - p)).astype(BF16)
    srow = lax.broadcasted_iota(jnp.int32, (GLA_V, GLA_K), 0) // GLA_DV
    scol = lax.broadcasted_iota(jnp.int32, (GLA_V, GLA_K), 1) // GLA_DK
    upd = _dot(jnp.transpose(v).astype(BF16), kdec)
    st_new = st * jnp.exp(g[0:1, :]) + jnp.where(srow == scol, upd, 0.0)
    st_ref[...] = st_new
    sfin_ref[...] = st_new

    o2 = o * o
    o2_hi = o2.astype(BF16)
    o2_lo = (o2 - o2_hi.astype(F32)).astype(BF16)
    ms = (_dot(o2_hi, hblk_ref[...]) + _dot(o2_lo, hblk_ref[...])) * (1.0 / GLA_DV)
    y = o * lax.rsqrt(ms + EPS) * ng_ref[...] * _silu(gr_ref[...])
    y_ref[...] = y.astype(BF16)


def _gla(gla_in, misc, w_a2, b_a, norm_g, hsel, hblk, s0t, layer, nb, seq, tt):
    nt = seq // tt
    row = lambda b, j: (b * nt + j, 0)
    return pl.pallas_call(
        functools.partial(_gla_kernel, tt=tt),
        grid=(nb, nt),
        in_specs=[
            pl.BlockSpec((tt, GLA_K), lambda b, j: (b * nt + j, 0)),
            pl.BlockSpec((tt, GLA_K), lambda b, j: (b * nt + j, 1)),
            pl.BlockSpec((tt, GLA_V), lambda b, j: (b * nt + j, 1)),
            pl.BlockSpec((tt, GLA_V), lambda b, j: (b * nt + j, 2)),
            pl.BlockSpec((tt, MISC_W), row),
            _resident((None, MISC_W, GLA_K), lambda b, j: (layer, 0, 0)),
            _resident((None, 1, GLA_K), lambda b, j: (layer, 0, 0)),
            _resident((None, 1, GLA_V), lambda b, j: (layer, 0, 0)),
            _resident((GLA_K, GLA_V), lambda b, j: (0, 0)),
            _resident((GLA_V, GLA_V), lambda b, j: (0, 0)),
            pl.BlockSpec((None, GLA_V, GLA_K), lambda b, j: (b, 0, 0)),
        ],
        out_specs=[pl.BlockSpec((tt, GLA_V), row),
                   pl.BlockSpec((None, GLA_V, GLA_K), lambda b, j: (b, 0, 0))],
        out_shape=(jax.ShapeDtypeStruct((nb * seq, GLA_V), BF16),
                   jax.ShapeDtypeStruct((nb, GLA_V, GLA_K), F32)),
        scratch_shapes=[pltpu.VMEM((GLA_V, GLA_K), F32)],
        compiler_params=_cparams(("parallel", "arbitrary")),
        name="gla",
    )(gla_in, gla_in, gla_in, gla_in, misc, w_a2, b_a, norm_g, hsel, hblk, s0t)


def _merge_kernel(x_ref, yp_ref, ym_ref, yg_ref, wg_ref, wbp_ref, wbm_ref, wbg_ref, wo_ref,
                  g_ref, b_ref, o_ref):
    x = x_ref[...]
    xb = x.astype(BF16)
    m = jax.nn.sigmoid(_dot(xb, wg_ref[:, 0:D_MODEL])) * _dot(yp_ref[...], wbp_ref[...])
    m = m + jax.nn.sigmoid(_dot(xb, wg_ref[:, D_MODEL:2 * D_MODEL])) * _dot(ym_ref[...], wbm_ref[...])
    m = m + jax.nn.sigmoid(_dot(xb, wg_ref[:, 2 * D_MODEL:3 * D_MODEL])) * _dot(yg_ref[...], wbg_ref[...])
    y = _dot(m.astype(BF16), wo_ref[...])
    o_ref[...] = _layer_norm(ALPHA * x + y, g_ref[...], b_ref[...])


def _merge(x, y_pool, y_mla, y_gla, w_gates, w_bp, w_bm, w_bg, w_out, ln_g, ln_b, layer, tm):
    t = x.shape[0]
    row = lambda i: (i, 0)
    lay = lambda i: (layer, 0, 0)
    return pl.pallas_call(
        _merge_kernel,
        grid=(t // tm,),
        in_specs=[
            pl.BlockSpec((tm, D_MODEL), row),
            pl.BlockSpec((tm, POOL_WIDTH), row),
            pl.BlockSpec((tm, MLA_HEADS * MLA_V), row),
            pl.BlockSpec((tm, GLA_V), row),
            _resident((None, D_MODEL, 3 * D_MODEL), lay),
            _resident((None, POOL_WIDTH, D_MODEL), lay),
            _resident((None, MLA_HEADS * MLA_V, D_MODEL), lay),
            _resident((None, GLA_V, D_MODEL), lay),
            _resident((None, D_MODEL, D_MODEL), lay),
            _resident((None, None, 1, D_MODEL), lambda i: (layer, 1, 0, 0)),
            _resident((None, None, 1, D_MODEL), lambda i: (layer, 1, 0, 0)),
        ],
        out_specs=pl.BlockSpec((tm, D_MODEL), row),
        out_shape=jax.ShapeDtypeStruct((t, D_MODEL), F32),
        compiler_params=_cparams(("parallel",)),
        name="merge",
    )(x, y_pool, y_mla, y_gla, w_gates, w_bp, w_bm, w_bg, w_out, ln_g, ln_b)


def _memkv_kernel(m_ref, wk_ref, wv_ref, k_ref, v_ref):
    mb = m_ref[...].astype(BF16)
    k_ref[...] = _dot(mb, wk_ref[...])
    v_ref[...] = _dot(mb, wv_ref[...])


def _memkv(mem, w_k, w_v, tm):
    rows = mem.shape[0]
    nl = w_k.shape[0]
    return pl.pallas_call(
        _memkv_kernel,
        grid=(nl, rows // tm),
        in_specs=[
            pl.BlockSpec((tm, D_MODEL), lambda l, i: (i, 0)),
            pl.BlockSpec((None, D_MODEL, D_MODEL), lambda l, i: (l, 0, 0)),
            pl.BlockSpec((None, D_MODEL, D_MODEL), lambda l, i: (l, 0, 0)),
        ],
        out_specs=[pl.BlockSpec((None, tm, D_MODEL), lambda l, i: (l, i, 0)),
                   pl.BlockSpec((None, tm, D_MODEL), lambda l, i: (l, i, 0))],
        out_shape=(jax.ShapeDtypeStruct((nl, rows, D_MODEL), F32),
                   jax.ShapeDtypeStruct((nl, rows, D_MODEL), F32)),
        compiler_params=_cparams(("parallel", "parallel")),
        name="memkv",
    )(mem, w_k, w_v)


def _memattn_kernel(x_ref, k_ref, v_ref, wq_ref, wo_ref, g_ref, b_ref, o_ref):
    x = x_ref[...]
    q = _dot(x.astype(BF16), wq_ref[...]) * (MEM_HEAD_DIM ** -0.5)
    kb = k_ref[...].astype(BF16)
    vb = v_ref[...].astype(BF16)
    outs = []
    for h in range(MEM_HEADS):
        a = h * MEM_HEAD_DIM
        s = _dot_nt(q[:, a:a + MEM_HEAD_DIM].astype(BF16), kb[:, a:a + MEM_HEAD_DIM])
        s = s - jnp.max(s, axis=-1, keepdims=True)
        p = jnp.exp(s)
        l = jnp.sum(p, axis=-1, keepdims=True)
        outs.append(_dot(p.astype(BF16), vb[:, a:a + MEM_HEAD_DIM]) / l)
    o = jnp.concatenate(outs, axis=-1).astype(BF16)
    y = _dot(o, wo_ref[...])
    o_ref[...] = _layer_norm(ALPHA * x + y, g_ref[...], b_ref[...])


def _memattn(x, mem_k, mem_v, kv_layer, w_q, w_o, ln_g, ln_b, layer, nb, seq, tm):
    nt = seq // tm
    n_mem = mem_k.shape[2]
    lay = lambda b, i: (layer, 0, 0)
    return pl.pallas_call(
        _memattn_kernel,
        grid=(nb, nt),
        in_specs=[
            pl.BlockSpec((tm, D_MODEL), lambda b, i: (b * nt + i, 0)),
            pl.BlockSpec((None, None, n_mem, D_MODEL), lambda b, i: (kv_layer, b, 0, 0)),
            pl.BlockSpec((None, None, n_mem, D_MODEL), lambda b, i: (kv_layer, b, 0, 0)),
            _resident((None, D_MODEL, D_MODEL), lay),
            _resident((None, D_MODEL, D_MODEL), lay),
            _resident((None, None, 1, D_MODEL), lambda b, i: (layer, 2, 0, 0)),
            _resident((None, None, 1, D_MODEL), lambda b, i: (layer, 2, 0, 0)),
        ],
        out_specs=pl.BlockSpec((tm, D_MODEL), lambda b, i: (b * nt + i, 0)),
        out_shape=jax.ShapeDtypeStruct((nb * seq, D_MODEL), F32),
        compiler_params=_cparams(("parallel", "parallel")),
        name="memattn",
    )(x, mem_k, mem_v, w_q, w_o, ln_g, ln_b)


def _pack_weights(w_in, pool_w, mla_w_uq, mla_w_uk, mla_w_uv, gla_w_a2):
    nl = w_in.shape[0]
    splits = (POOL_WIDTH, MLA_Q_RANK, MLA_KV_RANK, MLA_ROPE, GLA_K, GLA_K, GLA_V,
              GLA_GATE_RANK, GLA_V, 3 * D_MODEL)
    offs = [0]
    for s in splits:
        offs.append(offs[-1] + s)
    col = lambda n: w_in[:, :, offs[n]:offs[n + 1]]
    half = MLA_ROPE // 2
    kr = col(3)
    kr_sw = jnp.concatenate([kr[..., half:], kr[..., :half]], -1)
    zeros = lambda n: jnp.zeros((nl, D_MODEL, n), w_in.dtype)
    w_a = jnp.concatenate(
        [col(0), col(1), col(2),
         kr, col(7), zeros(MISC_W - MLA_ROPE - GLA_GATE_RANK),
         kr_sw, zeros(MISC_W - MLA_ROPE),
         col(4), col(5), col(6), col(8)], -1).astype(BF16)
    w_gates = col(9).astype(BF16)

    uq = mla_w_uq.reshape(nl, MLA_Q_RANK, MLA_HEADS, MLA_NOPE + MLA_ROPE)
    uq_n, uq_r = uq[..., :MLA_NOPE], uq[..., MLA_NOPE:]
    uq_rs = jnp.concatenate([uq_r[..., half:], uq_r[..., :half]], -1)
    padq = jnp.zeros((nl, MLA_Q_RANK, MLA_HEADS, MLA_HEAD_PAD - MLA_NOPE - MLA_ROPE), uq.dtype)
    q_main = jnp.concatenate([uq_n, uq_r, padq], -1)
    q_swap = jnp.concatenate([jnp.zeros_like(uq_n), uq_rs, padq], -1)
    kw = MLA_HEADS * MLA_HEAD_PAD
    w_q2 = jnp.concatenate([q_main.reshape(nl, MLA_Q_RANK, kw),
                            q_swap.reshape(nl, MLA_Q_RANK, kw)], -1).astype(BF16)

    padk = jnp.zeros((nl, MLA_KV_RANK, MLA_HEADS, MLA_HEAD_PAD - MLA_NOPE), mla_w_uk.dtype)
    k_lat = jnp.concatenate([mla_w_uk, padk], -1).reshape(nl, MLA_KV_RANK, kw)
    v_lat = mla_w_uv.reshape(nl, MLA_KV_RANK, MLA_HEADS * MLA_V)
    place = jnp.zeros((MISC_W, MLA_HEADS, MLA_HEAD_PAD), F32)
    idx = jnp.arange(MLA_ROPE)
    place = place.at[idx, :, MLA_NOPE + idx].set(1.0).reshape(MISC_W, kw)
    k_misc = jnp.broadcast_to(place, (nl, MISC_W, kw))
    v_misc = jnp.zeros((nl, MISC_W, MLA_HEADS * MLA_V), F32)
    w_kv = jnp.concatenate([jnp.concatenate([k_lat, v_lat], -1),
                            jnp.concatenate([k_misc, v_misc], -1)], 1).astype(BF16)

    ng = len(POOL_WINDOWS)
    eye = jnp.eye(ng, dtype=pool_w.dtype)
    pool_bd = (pool_w[:, :, :, None, :] * eye[None, :, None, :, None]).reshape(
        nl, POOL_WIDTH, POOL_WIDTH).astype(BF16)

    w_a2 = jnp.zeros((nl, MISC_W, GLA_K), gla_w_a2.dtype)
    w_a2 = w_a2.at[:, MLA_ROPE:MLA_ROPE + GLA_GATE_RANK, :].set(gla_w_a2).astype(BF16)
    return w_a, w_gates, w_q2, w_kv, pool_bd, w_a2


def _rope_table(pos):
    half = MLA_ROPE // 2
    inv = ROPE_BASE ** (-jnp.arange(half, dtype=F32) / half)
    ang = pos.astype(F32)[:, None] * inv
    cos, sin = jnp.cos(ang), jnp.sin(ang)
    n = pos.shape[0]
    one = jnp.ones((n, GLA_GATE_RANK), F32)
    cm = jnp.concatenate([cos, cos, one, jnp.zeros((n, MISC_W - MLA_ROPE - GLA_GATE_RANK), F32)], -1)
    sm = jnp.concatenate([-sin, sin, jnp.zeros((n, MISC_W - MLA_ROPE), F32)], -1)
    padq = jnp.zeros((n, MLA_HEAD_PAD - MLA_NOPE - MLA_ROPE), F32)
    cq = jnp.concatenate([jnp.ones((n, MLA_NOPE), F32), cos, cos, padq], -1) * MLA_SCALE
    sq = jnp.concatenate([jnp.zeros((n, MLA_NOPE), F32), -sin, sin, padq], -1) * MLA_SCALE
    return jnp.concatenate([cm, sm, cq, sq], -1)


def _state_to_tiles(s):
    eye = jnp.eye(GLA_HEADS, dtype=s.dtype)
    st = jnp.swapaxes(s, 2, 3)
    bd = st[:, :, :, None, :] * eye[None, :, None, :, None]
    return bd.reshape(s.shape[0], GLA_V, GLA_K)


def _tiles_to_state(t):
    b = t.shape[0]
    t5 = t.reshape(b, GLA_HEADS, GLA_DV, GLA_HEADS, GLA_DK)
    idx = jnp.arange(GLA_HEADS)
    diag = t5[:, idx, :, idx, :]
    return jnp.transpose(diag, (1, 0, 3, 2))


def _tile(n, pref):
    return pref if n % pref == 0 else n


def kernel(x_prompt, x_sample, mem_prompt, cache_pool, cache_mla_ckv, cache_mla_krope, state_gla,
           cache_mem_k, cache_mem_v, ln_g, ln_b, w_ffn1_in, w_ffn1_out, w_in, pool_w, pool_scale,
           mla_q_norm, mla_kv_norm, mla_w_uq, mla_w_uk, mla_w_uv, gla_w_a2, gla_b_a, gla_norm,
           w_branch_pool, w_branch_mla, w_branch_gla, w_out, mem_w_q, mem_w_k, mem_w_v, mem_w_o,
           w_ffn2_in, w_ffn2_out):
    nl = w_in.shape[0]
    bp, sp, _ = x_prompt.shape
    bs, ss, _ = x_sample.shape
    past = cache_mla_ckv.shape[2]
    n_mem = mem_prompt.shape[1]

    w_a, w_gates, w_q2, w_kv, pool_bd, w_a2 = _pack_weights(
        w_in, pool_w, mla_w_uq, mla_w_uk, mla_w_uv, gla_w_a2)
    bf = lambda w: w.astype(BF16)
    ffn_w = ((bf(w_ffn1_in), bf(w_ffn1_out)), (bf(w_ffn2_in), bf(w_ffn2_out)))
    w_bp, w_bm, w_bg, w_o = bf(w_branch_pool), bf(w_branch_mla), bf(w_branch_gla), bf(w_out)
    mw_q, mw_k, mw_v, mw_o = bf(mem_w_q), bf(mem_w_k), bf(mem_w_v), bf(mem_w_o)
    lng = ln_g.reshape(nl, 4, 1, D_MODEL)
    lnb = ln_b.reshape(nl, 4, 1, D_MODEL)
    qnorm = mla_q_norm.reshape(nl, 1, MLA_Q_RANK)
    kvnorm = mla_kv_norm.reshape(nl, 1, MLA_KV_RANK)
    pscale = pool_scale.reshape(nl, 1, POOL_WIDTH)
    b_a = gla_b_a.reshape(nl, 1, GLA_K)
    norm_g = jnp.tile(gla_norm, (1, GLA_HEADS)).reshape(nl, 1, GLA_V)
    hsel = (jnp.arange(GLA_K)[:, None] // GLA_DK == jnp.arange(GLA_V)[None, :] // GLA_DV).astype(BF16)
    hblk = (jnp.arange(GLA_V)[:, None] // GLA_DV == jnp.arange(GLA_V)[None, :] // GLA_DV).astype(BF16)

    tmem = _tile(bp * n_mem, 512)
    mem_k, mem_v = _memkv(mem_prompt.reshape(bp * n_mem, D_MODEL), mw_k, mw_v, tmem)
    mem_k = mem_k.reshape(nl, bp, n_mem, D_MODEL)
    mem_v = mem_v.reshape(nl, bp, n_mem, D_MODEL)
    k_past, v_past = _kvproj(cache_mla_ckv.reshape(nl, bs * past, MLA_KV_RANK),
                             cache_mla_krope.reshape(nl, bs * past, MLA_ROPE), w_kv,
                             _tile(bs * past, 1024))
    smem_k = cache_mem_k.reshape(nl, bs, n_mem, D_MODEL)
    smem_v = cache_mem_v.reshape(nl, bs, n_mem, D_MODEL)

    groups = {
        "p": dict(nb=bp, seq=sp, pos0=0, tm=_tile(bp * sp, 512)),
        "s": dict(nb=bs, seq=ss, pos0=past, tm=_tile(bs * ss, 512)),
    }
    for gname, gr in groups.items():
        seq, tm = gr["seq"], gr["tm"]
        pos = gr["pos0"] + jnp.arange(seq)
        tab = _rope_table(pos)
        if tm <= seq:
            gr["tab"], gr["tab_period"] = tab, seq // tm
        else:
            gr["tab"], gr["tab_period"] = jnp.tile(tab, (tm // seq, 1)), 1
        gr["tseq"] = _tile(seq, 512)
        gr["tgla"] = _tile(seq, 256)

    xs = {"p": x_prompt.reshape(bp * sp, D_MODEL), "s": x_sample.reshape(bs * ss, D_MODEL)}
    zero_hist = jnp.zeros((bp, POOL_PAD, POOL_WIDTH), F32)
    zero_state = jnp.zeros((bp, GLA_V, GLA_K), F32)
    outs = {n: {"pool": [], "ckv": [], "kr": [], "gla": []} for n in groups}

    for l in range(nl):
        for gname, gr in groups.items():
            nb, seq, tm = gr["nb"], gr["seq"], gr["tm"]
            x = xs[gname]
            x = _ffn(x, ffn_w[0][0], ffn_w[0][1], lng, lnb, l, 0, tm)
            u_pool, ckv, misc, q, k, v, gla_in = _inproj(
                x, gr["tab"], w_a, w_q2, w_kv, qnorm, kvnorm, l, tm, gr["tab_period"])
            if gname == "p":
                hist = None
                hist16 = zero_hist
                s0t = zero_state
            else:
                hist = cache_pool[l]
                hist16 = jnp.concatenate(
                    [jnp.zeros((nb, POOL_PAD - POOL_HIST, POOL_WIDTH), F32), hist], 1)
                s0t = _state_to_tiles(state_gla[l])
            y_pool = _pool(u_pool, hist16, pool_bd, pscale, l, nb, seq, gr["tseq"], gr["pos0"])
            if gname == "p":
                y_mla = _mla_causal(q, k, v, nb, seq, gr["tseq"])
            else:
                y_mla = _mla_cached(q, k, v, k_past, v_past, l, nb, seq, past)
            y_gla, s_fin = _gla(gla_in, misc, w_a2, b_a, norm_g, hsel, hblk, s0t, l, nb, seq, gr["tgla"])
            x = _merge(x, y_pool, y_mla, y_gla, w_gates, w_bp, w_bm, w_bg, w_o, lng, lnb, l, tm)
            if gname == "p":
                x = _memattn(x, mem_k, mem_v, l, mw_q, mw_o, lng, lnb, l, nb, seq, gr["tseq"])
            else:
                x = _memattn(x, smem_k, smem_v, l, mw_q, mw_o, lng, lnb, l, nb, seq, gr["tseq"])
            x = _ffn(x, ffn_w[1][0], ffn_w[1][1], lng, lnb, l, 3, tm)
            xs[gname] = x

            u3 = u_pool.reshape(nb, seq, POOL_WIDTH)
            if hist is None:
                hist = jnp.zeros((nb, POOL_HIST, POOL_WIDTH), F32)
            new_pool = jnp.concatenate([hist, u3], 1)[:, -POOL_HIST:]
            o = outs[gname]
            o["pool"].append(new_pool)
            o["ckv"].append(ckv.reshape(nb, seq, MLA_KV_RANK))
            o["kr"].append(misc[:, :MLA_ROPE].reshape(nb, seq, MLA_ROPE))
            o["gla"].append(_tiles_to_state(s_fin))

    st = lambda n, key: jnp.stack(outs[n][key], 0)
    return (xs["p"].reshape(bp, sp, D_MODEL), xs["s"].reshape(bs, ss, D_MODEL),
            st("p", "pool"), st("s", "pool"),
            st("p", "ckv"), st("s", "ckv"),
            st("p", "kr"), st("s", "kr"),
            st("p", "gla"), st("s", "gla"),
            mem_k.reshape(nl, bp, n_mem, MEM_HEADS, MEM_HEAD_DIM),
            mem_v.reshape(nl, bp, n_mem, MEM_HEADS, MEM_HEAD_DIM))
```

```python
import functools
import math

import jax
import jax.numpy as jnp
from jax import lax
from jax.experimental import pallas as pl
from jax.experimental.pallas import tpu as pltpu

F32 = jnp.float32
BF16 = jnp.bfloat16

D_MODEL = 1024
DEPTH = 4
CHUNK = 64
EPS = 1e-5
ALPHA = (2 * DEPTH) ** 0.25

POOL_WINDOWS = (2, 4, 8, 16)
POOL_GROUP = D_MODEL // 16
POOL_WIDTH = len(POOL_WINDOWS) * POOL_GROUP
POOL_HIST = max(POOL_WINDOWS) - 1

MLA_HEADS = 8
MLA_Q_RANK = D_MODEL // 4
MLA_KV_RANK = D_MODEL // 4
MLA_NOPE = 64
MLA_ROPE = 32
MLA_V = 64
MLA_SCALE = (MLA_NOPE + MLA_ROPE) ** -0.5
ROPE_BASE = 10000.0
MLA_HEAD_PAD = 128
MLA_PAIR = 2
V_ONE_EVEN = MLA_V
V_ONE_ODD = MLA_HEAD_PAD
LOG2E = math.log2(math.e)

GLA_HEADS = 4
GLA_DK = 32
GLA_DV = 64
GLA_GATE_RANK = 16
GLA_GATE_TAU = 16.0
GLA_K = GLA_HEADS * GLA_DK
GLA_V = GLA_HEADS * GLA_DV

MEM_HEADS = 4
MEM_HEAD_DIM = D_MODEL // MEM_HEADS
D_FF = 2816

V7X_LANES = 128
V7X_SUBLANES = 8
V7X_VMEM_BYTES = 64 * 1024 * 1024
VMEM_LIMIT = V7X_VMEM_BYTES - 8 * 1024 * 1024

A_POOL = 0
A_QLAT = 256
A_CKV = 512
A_M1 = 768
A_M2 = 896
A_GLA = 1024
A_COLS = 1792
MISC_W = 128
GLA_IN_W = 2 * GLA_K + 2 * GLA_V


def _cparams(sem):
    return pltpu.CompilerParams(dimension_semantics=sem, vmem_limit_bytes=VMEM_LIMIT)


def _resident(shape, index_map):
    return pl.BlockSpec(shape, index_map, pipeline_mode=pl.Buffered(1))


def _layer_norm(y, g, b):
    mu = jnp.mean(y, axis=-1, keepdims=True)
    d = y - mu
    var = jnp.mean(d * d, axis=-1, keepdims=True)
    return d * lax.rsqrt(var + EPS) * g + b


def _rms_norm(y, g):
    return y * lax.rsqrt(jnp.mean(y * y, axis=-1, keepdims=True) + EPS) * g


def _dot(a, b):
    return jnp.dot(a, b, preferred_element_type=F32)


def _dot_nt(a, b):
    return lax.dot_general(a, b, (((1,), (1,)), ((), ())), preferred_element_type=F32)


def _silu(x):
    return x * jax.nn.sigmoid(x)


FFN_CHUNKS = ((0, 1024), (1024, 1024), (2048, 768))


def _ffn_kernel(x_ref, win_ref, wout_ref, g_ref, b_ref, o_ref, act_ref):
    x = x_ref[...]
    xb = x.astype(BF16)
    for c0, cw in FFN_CHUNKS:
        gate = _dot(xb, win_ref[:, c0:c0 + cw])
        up = _dot(xb, win_ref[:, D_FF + c0:D_FF + c0 + cw])
        act_ref[:, c0:c0 + cw] = (_silu(gate) * up).astype(BF16)
    y = _dot(act_ref[...], wout_ref[...])
    o_ref[...] = _layer_norm(ALPHA * x + 0.5 * y, g_ref[...], b_ref[...])


def _ffn(x, w_in, w_out, ln_g, ln_b, layer, which, tm):
    t = x.shape[0]
    return pl.pallas_call(
        _ffn_kernel,
        grid=(t // tm,),
        in_specs=[
            pl.BlockSpec((tm, D_MODEL), lambda i: (i, 0)),
            _resident((None, D_MODEL, 2 * D_FF), lambda i: (layer, 0, 0)),
            _resident((None, D_FF, D_MODEL), lambda i: (layer, 0, 0)),
            _resident((None, None, 1, D_MODEL), lambda i: (layer, which, 0, 0)),
            _resident((None, None, 1, D_MODEL), lambda i: (layer, which, 0, 0)),
        ],
        out_specs=pl.BlockSpec((tm, D_MODEL), lambda i: (i, 0)),
        out_shape=jax.ShapeDtypeStruct((t, D_MODEL), F32),
        scratch_shapes=[pltpu.VMEM((tm, D_FF), BF16)],
        compiler_params=_cparams(("parallel",)),
        name="ffn",
    )(x, w_in, w_out, ln_g, ln_b)


def _v_with_ones(vv):
    lane = lax.broadcasted_iota(jnp.int32, vv.shape, 1) % (MLA_PAIR * MLA_HEAD_PAD)
    return jnp.where((lane == V_ONE_EVEN) | (lane == V_ONE_ODD), 1.0, vv).astype(BF16)


def _inproj_kernel(x_ref, tab_ref, wa_ref, wq_ref, wk_ref, wv_ref, qn_ref, kvn_ref,
                   pool_ref, ckv_ref, misc_ref, q_ref, k_ref, v_ref, gla_ref, *, k_transposed):
    xb = x_ref[...].astype(BF16)
    h = _dot(xb, wa_ref[:, 0:A_GLA])
    gla_ref[...] = _dot(xb, wa_ref[:, A_GLA:A_COLS])
    pool_ref[...] = h[:, A_POOL:A_POOL + POOL_WIDTH]

    cm = tab_ref[:, 0:128]
    sm = tab_ref[:, 128:256]
    misc = h[:, A_M1:A_M1 + MISC_W] * cm + h[:, A_M2:A_M2 + MISC_W] * sm
    misc_ref[...] = misc

    ckv = _rms_norm(h[:, A_CKV:A_CKV + MLA_KV_RANK], kvn_ref[...])
    ckv_ref[...] = ckv
    ckvb = ckv.astype(BF16)
    lat = jnp.concatenate([ckvb, misc.astype(BF16)], axis=-1)
    if k_transposed:
        k_ref[...] = _dot_nt(wk_ref[...], lat).astype(BF16)
    else:
        k_ref[...] = _dot(lat, wk_ref[...]).astype(BF16)
    v_ref[...] = _v_with_ones(_dot(ckvb, wv_ref[...]))

    qn = _rms_norm(h[:, A_QLAT:A_QLAT + MLA_Q_RANK], qn_ref[...]).astype(BF16)
    q2 = _dot(qn, wq_ref[...])
    kw = MLA_HEADS * MLA_HEAD_PAD
    cq = tab_ref[:, 256:384]
    sq = tab_ref[:, 384:512]
    for hd in range(MLA_HEADS):
        a = hd * MLA_HEAD_PAD
        q_ref[:, a:a + MLA_HEAD_PAD] = (
            q2[:, a:a + MLA_HEAD_PAD] * cq + q2[:, kw + a:kw + a + MLA_HEAD_PAD] * sq).astype(BF16)


def _inproj(x, tab, w_a, w_q2, w_k, w_v, qnorm, kvnorm, layer, tm, tab_period, k_transposed):
    t = x.shape[0]
    kw = MLA_HEADS * MLA_HEAD_PAD
    lat_w = MLA_KV_RANK + MISC_W
    row = lambda i: (i, 0)
    if k_transposed:
        k_shape = jax.ShapeDtypeStruct((t // tm, kw, tm), BF16)
        k_spec = pl.BlockSpec((None, kw, tm), lambda i: (i, 0, 0))
        wk_spec = _resident((None, kw, lat_w), lambda i: (layer, 0, 0))
    else:
        k_shape = jax.ShapeDtypeStruct((t, kw), BF16)
        k_spec = pl.BlockSpec((tm, kw), row)
        wk_spec = _resident((None, lat_w, kw), lambda i: (layer, 0, 0))
    out_shapes = (
        jax.ShapeDtypeStruct((t, POOL_WIDTH), F32),
        jax.ShapeDtypeStruct((t, MLA_KV_RANK), F32),
        jax.ShapeDtypeStruct((t, MISC_W), F32),
        jax.ShapeDtypeStruct((t, kw), BF16),
        k_shape,
        jax.ShapeDtypeStruct((t, kw), BF16),
        jax.ShapeDtypeStruct((t, GLA_IN_W), F32),
    )
    out_specs = [pl.BlockSpec((tm, s.shape[-1]), row) for s in out_shapes]
    out_specs[4] = k_spec
    return pl.pallas_call(
        functools.partial(_inproj_kernel, k_transposed=k_transposed),
        grid=(t // tm,),
        in_specs=[
            pl.BlockSpec((tm, D_MODEL), row),
            pl.BlockSpec((tm, 512), lambda i: (i % tab_period, 0)),
            _resident((None, D_MODEL, A_COLS), lambda i: (layer, 0, 0)),
            _resident((None, MLA_Q_RANK, 2 * kw), lambda i: (layer, 0, 0)),
            wk_spec,
            _resident((None, MLA_KV_RANK, kw), lambda i: (layer, 0, 0)),
            _resident((None, 1, MLA_Q_RANK), lambda i: (layer, 0, 0)),
            _resident((None, 1, MLA_KV_RANK), lambda i: (layer, 0, 0)),
        ],
        out_specs=out_specs,
        out_shape=out_shapes,
        compiler_params=_cparams(("parallel",)),
        name="inproj",
    )(x, tab, w_a, w_q2, w_k, w_v, qnorm, kvnorm)


def _kvproj_kernel(ckv_ref, kr_ref, wk_ref, wv_ref, k_ref, v_ref):
    ckvb = ckv_ref[...].astype(BF16)
    k = (_dot(ckvb, wk_ref[0:MLA_KV_RANK, :])
         + _dot(kr_ref[...].astype(BF16), wk_ref[MLA_KV_RANK:MLA_KV_RANK + MLA_ROPE, :]))
    k_ref[...] = k.astype(BF16)
    v_ref[...] = _v_with_ones(_dot(ckvb, wv_ref[...]))


def _kvproj(ckv, kr, w_k, w_v, tm):
    nl, rows, _ = ckv.shape
    kw = MLA_HEADS * MLA_HEAD_PAD
    return pl.pallas_call(
        _kvproj_kernel,
        grid=(nl, rows // tm),
        in_specs=[
            pl.BlockSpec((None, tm, MLA_KV_RANK), lambda l, i: (l, i, 0)),
            pl.BlockSpec((None, tm, MLA_ROPE), lambda l, i: (l, i, 0)),
            pl.BlockSpec((None, MLA_KV_RANK + MISC_W, kw), lambda l, i: (l, 0, 0)),
            pl.BlockSpec((None, MLA_KV_RANK, kw), lambda l, i: (l, 0, 0)),
        ],
        out_specs=[pl.BlockSpec((None, tm, kw), lambda l, i: (l, i, 0)),
                   pl.BlockSpec((None, tm, kw), lambda l, i: (l, i, 0))],
        out_shape=(jax.ShapeDtypeStruct((nl, rows, kw), BF16),
                   jax.ShapeDtypeStruct((nl, rows, kw), BF16)),
        compiler_params=_cparams(("parallel", "parallel")),
        name="kvproj",
    )(ckv, kr, w_k, w_v)


POOL_PAD = 16


def _pool_kernel(u_ref, hist_ref, w_ref, scale_ref, y_ref, ext_ref, *, tp, pos0):
    j = pl.program_id(1)

    @pl.when(j == 0)
    def _():
        ext_ref[0:POOL_PAD, :] = hist_ref[...]

    ext_ref[POOL_PAD:POOL_PAD + tp, :] = u_ref[...]
    tok = ext_ref[POOL_PAD:POOL_PAD + tp, :]
    acc = tok
    sums = {}
    for k in range(1, max(POOL_WINDOWS)):
        acc = acc + ext_ref[POOL_PAD - k:POOL_PAD - k + tp, :]
        if k + 1 in POOL_WINDOWS:
            sums[k + 1] = acc
    lane = lax.broadcasted_iota(jnp.int32, (tp, POOL_WIDTH), 1)
    pos = pos0 + j * tp + lax.broadcasted_iota(jnp.int32, (tp, POOL_WIDTH), 0)
    pooled = None
    for gi, w in enumerate(POOL_WINDOWS):
        cnt = jnp.minimum(pos + 1, w).astype(F32)
        val = sums[w] / cnt
        pooled = val if pooled is None else jnp.where(lane >= gi * POOL_GROUP, val, pooled)
    pooled = pooled - tok
    y = _dot(pooled.astype(BF16), w_ref[...]) * scale_ref[...]
    y_ref[...] = y.astype(BF16)
    ext_ref[0:POOL_PAD, :] = ext_ref[tp:tp + POOL_PAD, :]


def _pool(u, hist16, w_bd, scale, layer, nb, seq, tp, pos0):
    nt = seq // tp
    return pl.pallas_call(
        functools.partial(_pool_kernel, tp=tp, pos0=pos0),
        grid=(nb, nt),
        in_specs=[
            pl.BlockSpec((tp, POOL_WIDTH), lambda b, j: (b * nt + j, 0)),
            pl.BlockSpec((None, POOL_PAD, POOL_WIDTH), lambda b, j: (b, 0, 0)),
            _resident((None, POOL_WIDTH, POOL_WIDTH), lambda b, j: (layer, 0, 0)),
            _resident((None, 1, POOL_WIDTH), lambda b, j: (layer, 0, 0)),
        ],
        out_specs=pl.BlockSpec((tp, POOL_WIDTH), lambda b, j: (b * nt + j, 0)),
        out_shape=jax.ShapeDtypeStruct((nb * seq, POOL_WIDTH), BF16),
        scratch_shapes=[pltpu.VMEM((POOL_PAD + tp, POOL_WIDTH), F32)],
        compiler_params=_cparams(("parallel", "arbitrary")),
        name="pool",
    )(u, hist16, w_bd, scale)


def _pair_output(acc0, acc1):
    lane = lax.broadcasted_iota(jnp.int32, acc0.shape, 1)
    o0 = acc0 / acc0[:, V_ONE_EVEN:V_ONE_EVEN + 1]
    o1 = acc1 / acc1[:, 0:1]
    return jnp.where(lane < MLA_V, o0, o1).astype(BF16)


def _mla_causal_kernel(q_ref, kt_ref, v_ref, o_ref, m_ref, acc_ref, *, tq):
    i = pl.program_id(2)
    m_ref[...] = jnp.full(m_ref.shape, -jnp.inf, F32)
    acc_ref[...] = jnp.zeros(acc_ref.shape, F32)
    reps = tq // V7X_LANES

    def tile(kt, masked):
        r0 = pl.multiple_of(kt * tq, tq)
        for h in range(MLA_PAIR):
            a = h * MLA_HEAD_PAD
            s = _dot(q_ref[:, a:a + MLA_HEAD_PAD], kt_ref[kt, a:a + MLA_HEAD_PAD, :])
            if masked:
                qc = lax.broadcasted_iota(jnp.int32, (tq, tq), 0) // CHUNK
                kc = lax.broadcasted_iota(jnp.int32, (tq, tq), 1) // CHUNK
                s = jnp.where(kc <= qc, s, -jnp.inf)
            m_prev = m_ref[h]
            m_new = jnp.maximum(m_prev, jnp.max(s, axis=-1, keepdims=True))
            p = jnp.exp2(s - jnp.tile(m_new, (1, reps)))
            corr = jnp.exp2(m_prev - m_new)
            acc_ref[h] = corr * acc_ref[h] + _dot(p.astype(BF16),
                                                  v_ref[pl.ds(r0, tq), a:a + MLA_HEAD_PAD])
            m_ref[h] = m_new

    def body(kt, carry):
        tile(kt, False)
        return carry

    lax.fori_loop(0, i, body, 0)
    tile(i, True)
    o_ref[...] = _pair_output(acc_ref[0], acc_ref[1])


def _mla_causal(q, kt, v, nb, seq, tq):
    nq = seq // tq
    npair = MLA_HEADS // MLA_PAIR
    pw = MLA_PAIR * MLA_HEAD_PAD
    return pl.pallas_call(
        functools.partial(_mla_causal_kernel, tq=tq),
        grid=(nb, npair, nq),
        in_specs=[
            pl.BlockSpec((tq, pw), lambda b, p, i: (b * nq + i, p)),
            pl.BlockSpec((nq, pw, tq), lambda b, p, i: (b, p, 0)),
            pl.BlockSpec((seq, pw), lambda b, p, i: (b, p)),
        ],
        out_specs=pl.BlockSpec((tq, MLA_PAIR * MLA_V), lambda b, p, i: (b * nq + i, p)),
        out_shape=jax.ShapeDtypeStruct((nb * seq, MLA_HEADS * MLA_V), BF16),
        scratch_shapes=[pltpu.VMEM((MLA_PAIR, tq, V7X_LANES), F32),
                        pltpu.VMEM((MLA_PAIR, tq, MLA_HEAD_PAD), F32)],
        compiler_params=_cparams(("parallel", "parallel", "arbitrary")),
        name="mla_causal",
    )(q, kt, v)


def _mla_cached_kernel(q_ref, kp_ref, vp_ref, kn_ref, vn_ref, o_ref):
    accs = []
    for h in range(MLA_PAIR):
        a = h * MLA_HEAD_PAD
        q = q_ref[:, a:a + MLA_HEAD_PAD]
        s1 = _dot_nt(q, kp_ref[:, a:a + MLA_HEAD_PAD])
        s2 = _dot_nt(q, kn_ref[:, a:a + MLA_HEAD_PAD])
        m = jnp.maximum(jnp.max(s1, axis=-1, keepdims=True), jnp.max(s2, axis=-1, keepdims=True))
        p1 = jnp.exp2(s1 - m)
        p2 = jnp.exp2(s2 - m)
        accs.append(_dot(p1.astype(BF16), vp_ref[:, a:a + MLA_HEAD_PAD])
                    + _dot(p2.astype(BF16), vn_ref[:, a:a + MLA_HEAD_PAD]))
    o_ref[...] = _pair_output(accs[0], accs[1])


def _mla_cached(q, k_new, v_new, k_past, v_past, layer, nb, seq, past):
    npair = MLA_HEADS // MLA_PAIR
    pw = MLA_PAIR * MLA_HEAD_PAD
    return pl.pallas_call(
        _mla_cached_kernel,
        grid=(nb, npair),
        in_specs=[
            pl.BlockSpec((seq, pw), lambda b, p: (b, p)),
            pl.BlockSpec((None, past, pw), lambda b, p: (layer, b, p)),
            pl.BlockSpec((None, past, pw), lambda b, p: (layer, b, p)),
            pl.BlockSpec((seq, pw), lambda b, p: (b, p)),
            pl.BlockSpec((seq, pw), lambda b, p: (b, p)),
        ],
        out_specs=pl.BlockSpec((seq, MLA_PAIR * MLA_V), lambda b, p: (b, p)),
        out_shape=jax.ShapeDtypeStruct((nb * seq, MLA_HEADS * MLA_V), BF16),
        compiler_params=_cparams(("parallel", "parallel")),
        name="mla_cached",
    )(q, k_past, v_past, k_new, v_new)


def _gla_kernel(gq_ref, gk_ref, gv_ref, gr_ref, misc_ref, wa2_ref, ba_ref, ng_ref, hsel_ref,
                hblk_ref, s0_ref, y_ref, sfin_ref, st_ref, *, tt):
    j = pl.program_id(1)

    @pl.when(j == 0)
    def _():
        st_ref[...] = s0_ref[...]

    q = gq_ref[...] * (GLA_DK ** -0.5)
    k = gk_ref[...]
    v = gv_ref[...]
    vb = v.astype(BF16)
    z = _dot(misc_ref[...].astype(BF16), wa2_ref[...]) + ba_ref[...]
    la = jax.nn.log_sigmoid(z) / GLA_GATE_TAU

    row = lax.broadcasted_iota(jnp.int32, (tt, GLA_K), 0)
    lane_k = lax.broadcasted_iota(jnp.int32, (tt, GLA_K), 1)
    lane_v = lax.broadcasted_iota(jnp.int32, (tt, GLA_V), 1)
    trow = lax.broadcasted_iota(jnp.int32, (tt, tt), 0)
    tcol = lax.broadcasted_iota(jnp.int32, (tt, tt), 1)
    head_k = [lane_k // GLA_DK == h for h in range(GLA_HEADS)]
    head_v = [lane_v // GLA_DV == h for h in range(GLA_HEADS)]

    scores = [jnp.zeros((tt, tt), F32) for _ in range(GLA_HEADS)]
    p = la
    g = la
    m = 1
    while m < tt:
        second = (row & m) != 0
        qe = jnp.where(second, q * jnp.exp(p), 0.0)
        ke = jnp.where(second, 0.0, k * jnp.exp(g - p)).astype(BF16)
        same = (trow // (2 * m)) == (tcol // (2 * m))
        for h in range(GLA_HEADS):
            a = _dot_nt(jnp.where(head_k[h], qe, 0.0).astype(BF16), ke)
            scores[h] = scores[h] + jnp.where(same, a, 0.0)
        up = pltpu.roll(g, m, 0)
        dn = pltpu.roll(g, tt - m, 0)
        p = p + jnp.where(second, up, 0.0)
        g = g + jnp.where(second, up, dn)
        m *= 2

    st = st_ref[...]
    o = _dot_nt((q * jnp.exp(p)).astype(BF16), st.astype(BF16))
    o = o + _dot((q * k).astype(BF16), hsel_ref[...]) * v
    for h in range(GLA_HEADS):
        o = o + jnp.where(head_v[h], _dot(scores[h].astype(BF16), vb), 0.0)

    kdec = (k * jnp.exp(g - p)).astype(BF16)
    srow = lax.broadcasted_iota(jnp.int32, (GLA_V, GLA_K), 0) // GLA_DV
    scol = lax.broadcasted_iota(jnp.int32, (GLA_V, GLA_K), 1) // GLA_DK
    upd = _dot(jnp.transpose(v).astype(BF16), kdec)
    st_new = st * jnp.exp(g[0:1, :]) + jnp.where(srow == scol, upd, 0.0)
    st_ref[...] = st_new
    sfin_ref[...] = st_new

    o2 = o * o
    o2_hi = o2.astype(BF16)
    o2_lo = (o2 - o2_hi.astype(F32)).astype(BF16)
    ms = (_dot(o2_hi, hblk_ref[...]) + _dot(o2_lo, hblk_ref[...])) * (1.0 / GLA_DV)
    y = o * lax.rsqrt(ms + EPS) * ng_ref[...] * _silu(gr_ref[...])
    y_ref[...] = y.astype(BF16)


def _gla(gla_in, misc, w_a2, b_a, norm_g, hsel, hblk, s0t, layer, nb, seq, tt):
    nt = seq // tt
    row = lambda b, j: (b * nt + j, 0)
    return pl.pallas_call(
        functools.partial(_gla_kernel, tt=tt),
        grid=(nb, nt),
        in_specs=[
            pl.BlockSpec((tt, GLA_K), lambda b, j: (b * nt + j, 0)),
            pl.BlockSpec((tt, GLA_K), lambda b, j: (b * nt + j, 1)),
            pl.BlockSpec((tt, GLA_V), lambda b, j: (b * nt + j, 1)),
            pl.BlockSpec((tt, GLA_V), lambda b, j: (b * nt + j, 2)),
            pl.BlockSpec((tt, MISC_W), row),
            _resident((None, MISC_W, GLA_K), lambda b, j: (layer, 0, 0)),
            _resident((None, 1, GLA_K), lambda b, j: (layer, 0, 0)),
            _resident((None, 1, GLA_V), lambda b, j: (layer, 0, 0)),
            _resident((GLA_K, GLA_V), lambda b, j: (0, 0)),
            _resident((GLA_V, GLA_V), lambda b, j: (0, 0)),
            pl.BlockSpec((None, GLA_V, GLA_K), lambda b, j: (b, 0, 0)),
        ],
        out_specs=[pl.BlockSpec((tt, GLA_V), row),
                   pl.BlockSpec((None, GLA_V, GLA_K), lambda b, j: (b, 0, 0))],
        out_shape=(jax.ShapeDtypeStruct((nb * seq, GLA_V), BF16),
                   jax.ShapeDtypeStruct((nb, GLA_V, GLA_K), F32)),
        scratch_shapes=[pltpu.VMEM((GLA_V, GLA_K), F32)],
        compiler_params=_cparams(("parallel", "arbitrary")),
        name="gla",
    )(gla_in, gla_in, gla_in, gla_in, misc, w_a2, b_a, norm_g, hsel, hblk, s0t)


def _merge_kernel(x_ref, yp_ref, ym_ref, yg_ref, wg_ref, wbp_ref, wbm_ref, wbg_ref, wo_ref,
                  g_ref, b_ref, o_ref):
    x = x_ref[...]
    xb = x.astype(BF16)
    m = jax.nn.sigmoid(_dot(xb, wg_ref[:, 0:D_MODEL])) * _dot(yp_ref[...], wbp_ref[...])
    m = m + jax.nn.sigmoid(_dot(xb, wg_ref[:, D_MODEL:2 * D_MODEL])) * _dot(ym_ref[...], wbm_ref[...])
    m = m + jax.nn.sigmoid(_dot(xb, wg_ref[:, 2 * D_MODEL:3 * D_MODEL])) * _dot(yg_ref[...], wbg_ref[...])
    y = _dot(m.astype(BF16), wo_ref[...])
    o_ref[...] = _layer_norm(ALPHA * x + y, g_ref[...], b_ref[...])


def _merge(x, y_pool, y_mla, y_gla, w_gates, w_bp, w_bm, w_bg, w_out, ln_g, ln_b, layer, tm):
    t = x.shape[0]
    row = lambda i: (i, 0)
    lay = lambda i: (layer, 0, 0)
    return pl.pallas_call(
        _merge_kernel,
        grid=(t // tm,),
        in_specs=[
            pl.BlockSpec((tm, D_MODEL), row),
            pl.BlockSpec((tm, POOL_WIDTH), row),
            pl.BlockSpec((tm, MLA_HEADS * MLA_V), row),
            pl.BlockSpec((tm, GLA_V), row),
            _resident((None, D_MODEL, 3 * D_MODEL), lay),
            _resident((None, POOL_WIDTH, D_MODEL), lay),
            _resident((None, MLA_HEADS * MLA_V, D_MODEL), lay),
            _resident((None, GLA_V, D_MODEL), lay),
            _resident((None, D_MODEL, D_MODEL), lay),
            _resident((None, None, 1, D_MODEL), lambda i: (layer, 1, 0, 0)),
            _resident((None, None, 1, D_MODEL), lambda i: (layer, 1, 0, 0)),
        ],
        out_specs=pl.BlockSpec((tm, D_MODEL), row),
        out_shape=jax.ShapeDtypeStruct((t, D_MODEL), F32),
        compiler_params=_cparams(("parallel",)),
        name="merge",
    )(x, y_pool, y_mla, y_gla, w_gates, w_bp, w_bm, w_bg, w_out, ln_g, ln_b)


def _memkv_kernel(m_ref, wk_ref, wv_ref, k_ref, v_ref):
    mb = m_ref[...].astype(BF16)
    k_ref[...] = _dot(mb, wk_ref[...])
    v_ref[...] = _dot(mb, wv_ref[...])


def _memkv(mem, w_k, w_v, tm):
    rows = mem.shape[0]
    nl = w_k.shape[0]
    return pl.pallas_call(
        _memkv_kernel,
        grid=(nl, rows // tm),
        in_specs=[
            pl.BlockSpec((tm, D_MODEL), lambda l, i: (i, 0)),
            pl.BlockSpec((None, D_MODEL, D_MODEL), lambda l, i: (l, 0, 0)),
            pl.BlockSpec((None, D_MODEL, D_MODEL), lambda l, i: (l, 0, 0)),
        ],
        out_specs=[pl.BlockSpec((None, tm, D_MODEL), lambda l, i: (l, i, 0)),
                   pl.BlockSpec((None, tm, D_MODEL), lambda l, i: (l, i, 0))],
        out_shape=(jax.ShapeDtypeStruct((nl, rows, D_MODEL), F32),
                   jax.ShapeDtypeStruct((nl, rows, D_MODEL), F32)),
        compiler_params=_cparams(("parallel", "parallel")),
        name="memkv",
    )(mem, w_k, w_v)


def _memattn_kernel(x_ref, k_ref, v_ref, wq_ref, wo_ref, g_ref, b_ref, o_ref):
    x = x_ref[...]
    q = _dot(x.astype(BF16), wq_ref[...]) * (MEM_HEAD_DIM ** -0.5)
    kb = k_ref[...].astype(BF16)
    vb = v_ref[...].astype(BF16)
    outs = []
    for h in range(MEM_HEADS):
        a = h * MEM_HEAD_DIM
        s = _dot_nt(q[:, a:a + MEM_HEAD_DIM].astype(BF16), kb[:, a:a + MEM_HEAD_DIM])
        s = s - jnp.max(s, axis=-1, keepdims=True)
        p = jnp.exp(s)
        l = jnp.sum(p, axis=-1, keepdims=True)
        outs.append(_dot(p.astype(BF16), vb[:, a:a + MEM_HEAD_DIM]) / l)
    o = jnp.concatenate(outs, axis=-1).astype(BF16)
    y = _dot(o, wo_ref[...])
    o_ref[...] = _layer_norm(ALPHA * x + y, g_ref[...], b_ref[...])


def _memattn(x, mem_k, mem_v, kv_layer, w_q, w_o, ln_g, ln_b, layer, nb, seq, tm):
    nt = seq // tm
    n_mem = mem_k.shape[2]
    lay = lambda b, i: (layer, 0, 0)
    return pl.pallas_call(
        _memattn_kernel,
        grid=(nb, nt),
        in_specs=[
            pl.BlockSpec((tm, D_MODEL), lambda b, i: (b * nt + i, 0)),
            pl.BlockSpec((None, None, n_mem, D_MODEL), lambda b, i: (kv_layer, b, 0, 0)),
            pl.BlockSpec((None, None, n_mem, D_MODEL), lambda b, i: (kv_layer, b, 0, 0)),
            _resident((None, D_MODEL, D_MODEL), lay),
            _resident((None, D_MODEL, D_MODEL), lay),
            _resident((None, None, 1, D_MODEL), lambda b, i: (layer, 2, 0, 0)),
            _resident((None, None, 1, D_MODEL), lambda b, i: (layer, 2, 0, 0)),
        ],
        out_specs=pl.BlockSpec((tm, D_MODEL), lambda b, i: (b * nt + i, 0)),
        out_shape=jax.ShapeDtypeStruct((nb * seq, D_MODEL), F32),
        compiler_params=_cparams(("parallel", "parallel")),
        name="memattn",
    )(x, mem_k, mem_v, w_q, w_o, ln_g, ln_b)


def _pack_weights(w_in, pool_w, mla_w_uq, mla_w_uk, mla_w_uv, gla_w_a2):
    nl = w_in.shape[0]
    splits = (POOL_WIDTH, MLA_Q_RANK, MLA_KV_RANK, MLA_ROPE, GLA_K, GLA_K, GLA_V,
              GLA_GATE_RANK, GLA_V, 3 * D_MODEL)
    offs = [0]
    for s in splits:
        offs.append(offs[-1] + s)
    col = lambda n: w_in[:, :, offs[n]:offs[n + 1]]
    half = MLA_ROPE // 2
    kr = col(3)
    kr_sw = jnp.concatenate([kr[..., half:], kr[..., :half]], -1)
    zeros = lambda n: jnp.zeros((nl, D_MODEL, n), w_in.dtype)
    w_a = jnp.concatenate(
        [col(0), col(1), col(2),
         kr, col(7), zeros(MISC_W - MLA_ROPE - GLA_GATE_RANK),
         kr_sw, zeros(MISC_W - MLA_ROPE),
         col(4), col(5), col(6), col(8)], -1).astype(BF16)
    w_gates = col(9).astype(BF16)

    uq = mla_w_uq.reshape(nl, MLA_Q_RANK, MLA_HEADS, MLA_NOPE + MLA_ROPE)
    uq_n, uq_r = uq[..., :MLA_NOPE], uq[..., MLA_NOPE:]
    uq_rs = jnp.concatenate([uq_r[..., half:], uq_r[..., :half]], -1)
    padq = jnp.zeros((nl, MLA_Q_RANK, MLA_HEADS, MLA_HEAD_PAD - MLA_NOPE - MLA_ROPE), uq.dtype)
    q_main = jnp.concatenate([uq_n, uq_r, padq], -1)
    q_swap = jnp.concatenate([jnp.zeros_like(uq_n), uq_rs, padq], -1)
    kw = MLA_HEADS * MLA_HEAD_PAD
    w_q2 = jnp.concatenate([q_main.reshape(nl, MLA_Q_RANK, kw),
                            q_swap.reshape(nl, MLA_Q_RANK, kw)], -1).astype(BF16)

    padk = jnp.zeros((nl, MLA_KV_RANK, MLA_HEADS, MLA_HEAD_PAD - MLA_NOPE), mla_w_uk.dtype)
    k_lat = jnp.concatenate([mla_w_uk, padk], -1).reshape(nl, MLA_KV_RANK, kw)
    place = jnp.zeros((MISC_W, MLA_HEADS, MLA_HEAD_PAD), F32)
    idx = jnp.arange(MLA_ROPE)
    place = place.at[idx, :, MLA_NOPE + idx].set(1.0).reshape(MISC_W, kw)
    k_misc = jnp.broadcast_to(place, (nl, MISC_W, kw))
    w_k = jnp.concatenate([k_lat, k_misc], 1).astype(BF16)
    w_kt = jnp.swapaxes(w_k, 1, 2)
    uv = mla_w_uv.reshape(nl, MLA_KV_RANK, MLA_HEADS // MLA_PAIR, MLA_PAIR, MLA_V)
    padv = jnp.zeros_like(uv[:, :, :, 0])
    w_v = jnp.stack([jnp.concatenate([uv[:, :, :, 0], padv], -1),
                     jnp.concatenate([padv, uv[:, :, :, 1]], -1)], 3)
    w_v = w_v.reshape(nl, MLA_KV_RANK, kw).astype(BF16)

    ng = len(POOL_WINDOWS)
    eye = jnp.eye(ng, dtype=pool_w.dtype)
    pool_bd = (pool_w[:, :, :, None, :] * eye[None, :, None, :, None]).reshape(
        nl, POOL_WIDTH, POOL_WIDTH).astype(BF16)

    w_a2 = jnp.zeros((nl, MISC_W, GLA_K), gla_w_a2.dtype)
    w_a2 = w_a2.at[:, MLA_ROPE:MLA_ROPE + GLA_GATE_RANK, :].set(gla_w_a2).astype(BF16)
    return w_a, w_gates, w_q2, w_k, w_kt, w_v, pool_bd, w_a2


def _rope_table(pos):
    half = MLA_ROPE // 2
    inv = ROPE_BASE ** (-jnp.arange(half, dtype=F32) / half)
    ang = pos.astype(F32)[:, None] * inv
    cos, sin = jnp.cos(ang), jnp.sin(ang)
    n = pos.shape[0]
    one = jnp.ones((n, GLA_GATE_RANK), F32)
    cm = jnp.concatenate([cos, cos, one, jnp.zeros((n, MISC_W - MLA_ROPE - GLA_GATE_RANK), F32)], -1)
    sm = jnp.concatenate([-sin, sin, jnp.zeros((n, MISC_W - MLA_ROPE), F32)], -1)
    padq = jnp.zeros((n, MLA_HEAD_PAD - MLA_NOPE - MLA_ROPE), F32)
    qs = MLA_SCALE * LOG2E
    cq = jnp.concatenate([jnp.ones((n, MLA_NOPE), F32), cos, cos, padq], -1) * qs
    sq = jnp.concatenate([jnp.zeros((n, MLA_NOPE), F32), -sin, sin, padq], -1) * qs
    return jnp.concatenate([cm, sm, cq, sq], -1)


def _state_to_tiles(s):
    eye = jnp.eye(GLA_HEADS, dtype=s.dtype)
    st = jnp.swapaxes(s, 2, 3)
    bd = st[:, :, :, None, :] * eye[None, :, None, :, None]
    return bd.reshape(s.shape[0], GLA_V, GLA_K)


def _tiles_to_state(t):
    b = t.shape[0]
    t5 = t.reshape(b, GLA_HEADS, GLA_DV, GLA_HEADS, GLA_DK)
    idx = jnp.arange(GLA_HEADS)
    diag = t5[:, idx, :, idx, :]
    return jnp.transpose(diag, (1, 0, 3, 2))


def _tile(n, pref):
    return pref if n % pref == 0 else n


def kernel(x_prompt, x_sample, mem_prompt, cache_pool, cache_mla_ckv, cache_mla_krope, state_gla,
           cache_mem_k, cache_mem_v, ln_g, ln_b, w_ffn1_in, w_ffn1_out, w_in, pool_w, pool_scale,
           mla_q_norm, mla_kv_norm, mla_w_uq, mla_w_uk, mla_w_uv, gla_w_a2, gla_b_a, gla_norm,
           w_branch_pool, w_branch_mla, w_branch_gla, w_out, mem_w_q, mem_w_k, mem_w_v, mem_w_o,
           w_ffn2_in, w_ffn2_out):
    nl = w_in.shape[0]
    bp, sp, _ = x_prompt.shape
    bs, ss, _ = x_sample.shape
    past = cache_mla_ckv.shape[2]
    n_mem = mem_prompt.shape[1]

    w_a, w_gates, w_q2, w_k, w_kt, w_v, pool_bd, w_a2 = _pack_weights(
        w_in, pool_w, mla_w_uq, mla_w_uk, mla_w_uv, gla_w_a2)
    bf = lambda w: w.astype(BF16)
    ffn_w = ((bf(w_ffn1_in), bf(w_ffn1_out)), (bf(w_ffn2_in), bf(w_ffn2_out)))
    w_bp, w_bm, w_bg, w_o = bf(w_branch_pool), bf(w_branch_mla), bf(w_branch_gla), bf(w_out)
    mw_q, mw_k, mw_v, mw_o = bf(mem_w_q), bf(mem_w_k), bf(mem_w_v), bf(mem_w_o)
    lng = ln_g.reshape(nl, 4, 1, D_MODEL)
    lnb = ln_b.reshape(nl, 4, 1, D_MODEL)
    qnorm = mla_q_norm.reshape(nl, 1, MLA_Q_RANK)
    kvnorm = mla_kv_norm.reshape(nl, 1, MLA_KV_RANK)
    pscale = pool_scale.reshape(nl, 1, POOL_WIDTH)
    b_a = gla_b_a.reshape(nl, 1, GLA_K)
    norm_g = jnp.tile(gla_norm, (1, GLA_HEADS)).reshape(nl, 1, GLA_V)
    hsel = (jnp.arange(GLA_K)[:, None] // GLA_DK == jnp.arange(GLA_V)[None, :] // GLA_DV).astype(BF16)
    hblk = (jnp.arange(GLA_V)[:, None] // GLA_DV == jnp.arange(GLA_V)[None, :] // GLA_DV).astype(BF16)

    tmem = _tile(bp * n_mem, 512)
    mem_k, mem_v = _memkv(mem_prompt.reshape(bp * n_mem, D_MODEL), mw_k, mw_v, tmem)
    mem_k = mem_k.reshape(nl, bp, n_mem, D_MODEL)
    mem_v = mem_v.reshape(nl, bp, n_mem, D_MODEL)
    k_past, v_past = _kvproj(cache_mla_ckv.reshape(nl, bs * past, MLA_KV_RANK),
                             cache_mla_krope.reshape(nl, bs * past, MLA_ROPE), w_k, w_v,
                             _tile(bs * past, 1024))
    smem_k = cache_mem_k.reshape(nl, bs, n_mem, D_MODEL)
    smem_v = cache_mem_v.reshape(nl, bs, n_mem, D_MODEL)

    groups = {
        "p": dict(nb=bp, seq=sp, pos0=0, tm=_tile(bp * sp, 512)),
        "s": dict(nb=bs, seq=ss, pos0=past, tm=_tile(bs * ss, 512)),
    }
    for gname, gr in groups.items():
        seq, tm = gr["seq"], gr["tm"]
        pos = gr["pos0"] + jnp.arange(seq)
        tab = _rope_table(pos)
        if tm <= seq:
            gr["tab"], gr["tab_period"] = tab, seq // tm
        else:
            gr["tab"], gr["tab_period"] = jnp.tile(tab, (tm // seq, 1)), 1
        gr["tseq"] = _tile(seq, 512)
        gr["tgla"] = _tile(seq, 256)

    xs = {"p": x_prompt.reshape(bp * sp, D_MODEL), "s": x_sample.reshape(bs * ss, D_MODEL)}
    zero_hist = jnp.zeros((bp, POOL_PAD, POOL_WIDTH), F32)
    zero_state = jnp.zeros((bp, GLA_V, GLA_K), F32)
    outs = {n: {"pool": [], "ckv": [], "kr": [], "gla": []} for n in groups}

    for l in range(nl):
        for gname, gr in groups.items():
            nb, seq, tm = gr["nb"], gr["seq"], gr["tm"]
            x = xs[gname]
            x = _ffn(x, ffn_w[0][0], ffn_w[0][1], lng, lnb, l, 0, tm)
            k_transposed = gname == "p"
            u_pool, ckv, misc, q, k, v, gla_in = _inproj(
                x, gr["tab"], w_a, w_q2, w_kt if k_transposed else w_k, w_v, qnorm, kvnorm,
                l, tm, gr["tab_period"], k_transposed)
            if gname == "p":
                hist = None
                hist16 = zero_hist
                s0t = zero_state
            else:
                hist = cache_pool[l]
                hist16 = jnp.concatenate(
                    [jnp.zeros((nb, POOL_PAD - POOL_HIST, POOL_WIDTH), F32), hist], 1)
                s0t = _state_to_tiles(state_gla[l])
            y_pool = _pool(u_pool, hist16, pool_bd, pscale, l, nb, seq, gr["tseq"], gr["pos0"])
            if gname == "p":
                y_mla = _mla_causal(q, k, v, nb, seq, gr["tseq"])
            else:
                y_mla = _mla_cached(q, k, v, k_past, v_past, l, nb, seq, past)
            y_gla, s_fin = _gla(gla_in, misc, w_a2, b_a, norm_g, hsel, hblk, s0t, l, nb, seq, gr["tgla"])
            x = _merge(x, y_pool, y_mla, y_gla, w_gates, w_bp, w_bm, w_bg, w_o, lng, lnb, l, tm)
            if gname == "p":
                x = _memattn(x, mem_k, mem_v, l, mw_q, mw_o, lng, lnb, l, nb, seq, gr["tseq"])
            else:
                x = _memattn(x, smem_k, smem_v, l, mw_q, mw_o, lng, lnb, l, nb, seq, gr["tseq"])
            x = _ffn(x, ffn_w[1][0], ffn_w[1][1], lng, lnb, l, 3, tm)
            xs[gname] = x

            u3 = u_pool.reshape(nb, seq, POOL_WIDTH)
            if hist is None:
                hist = jnp.zeros((nb, POOL_HIST, POOL_WIDTH), F32)
            new_pool = jnp.concatenate([hist, u3], 1)[:, -POOL_HIST:]
            o = outs[gname]
            o["pool"].append(new_pool)
            o["ckv"].append(ckv.reshape(nb, seq, MLA_KV_RANK))
            o["kr"].append(misc[:, :MLA_ROPE].reshape(nb, seq, MLA_ROPE))
            o["gla"].append(_tiles_to_state(s_fin))

    st = lambda n, key: jnp.stack(outs[n][key], 0)
    return (xs["p"].reshape(bp, sp, D_MODEL), xs["s"].reshape(bs, ss, D_MODEL),
            st("p", "pool"), st("s", "pool"),
            st("p", "ckv"), st("s", "ckv"),
            st("p", "kr"), st("s", "kr"),
            st("p", "gla"), st("s", "gla"),
            mem_k.reshape(nl, bp, n_mem, MEM_HEADS, MEM_HEAD_DIM),
            mem_v.reshape(nl, bp, n_mem, MEM_HEADS, MEM_HEAD_DIM))
```

```python
import functools
import math

import jax
import jax.numpy as jnp
from jax import lax
from jax.experimental import pallas as pl
from jax.experimental.pallas import tpu as pltpu

F32 = jnp.float32
BF16 = jnp.bfloat16

D_MODEL = 1024
DEPTH = 4
CHUNK = 64
EPS = 1e-5
ALPHA = (2 * DEPTH) ** 0.25

POOL_WINDOWS = (2, 4, 8, 16)
POOL_GROUP = D_MODEL // 16
POOL_WIDTH = len(POOL_WINDOWS) * POOL_GROUP
POOL_HIST = max(POOL_WINDOWS) - 1

MLA_HEADS = 8
MLA_Q_RANK = D_MODEL // 4
MLA_KV_RANK = D_MODEL // 4
MLA_NOPE = 64
MLA_ROPE = 32
MLA_V = 64
MLA_SCALE = (MLA_NOPE + MLA_ROPE) ** -0.5
ROPE_BASE = 10000.0
MLA_HEAD_PAD = 128
MLA_PAIR = 2
MLA_GROUP = 4
V_ONE_EVEN = MLA_V
V_ONE_ODD = MLA_HEAD_PAD
LOG2E = math.log2(math.e)

GLA_HEADS = 4
GLA_DK = 32
GLA_DV = 64
GLA_GATE_RANK = 16
GLA_GATE_TAU = 16.0
GLA_K = GLA_HEADS * GLA_DK
GLA_V = GLA_HEADS * GLA_DV

MEM_HEADS = 4
MEM_HEAD_DIM = D_MODEL // MEM_HEADS
D_FF = 2816

V7X_LANES = 128
V7X_SUBLANES = 8
V7X_VMEM_BYTES = 64 * 1024 * 1024
VMEM_LIMIT = V7X_VMEM_BYTES - 8 * 1024 * 1024

A_POOL = 0
A_QLAT = 256
A_CKV = 512
A_M1 = 768
A_M2 = 896
A_GLA = 1024
A_COLS = 1792
MISC_W = 128
GLA_IN_W = 2 * GLA_K + 2 * GLA_V


def _cparams(sem):
    return pltpu.CompilerParams(dimension_semantics=sem, vmem_limit_bytes=VMEM_LIMIT)


def _resident(shape, index_map):
    return pl.BlockSpec(shape, index_map, pipeline_mode=pl.Buffered(1))


def _layer_norm(y, g, b):
    mu = jnp.mean(y, axis=-1, keepdims=True)
    d = y - mu
    var = jnp.mean(d * d, axis=-1, keepdims=True)
    return d * lax.rsqrt(var + EPS) * g + b


def _rms_norm(y, g):
    return y * lax.rsqrt(jnp.mean(y * y, axis=-1, keepdims=True) + EPS) * g


def _dot(a, b):
    return jnp.dot(a, b, preferred_element_type=F32)


def _dot_nt(a, b):
    return lax.dot_general(a, b, (((1,), (1,)), ((), ())), preferred_element_type=F32)


def _silu(x):
    return x * jax.nn.sigmoid(x)


FFN_CHUNKS = ((0, 1024), (1024, 1024), (2048, 768))


def _ffn_kernel(x_ref, win_ref, wout_ref, g_ref, b_ref, o_ref, act_ref):
    x = x_ref[...]
    xb = x.astype(BF16)
    for c0, cw in FFN_CHUNKS:
        gate = _dot(xb, win_ref[:, c0:c0 + cw])
        up = _dot(xb, win_ref[:, D_FF + c0:D_FF + c0 + cw])
        act_ref[:, c0:c0 + cw] = (_silu(gate) * up).astype(BF16)
    y = _dot(act_ref[...], wout_ref[...])
    o_ref[...] = _layer_norm(ALPHA * x + 0.5 * y, g_ref[...], b_ref[...])


def _ffn(x, w_in, w_out, ln_g, ln_b, layer, which, tm):
    t = x.shape[0]
    return pl.pallas_call(
        _ffn_kernel,
        grid=(t // tm,),
        in_specs=[
            pl.BlockSpec((tm, D_MODEL), lambda i: (i, 0)),
            _resident((None, D_MODEL, 2 * D_FF), lambda i: (layer, 0, 0)),
            _resident((None, D_FF, D_MODEL), lambda i: (layer, 0, 0)),
            _resident((None, None, 1, D_MODEL), lambda i: (layer, which, 0, 0)),
            _resident((None, None, 1, D_MODEL), lambda i: (layer, which, 0, 0)),
        ],
        out_specs=pl.BlockSpec((tm, D_MODEL), lambda i: (i, 0)),
        out_shape=jax.ShapeDtypeStruct((t, D_MODEL), F32),
        scratch_shapes=[pltpu.VMEM((tm, D_FF), BF16)],
        compiler_params=_cparams(("parallel",)),
        name="ffn",
    )(x, w_in, w_out, ln_g, ln_b)


def _v_with_ones(vv):
    lane = lax.broadcasted_iota(jnp.int32, vv.shape, 1) % (MLA_PAIR * MLA_HEAD_PAD)
    return jnp.where((lane == V_ONE_EVEN) | (lane == V_ONE_ODD), 1.0, vv).astype(BF16)


def _inproj_kernel(x_ref, tab_ref, wa_ref, wq_ref, wk_ref, wv_ref, qn_ref, kvn_ref,
                   pool_ref, ckv_ref, misc_ref, q_ref, k_ref, v_ref, gla_ref, *, k_transposed):
    xb = x_ref[...].astype(BF16)
    h = _dot(xb, wa_ref[:, 0:A_GLA])
    gla_ref[...] = _dot(xb, wa_ref[:, A_GLA:A_COLS])
    pool_ref[...] = h[:, A_POOL:A_POOL + POOL_WIDTH]

    cm = tab_ref[:, 0:128]
    sm = tab_ref[:, 128:256]
    misc = h[:, A_M1:A_M1 + MISC_W] * cm + h[:, A_M2:A_M2 + MISC_W] * sm
    misc_ref[...] = misc

    ckv = _rms_norm(h[:, A_CKV:A_CKV + MLA_KV_RANK], kvn_ref[...])
    ckv_ref[...] = ckv
    ckvb = ckv.astype(BF16)
    lat = jnp.concatenate([ckvb, misc.astype(BF16)], axis=-1)
    if k_transposed:
        k_ref[...] = _dot_nt(wk_ref[...], lat).astype(BF16)
    else:
        k_ref[...] = _dot(lat, wk_ref[...]).astype(BF16)
    v_ref[...] = _v_with_ones(_dot(ckvb, wv_ref[...]))

    qn = _rms_norm(h[:, A_QLAT:A_QLAT + MLA_Q_RANK], qn_ref[...]).astype(BF16)
    q2 = _dot(qn, wq_ref[...])
    kw = MLA_HEADS * MLA_HEAD_PAD
    cq = tab_ref[:, 256:384]
    sq = tab_ref[:, 384:512]
    for hd in range(MLA_HEADS):
        a = hd * MLA_HEAD_PAD
        q_ref[:, a:a + MLA_HEAD_PAD] = (
            q2[:, a:a + MLA_HEAD_PAD] * cq + q2[:, kw + a:kw + a + MLA_HEAD_PAD] * sq).astype(BF16)


def _inproj(x, tab, w_a, w_q2, w_k, w_v, qnorm, kvnorm, layer, tm, tab_period, k_transposed):
    t = x.shape[0]
    kw = MLA_HEADS * MLA_HEAD_PAD
    lat_w = MLA_KV_RANK + MISC_W
    row = lambda i: (i, 0)
    if k_transposed:
        k_shape = jax.ShapeDtypeStruct((t // tm, kw, tm), BF16)
        k_spec = pl.BlockSpec((None, kw, tm), lambda i: (i, 0, 0))
        wk_spec = _resident((None, kw, lat_w), lambda i: (layer, 0, 0))
    else:
        k_shape = jax.ShapeDtypeStruct((t, kw), BF16)
        k_spec = pl.BlockSpec((tm, kw), row)
        wk_spec = _resident((None, lat_w, kw), lambda i: (layer, 0, 0))
    out_shapes = (
        jax.ShapeDtypeStruct((t, POOL_WIDTH), F32),
        jax.ShapeDtypeStruct((t, MLA_KV_RANK), F32),
        jax.ShapeDtypeStruct((t, MISC_W), F32),
        jax.ShapeDtypeStruct((t, kw), BF16),
        k_shape,
        jax.ShapeDtypeStruct((t, kw), BF16),
        jax.ShapeDtypeStruct((t, GLA_IN_W), F32),
    )
    out_specs = [pl.BlockSpec((tm, s.shape[-1]), row) for s in out_shapes]
    out_specs[4] = k_spec
    return pl.pallas_call(
        functools.partial(_inproj_kernel, k_transposed=k_transposed),
        grid=(t // tm,),
        in_specs=[
            pl.BlockSpec((tm, D_MODEL), row),
            pl.BlockSpec((tm, 512), lambda i: (i % tab_period, 0)),
            _resident((None, D_MODEL, A_COLS), lambda i: (layer, 0, 0)),
            _resident((None, MLA_Q_RANK, 2 * kw), lambda i: (layer, 0, 0)),
            wk_spec,
            _resident((None, MLA_KV_RANK, kw), lambda i: (layer, 0, 0)),
            _resident((None, 1, MLA_Q_RANK), lambda i: (layer, 0, 0)),
            _resident((None, 1, MLA_KV_RANK), lambda i: (layer, 0, 0)),
        ],
        out_specs=out_specs,
        out_shape=out_shapes,
        compiler_params=_cparams(("parallel",)),
        name="inproj",
    )(x, tab, w_a, w_q2, w_k, w_v, qnorm, kvnorm)


def _kvproj_kernel(ckv_ref, kr_ref, wk_ref, wv_ref, k_ref, v_ref):
    ckvb = ckv_ref[...].astype(BF16)
    k = (_dot(ckvb, wk_ref[0:MLA_KV_RANK, :])
         + _dot(kr_ref[...].astype(BF16), wk_ref[MLA_KV_RANK:MLA_KV_RANK + MLA_ROPE, :]))
    k_ref[...] = k.astype(BF16)
    v_ref[...] = _v_with_ones(_dot(ckvb, wv_ref[...]))


def _kvproj(ckv, kr, w_k, w_v, tm):
    nl, rows, _ = ckv.shape
    kw = MLA_HEADS * MLA_HEAD_PAD
    return pl.pallas_call(
        _kvproj_kernel,
        grid=(nl, rows // tm),
        in_specs=[
            pl.BlockSpec((None, tm, MLA_KV_RANK), lambda l, i: (l, i, 0)),
            pl.BlockSpec((None, tm, MLA_ROPE), lambda l, i: (l, i, 0)),
            pl.BlockSpec((None, MLA_KV_RANK + MISC_W, kw), lambda l, i: (l, 0, 0)),
            pl.BlockSpec((None, MLA_KV_RANK, kw), lambda l, i: (l, 0, 0)),
        ],
        out_specs=[pl.BlockSpec((None, tm, kw), lambda l, i: (l, i, 0)),
                   pl.BlockSpec((None, tm, kw), lambda l, i: (l, i, 0))],
        out_shape=(jax.ShapeDtypeStruct((nl, rows, kw), BF16),
                   jax.ShapeDtypeStruct((nl, rows, kw), BF16)),
        compiler_params=_cparams(("parallel", "parallel")),
        name="kvproj",
    )(ckv, kr, w_k, w_v)


POOL_PAD = 16


def _pool_kernel(u_ref, hist_ref, w_ref, scale_ref, y_ref, ext_ref, *, tp, pos0):
    j = pl.program_id(1)

    @pl.when(j == 0)
    def _():
        ext_ref[0:POOL_PAD, :] = hist_ref[...]

    ext_ref[POOL_PAD:POOL_PAD + tp, :] = u_ref[...]
    tok = ext_ref[POOL_PAD:POOL_PAD + tp, :]
    acc = tok
    sums = {}
    for k in range(1, max(POOL_WINDOWS)):
        acc = acc + ext_ref[POOL_PAD - k:POOL_PAD - k + tp, :]
        if k + 1 in POOL_WINDOWS:
            sums[k + 1] = acc
    lane = lax.broadcasted_iota(jnp.int32, (tp, POOL_WIDTH), 1)
    pos = pos0 + j * tp + lax.broadcasted_iota(jnp.int32, (tp, POOL_WIDTH), 0)
    pooled = None
    for gi, w in enumerate(POOL_WINDOWS):
        cnt = jnp.minimum(pos + 1, w).astype(F32)
        val = sums[w] / cnt
        pooled = val if pooled is None else jnp.where(lane >= gi * POOL_GROUP, val, pooled)
    pooled = pooled - tok
    y = _dot(pooled.astype(BF16), w_ref[...]) * scale_ref[...]
    y_ref[...] = y.astype(BF16)
    ext_ref[0:POOL_PAD, :] = ext_ref[tp:tp + POOL_PAD, :]


def _pool(u, hist16, w_bd, scale, layer, nb, seq, tp, pos0):
    nt = seq // tp
    return pl.pallas_call(
        functools.partial(_pool_kernel, tp=tp, pos0=pos0),
        grid=(nb, nt),
        in_specs=[
            pl.BlockSpec((tp, POOL_WIDTH), lambda b, j: (b * nt + j, 0)),
            pl.BlockSpec((None, POOL_PAD, POOL_WIDTH), lambda b, j: (b, 0, 0)),
            _resident((None, POOL_WIDTH, POOL_WIDTH), lambda b, j: (layer, 0, 0)),
            _resident((None, 1, POOL_WIDTH), lambda b, j: (layer, 0, 0)),
        ],
        out_specs=pl.BlockSpec((tp, POOL_WIDTH), lambda b, j: (b * nt + j, 0)),
        out_shape=jax.ShapeDtypeStruct((nb * seq, POOL_WIDTH), BF16),
        scratch_shapes=[pltpu.VMEM((POOL_PAD + tp, POOL_WIDTH), F32)],
        compiler_params=_cparams(("parallel", "arbitrary")),
        name="pool",
    )(u, hist16, w_bd, scale)


def _pair_output(acc0, acc1):
    lane = lax.broadcasted_iota(jnp.int32, acc0.shape, 1)
    o0 = acc0 / acc0[:, V_ONE_EVEN:V_ONE_EVEN + 1]
    o1 = acc1 / acc1[:, 0:1]
    return jnp.where(lane < MLA_V, o0, o1).astype(BF16)


def _mla_causal_kernel(q_ref, kt_ref, v_ref, o_ref, m_ref, acc_ref, sa_ref, sb_ref, *, tq):
    i = pl.program_id(2)
    m_ref[...] = jnp.full(m_ref.shape, -jnp.inf, F32)
    acc_ref[...] = jnp.zeros(acc_ref.shape, F32)
    reps = tq // V7X_LANES

    def scores(kt, dst_ref):
        for h in range(MLA_GROUP):
            a = h * MLA_HEAD_PAD
            dst_ref[h] = _dot(q_ref[:, a:a + MLA_HEAD_PAD], kt_ref[kt, a:a + MLA_HEAD_PAD, :])

    def consume(src_ref, kt, masked):
        r0 = pl.multiple_of(kt * tq, tq)
        for h in range(MLA_GROUP):
            a = h * MLA_HEAD_PAD
            s = src_ref[h]
            if masked:
                qc = lax.broadcasted_iota(jnp.int32, (tq, tq), 0) // CHUNK
                kc = lax.broadcasted_iota(jnp.int32, (tq, tq), 1) // CHUNK
                s = jnp.where(kc <= qc, s, -jnp.inf)
            m_prev = m_ref[h]
            m_new = jnp.maximum(m_prev, jnp.max(s, axis=-1, keepdims=True))
            p = jnp.exp2(s - jnp.tile(m_new, (1, reps)))
            corr = jnp.exp2(m_prev - m_new)
            acc_ref[h] = corr * acc_ref[h] + _dot(p.astype(BF16),
                                                  v_ref[pl.ds(r0, tq), a:a + MLA_HEAD_PAD])
            m_ref[h] = m_new

    scores(0, sa_ref)

    def body(j, carry):
        scores(2 * j + 1, sb_ref)
        consume(sa_ref, 2 * j, False)
        scores(2 * j + 2, sa_ref)
        consume(sb_ref, 2 * j + 1, False)
        return carry

    lax.fori_loop(0, i // 2, body, 0)

    @pl.when(i % 2 == 0)
    def _():
        consume(sa_ref, i, True)

    @pl.when(i % 2 == 1)
    def _():
        scores(i, sb_ref)
        consume(sa_ref, i - 1, False)
        consume(sb_ref, i, True)

    ow = MLA_PAIR * MLA_V
    for pr in range(MLA_GROUP // MLA_PAIR):
        o_ref[:, pr * ow:(pr + 1) * ow] = _pair_output(acc_ref[MLA_PAIR * pr], acc_ref[MLA_PAIR * pr + 1])


def _mla_causal(q, kt, v, nb, seq, tq):
    nq = seq // tq
    ngrp = MLA_HEADS // MLA_GROUP
    pw = MLA_GROUP * MLA_HEAD_PAD
    return pl.pallas_call(
        functools.partial(_mla_causal_kernel, tq=tq),
        grid=(nb, ngrp, nq),
        in_specs=[
            pl.BlockSpec((tq, pw), lambda b, p, i: (b * nq + i, p)),
            pl.BlockSpec((nq, pw, tq), lambda b, p, i: (b, p, 0)),
            pl.BlockSpec((seq, pw), lambda b, p, i: (b, p)),
        ],
        out_specs=pl.BlockSpec((tq, MLA_GROUP * MLA_V), lambda b, p, i: (b * nq + i, p)),
        out_shape=jax.ShapeDtypeStruct((nb * seq, MLA_HEADS * MLA_V), BF16),
        scratch_shapes=[pltpu.VMEM((MLA_GROUP, tq, V7X_LANES), F32),
                        pltpu.VMEM((MLA_GROUP, tq, MLA_HEAD_PAD), F32),
                        pltpu.VMEM((MLA_GROUP, tq, tq), F32),
                        pltpu.VMEM((MLA_GROUP, tq, tq), F32)],
        compiler_params=_cparams(("parallel", "parallel", "arbitrary")),
        name="mla_causal",
    )(q, kt, v)


def _mla_cached_kernel(q_ref, kp_ref, vp_ref, kn_ref, vn_ref, o_ref):
    accs = []
    for h in range(MLA_PAIR):
        a = h * MLA_HEAD_PAD
        q = q_ref[:, a:a + MLA_HEAD_PAD]
        s1 = _dot_nt(q, kp_ref[:, a:a + MLA_HEAD_PAD])
        s2 = _dot_nt(q, kn_ref[:, a:a + MLA_HEAD_PAD])
        m = jnp.maximum(jnp.max(s1, axis=-1, keepdims=True), jnp.max(s2, axis=-1, keepdims=True))
        p1 = jnp.exp2(s1 - m)
        p2 = jnp.exp2(s2 - m)
        accs.append(_dot(p1.astype(BF16), vp_ref[:, a:a + MLA_HEAD_PAD])
                    + _dot(p2.astype(BF16), vn_ref[:, a:a + MLA_HEAD_PAD]))
    o_ref[...] = _pair_output(accs[0], accs[1])


def _mla_cached(q, k_new, v_new, k_past, v_past, layer, nb, seq, past):
    npair = MLA_HEADS // MLA_PAIR
    pw = MLA_PAIR * MLA_HEAD_PAD
    return pl.pallas_call(
        _mla_cached_kernel,
        grid=(nb, npair),
        in_specs=[
            pl.BlockSpec((seq, pw), lambda b, p: (b, p)),
            pl.BlockSpec((None, past, pw), lambda b, p: (layer, b, p)),
            pl.BlockSpec((None, past, pw), lambda b, p: (layer, b, p)),
            pl.BlockSpec((seq, pw), lambda b, p: (b, p)),
            pl.BlockSpec((seq, pw), lambda b, p: (b, p)),
        ],
        out_specs=pl.BlockSpec((seq, MLA_PAIR * MLA_V), lambda b, p: (b, p)),
        out_shape=jax.ShapeDtypeStruct((nb * seq, MLA_HEADS * MLA_V), BF16),
        compiler_params=_cparams(("parallel", "parallel")),
        name="mla_cached",
    )(q, k_past, v_past, k_new, v_new)


def _gla_kernel(gq_ref, gk_ref, gv_ref, gr_ref, misc_ref, wa2_ref, ba_ref, ng_ref, hsel_ref,
                hblk_ref, s0_ref, y_ref, sfin_ref, st_ref, sec_ref, same_ref, kbd_ref, vbd_ref,
                sbd_ref, *, tt):
    j = pl.program_id(1)
    c = min(CHUNK, tt)
    hc = GLA_HEADS * c
    levels = c.bit_length() - 1

    @pl.when(j == 0)
    def _():
        st_ref[...] = s0_ref[...]
        row = lax.broadcasted_iota(jnp.int32, (c, GLA_K), 0)
        trow = lax.broadcasted_iota(jnp.int32, (c, hc), 0)
        tcol = lax.broadcasted_iota(jnp.int32, (c, hc), 1) % c
        for lv in range(levels):
            m = 1 << lv
            sec_ref[lv] = ((row & m) != 0).astype(F32)
            same_ref[lv] = ((trow // (2 * m)) == (tcol // (2 * m))).astype(F32)
        kbd_ref[...] = (lax.broadcasted_iota(jnp.int32, (hc, GLA_K), 0) // c
                        == lax.broadcasted_iota(jnp.int32, (hc, GLA_K), 1) // GLA_DK).astype(BF16)
        vbd_ref[...] = (lax.broadcasted_iota(jnp.int32, (hc, GLA_V), 0) // c
                        == lax.broadcasted_iota(jnp.int32, (hc, GLA_V), 1) // GLA_DV).astype(BF16)
        sbd_ref[...] = (lax.broadcasted_iota(jnp.int32, (GLA_V, GLA_K), 0) // GLA_DV
                        == lax.broadcasted_iota(jnp.int32, (GLA_V, GLA_K), 1) // GLA_DK).astype(F32)

    z = _dot(misc_ref[...].astype(BF16), wa2_ref[...]) + ba_ref[...]
    la_all = jax.nn.log_sigmoid(z) / GLA_GATE_TAU

    st = st_ref[...]
    for r in range(tt // c):
        rs = slice(r * c, (r + 1) * c)
        q = gq_ref[rs, :] * (GLA_DK ** -0.5)
        k = gk_ref[rs, :]
        v = gv_ref[rs, :]
        la = la_all[rs, :]

        scores = jnp.zeros((c, hc), F32)
        p = la
        g = la
        for lv in range(levels):
            m = 1 << lv
            sec = sec_ref[lv]
            qe = (q * jnp.exp(p) * sec).astype(BF16)
            ke = (k * jnp.exp(g - p) * (1.0 - sec)).astype(BF16)
            ke_bd = jnp.tile(ke, (GLA_HEADS, 1)) * kbd_ref[...]
            scores = scores + _dot_nt(qe, ke_bd) * same_ref[lv]
            up = pltpu.roll(g, m, 0)
            dn = pltpu.roll(g, c - m, 0)
            p = p + sec * up
            g = g + dn + sec * (up - dn)

        o = _dot_nt((q * jnp.exp(p)).astype(BF16), st.astype(BF16))
        o = o + _dot((q * k).astype(BF16), hsel_ref[...]) * v
        v_bd = jnp.tile(v.astype(BF16), (GLA_HEADS, 1)) * vbd_ref[...]
        o = o + _dot(scores.astype(BF16), v_bd)

        kdec = (k * jnp.exp(g - p)).astype(BF16)
        upd = _dot(jnp.transpose(v).astype(BF16), kdec)
        st = st * jnp.exp(g[0:1, :]) + upd * sbd_ref[...]

        o2 = o * o
        o2_hi = o2.astype(BF16)
        o2_lo = (o2 - o2_hi.astype(F32)).astype(BF16)
        ms = (_dot(o2_hi, hblk_ref[...]) + _dot(o2_lo, hblk_ref[...])) * (1.0 / GLA_DV)
        y = o * lax.rsqrt(ms + EPS) * ng_ref[...] * _silu(gr_ref[rs, :])
        y_ref[rs, :] = y.astype(BF16)

    st_ref[...] = st
    sfin_ref[...] = st


def _gla(gla_in, misc, w_a2, b_a, norm_g, hsel, hblk, s0t, layer, nb, seq, tt):
    nt = seq // tt
    row = lambda b, j: (b * nt + j, 0)
    c = min(CHUNK, tt)
    hc = GLA_HEADS * c
    levels = c.bit_length() - 1
    return pl.pallas_call(
        functools.partial(_gla_kernel, tt=tt),
        grid=(nb, nt),
        in_specs=[
            pl.BlockSpec((tt, GLA_K), lambda b, j: (b * nt + j, 0)),
            pl.BlockSpec((tt, GLA_K), lambda b, j: (b * nt + j, 1)),
            pl.BlockSpec((tt, GLA_V), lambda b, j: (b * nt + j, 1)),
            pl.BlockSpec((tt, GLA_V), lambda b, j: (b * nt + j, 2)),
            pl.BlockSpec((tt, MISC_W), row),
            _resident((None, MISC_W, GLA_K), lambda b, j: (layer, 0, 0)),
            _resident((None, 1, GLA_K), lambda b, j: (layer, 0, 0)),
            _resident((None, 1, GLA_V), lambda b, j: (layer, 0, 0)),
            _resident((GLA_K, GLA_V), lambda b, j: (0, 0)),
            _resident((GLA_V, GLA_V), lambda b, j: (0, 0)),
            pl.BlockSpec((None, GLA_V, GLA_K), lambda b, j: (b, 0, 0)),
        ],
        out_specs=[pl.BlockSpec((tt, GLA_V), row),
                   pl.BlockSpec((None, GLA_V, GLA_K), lambda b, j: (b, 0, 0))],
        out_shape=(jax.ShapeDtypeStruct((nb * seq, GLA_V), BF16),
                   jax.ShapeDtypeStruct((nb, GLA_V, GLA_K), F32)),
        scratch_shapes=[pltpu.VMEM((GLA_V, GLA_K), F32),
                        pltpu.VMEM((levels, c, GLA_K), F32),
                        pltpu.VMEM((levels, c, hc), F32),
                        pltpu.VMEM((hc, GLA_K), BF16),
                        pltpu.VMEM((hc, GLA_V), BF16),
                        pltpu.VMEM((GLA_V, GLA_K), F32)],
        compiler_params=_cparams(("parallel", "arbitrary")),
        name="gla",
    )(gla_in, gla_in, gla_in, gla_in, misc, w_a2, b_a, norm_g, hsel, hblk, s0t)


def _merge_kernel(x_ref, yp_ref, ym_ref, yg_ref, wg_ref, wbp_ref, wbm_ref, wbg_ref, wo_ref,
                  g_ref, b_ref, o_ref):
    x = x_ref[...]
    xb = x.astype(BF16)
    m = jax.nn.sigmoid(_dot(xb, wg_ref[:, 0:D_MODEL])) * _dot(yp_ref[...], wbp_ref[...])
    m = m + jax.nn.sigmoid(_dot(xb, wg_ref[:, D_MODEL:2 * D_MODEL])) * _dot(ym_ref[...], wbm_ref[...])
    m = m + jax.nn.sigmoid(_dot(xb, wg_ref[:, 2 * D_MODEL:3 * D_MODEL])) * _dot(yg_ref[...], wbg_ref[...])
    y = _dot(m.astype(BF16), wo_ref[...])
    o_ref[...] = _layer_norm(ALPHA * x + y, g_ref[...], b_ref[...])


def _merge(x, y_pool, y_mla, y_gla, w_gates, w_bp, w_bm, w_bg, w_out, ln_g, ln_b, layer, tm):
    t = x.shape[0]
    row = lambda i: (i, 0)
    lay = lambda i: (layer, 0, 0)
    return pl.pallas_call(
        _merge_kernel,
        grid=(t // tm,),
        in_specs=[
            pl.BlockSpec((tm, D_MODEL), row),
            pl.BlockSpec((tm, POOL_WIDTH), row),
            pl.BlockSpec((tm, MLA_HEADS * MLA_V), row),
            pl.BlockSpec((tm, GLA_V), row),
            _resident((None, D_MODEL, 3 * D_MODEL), lay),
            _resident((None, POOL_WIDTH, D_MODEL), lay),
            _resident((None, MLA_HEADS * MLA_V, D_MODEL), lay),
            _resident((None, GLA_V, D_MODEL), lay),
            _resident((None, D_MODEL, D_MODEL), lay),
            _resident((None, None, 1, D_MODEL), lambda i: (layer, 1, 0, 0)),
            _resident((None, None, 1, D_MODEL), lambda i: (layer, 1, 0, 0)),
        ],
        out_specs=pl.BlockSpec((tm, D_MODEL), row),
        out_shape=jax.ShapeDtypeStruct((t, D_MODEL), F32),
        compiler_params=_cparams(("parallel",)),
        name="merge",
    )(x, y_pool, y_mla, y_gla, w_gates, w_bp, w_bm, w_bg, w_out, ln_g, ln_b)


def _memkv_kernel(m_ref, wk_ref, wv_ref, k_ref, v_ref):
    mb = m_ref[...].astype(BF16)
    k_ref[...] = _dot(mb, wk_ref[...])
    v_ref[...] = _dot(mb, wv_ref[...])


def _memkv(mem, w_k, w_v, tm):
    rows = mem.shape[0]
    nl = w_k.shape[0]
    return pl.pallas_call(
        _memkv_kernel,
        grid=(nl, rows // tm),
        in_specs=[
            pl.BlockSpec((tm, D_MODEL), lambda l, i: (i, 0)),
            pl.BlockSpec((None, D_MODEL, D_MODEL), lambda l, i: (l, 0, 0)),
            pl.BlockSpec((None, D_MODEL, D_MODEL), lambda l, i: (l, 0, 0)),
        ],
        out_specs=[pl.BlockSpec((None, tm, D_MODEL), lambda l, i: (l, i, 0)),
                   pl.BlockSpec((None, tm, D_MODEL), lambda l, i: (l, i, 0))],
        out_shape=(jax.ShapeDtypeStruct((nl, rows, D_MODEL), F32),
                   jax.ShapeDtypeStruct((nl, rows, D_MODEL), F32)),
        compiler_params=_cparams(("parallel", "parallel")),
        name="memkv",
    )(mem, w_k, w_v)


def _memattn_kernel(x_ref, k_ref, v_ref, wq_ref, wo_ref, g_ref, b_ref, o_ref):
    x = x_ref[...]
    q = _dot(x.astype(BF16), wq_ref[...]) * (MEM_HEAD_DIM ** -0.5)
    kb = k_ref[...].astype(BF16)
    vb = v_ref[...].astype(BF16)
    outs = []
    for h in range(MEM_HEADS):
        a = h * MEM_HEAD_DIM
        s = _dot_nt(q[:, a:a + MEM_HEAD_DIM].astype(BF16), kb[:, a:a + MEM_HEAD_DIM])
        s = s - jnp.max(s, axis=-1, keepdims=True)
        p = jnp.exp(s)
        l = jnp.sum(p, axis=-1, keepdims=True)
        outs.append(_dot(p.astype(BF16), vb[:, a:a + MEM_HEAD_DIM]) / l)
    o = jnp.concatenate(outs, axis=-1).astype(BF16)
    y = _dot(o, wo_ref[...])
    o_ref[...] = _layer_norm(ALPHA * x + y, g_ref[...], b_ref[...])


def _memattn(x, mem_k, mem_v, kv_layer, w_q, w_o, ln_g, ln_b, layer, nb, seq, tm):
    nt = seq // tm
    n_mem = mem_k.shape[2]
    lay = lambda b, i: (layer, 0, 0)
    return pl.pallas_call(
        _memattn_kernel,
        grid=(nb, nt),
        in_specs=[
            pl.BlockSpec((tm, D_MODEL), lambda b, i: (b * nt + i, 0)),
            pl.BlockSpec((None, None, n_mem, D_MODEL), lambda b, i: (kv_layer, b, 0, 0)),
            pl.BlockSpec((None, None, n_mem, D_MODEL), lambda b, i: (kv_layer, b, 0, 0)),
            _resident((None, D_MODEL, D_MODEL), lay),
            _resident((None, D_MODEL, D_MODEL), lay),
            _resident((None, None, 1, D_MODEL), lambda b, i: (layer, 2, 0, 0)),
            _resident((None, None, 1, D_MODEL), lambda b, i: (layer, 2, 0, 0)),
        ],
        out_specs=pl.BlockSpec((tm, D_MODEL), lambda b, i: (b * nt + i, 0)),
        out_shape=jax.ShapeDtypeStruct((nb * seq, D_MODEL), F32),
        compiler_params=_cparams(("parallel", "parallel")),
        name="memattn",
    )(x, mem_k, mem_v, w_q, w_o, ln_g, ln_b)


def _pack_weights(w_in, pool_w, mla_w_uq, mla_w_uk, mla_w_uv, gla_w_a2):
    nl = w_in.shape[0]
    splits = (POOL_WIDTH, MLA_Q_RANK, MLA_KV_RANK, MLA_ROPE, GLA_K, GLA_K, GLA_V,
              GLA_GATE_RANK, GLA_V, 3 * D_MODEL)
    offs = [0]
    for s in splits:
        offs.append(offs[-1] + s)
    col = lambda n: w_in[:, :, offs[n]:offs[n + 1]]
    half = MLA_ROPE // 2
    kr = col(3)
    kr_sw = jnp.concatenate([kr[..., half:], kr[..., :half]], -1)
    zeros = lambda n: jnp.zeros((nl, D_MODEL, n), w_in.dtype)
    w_a = jnp.concatenate(
        [col(0), col(1), col(2),
         kr, col(7), zeros(MISC_W - MLA_ROPE - GLA_GATE_RANK),
         kr_sw, zeros(MISC_W - MLA_ROPE),
         col(4), col(5), col(6), col(8)], -1).astype(BF16)
    w_gates = col(9).astype(BF16)

    uq = mla_w_uq.reshape(nl, MLA_Q_RANK, MLA_HEADS, MLA_NOPE + MLA_ROPE)
    uq_n, uq_r = uq[..., :MLA_NOPE], uq[..., MLA_NOPE:]
    uq_rs = jnp.concatenate([uq_r[..., half:], uq_r[..., :half]], -1)
    padq = jnp.zeros((nl, MLA_Q_RANK, MLA_HEADS, MLA_HEAD_PAD - MLA_NOPE - MLA_ROPE), uq.dtype)
    q_main = jnp.concatenate([uq_n, uq_r, padq], -1)
    q_swap = jnp.concatenate([jnp.zeros_like(uq_n), uq_rs, padq], -1)
    kw = MLA_HEADS * MLA_HEAD_PAD
    w_q2 = jnp.concatenate([q_main.reshape(nl, MLA_Q_RANK, kw),
                            q_swap.reshape(nl, MLA_Q_RANK, kw)], -1).astype(BF16)

    padk = jnp.zeros((nl, MLA_KV_RANK, MLA_HEADS, MLA_HEAD_PAD - MLA_NOPE), mla_w_uk.dtype)
    k_lat = jnp.concatenate([mla_w_uk, padk], -1).reshape(nl, MLA_KV_RANK, kw)
    place = jnp.zeros((MISC_W, MLA_HEADS, MLA_HEAD_PAD), F32)
    idx = jnp.arange(MLA_ROPE)
    place = place.at[idx, :, MLA_NOPE + idx].set(1.0).reshape(MISC_W, kw)
    k_misc = jnp.broadcast_to(place, (nl, MISC_W, kw))
    w_k = jnp.concatenate([k_lat, k_misc], 1).astype(BF16)
    w_kt = jnp.swapaxes(w_k, 1, 2)
    uv = mla_w_uv.reshape(nl, MLA_KV_RANK, MLA_HEADS // MLA_PAIR, MLA_PAIR, MLA_V)
    padv = jnp.zeros_like(uv[:, :, :, 0])
    w_v = jnp.stack([jnp.concatenate([uv[:, :, :, 0], padv], -1),
                     jnp.concatenate([padv, uv[:, :, :, 1]], -1)], 3)
    w_v = w_v.reshape(nl, MLA_KV_RANK, kw).astype(BF16)

    ng = len(POOL_WINDOWS)
    eye = jnp.eye(ng, dtype=pool_w.dtype)
    pool_bd = (pool_w[:, :, :, None, :] * eye[None, :, None, :, None]).reshape(
        nl, POOL_WIDTH, POOL_WIDTH).astype(BF16)

    w_a2 = jnp.zeros((nl, MISC_W, GLA_K), gla_w_a2.dtype)
    w_a2 = w_a2.at[:, MLA_ROPE:MLA_ROPE + GLA_GATE_RANK, :].set(gla_w_a2).astype(BF16)
    return w_a, w_gates, w_q2, w_k, w_kt, w_v, pool_bd, w_a2


def _rope_table(pos):
    half = MLA_ROPE // 2
    inv = ROPE_BASE ** (-jnp.arange(half, dtype=F32) / half)
    ang = pos.astype(F32)[:, None] * inv
    cos, sin = jnp.cos(ang), jnp.sin(ang)
    n = pos.shape[0]
    one = jnp.ones((n, GLA_GATE_RANK), F32)
    cm = jnp.concatenate([cos, cos, one, jnp.zeros((n, MISC_W - MLA_ROPE - GLA_GATE_RANK), F32)], -1)
    sm = jnp.concatenate([-sin, sin, jnp.zeros((n, MISC_W - MLA_ROPE), F32)], -1)
    padq = jnp.zeros((n, MLA_HEAD_PAD - MLA_NOPE - MLA_ROPE), F32)
    qs = MLA_SCALE * LOG2E
    cq = jnp.concatenate([jnp.ones((n, MLA_NOPE), F32), cos, cos, padq], -1) * qs
    sq = jnp.concatenate([jnp.zeros((n, MLA_NOPE), F32), -sin, sin, padq], -1) * qs
    return jnp.concatenate([cm, sm, cq, sq], -1)


def _state_to_tiles(s):
    eye = jnp.eye(GLA_HEADS, dtype=s.dtype)
    st = jnp.swapaxes(s, 2, 3)
    bd = st[:, :, :, None, :] * eye[None, :, None, :, None]
    return bd.reshape(s.shape[0], GLA_V, GLA_K)


def _tiles_to_state(t):
    b = t.shape[0]
    t5 = t.reshape(b, GLA_HEADS, GLA_DV, GLA_HEADS, GLA_DK)
    idx = jnp.arange(GLA_HEADS)
    diag = t5[:, idx, :, idx, :]
    return jnp.transpose(diag, (1, 0, 3, 2))


def _tile(n, pref):
    return pref if n % pref == 0 else n


def kernel(x_prompt, x_sample, mem_prompt, cache_pool, cache_mla_ckv, cache_mla_krope, state_gla,
           cache_mem_k, cache_mem_v, ln_g, ln_b, w_ffn1_in, w_ffn1_out, w_in, pool_w, pool_scale,
           mla_q_norm, mla_kv_norm, mla_w_uq, mla_w_uk, mla_w_uv, gla_w_a2, gla_b_a, gla_norm,
           w_branch_pool, w_branch_mla, w_branch_gla, w_out, mem_w_q, mem_w_k, mem_w_v, mem_w_o,
           w_ffn2_in, w_ffn2_out):
    nl = w_in.shape[0]
    bp, sp, _ = x_prompt.shape
    bs, ss, _ = x_sample.shape
    past = cache_mla_ckv.shape[2]
    n_mem = mem_prompt.shape[1]

    w_a, w_gates, w_q2, w_k, w_kt, w_v, pool_bd, w_a2 = _pack_weights(
        w_in, pool_w, mla_w_uq, mla_w_uk, mla_w_uv, gla_w_a2)
    bf = lambda w: w.astype(BF16)
    ffn_w = ((bf(w_ffn1_in), bf(w_ffn1_out)), (bf(w_ffn2_in), bf(w_ffn2_out)))
    w_bp, w_bm, w_bg, w_o = bf(w_branch_pool), bf(w_branch_mla), bf(w_branch_gla), bf(w_out)
    mw_q, mw_k, mw_v, mw_o = bf(mem_w_q), bf(mem_w_k), bf(mem_w_v), bf(mem_w_o)
    lng = ln_g.reshape(nl, 4, 1, D_MODEL)
    lnb = ln_b.reshape(nl, 4, 1, D_MODEL)
    qnorm = mla_q_norm.reshape(nl, 1, MLA_Q_RANK)
    kvnorm = mla_kv_norm.reshape(nl, 1, MLA_KV_RANK)
    pscale = pool_scale.reshape(nl, 1, POOL_WIDTH)
    b_a = gla_b_a.reshape(nl, 1, GLA_K)
    norm_g = jnp.tile(gla_norm, (1, GLA_HEADS)).reshape(nl, 1, GLA_V)
    hsel = (jnp.arange(GLA_K)[:, None] // GLA_DK == jnp.arange(GLA_V)[None, :] // GLA_DV).astype(BF16)
    hblk = (jnp.arange(GLA_V)[:, None] // GLA_DV == jnp.arange(GLA_V)[None, :] // GLA_DV).astype(BF16)

    tmem = _tile(bp * n_mem, 512)
    mem_k, mem_v = _memkv(mem_prompt.reshape(bp * n_mem, D_MODEL), mw_k, mw_v, tmem)
    mem_k = mem_k.reshape(nl, bp, n_mem, D_MODEL)
    mem_v = mem_v.reshape(nl, bp, n_mem, D_MODEL)
    k_past, v_past = _kvproj(cache_mla_ckv.reshape(nl, bs * past, MLA_KV_RANK),
                             cache_mla_krope.reshape(nl, bs * past, MLA_ROPE), w_k, w_v,
                             _tile(bs * past, 1024))
    smem_k = cache_mem_k.reshape(nl, bs, n_mem, D_MODEL)
    smem_v = cache_mem_v.reshape(nl, bs, n_mem, D_MODEL)

    groups = {
        "p": dict(nb=bp, seq=sp, pos0=0, tm=_tile(bp * sp, 512)),
        "s": dict(nb=bs, seq=ss, pos0=past, tm=_tile(bs * ss, 512)),
    }
    for gname, gr in groups.items():
        seq, tm = gr["seq"], gr["tm"]
        pos = gr["pos0"] + jnp.arange(seq)
        tab = _rope_table(pos)
        if tm <= seq:
            gr["tab"], gr["tab_period"] = tab, seq // tm
        else:
            gr["tab"], gr["tab_period"] = jnp.tile(tab, (tm // seq, 1)), 1
        gr["tseq"] = _tile(seq, 512)
        gr["tgla"] = _tile(seq, 512)

    xs = {"p": x_prompt.reshape(bp * sp, D_MODEL), "s": x_sample.reshape(bs * ss, D_MODEL)}
    zero_hist = jnp.zeros((bp, POOL_PAD, POOL_WIDTH), F32)
    zero_state = jnp.zeros((bp, GLA_V, GLA_K), F32)
    outs = {n: {"pool": [], "ckv": [], "kr": [], "gla": []} for n in groups}

    for l in range(nl):
        for gname, gr in groups.items():
            nb, seq, tm = gr["nb"], gr["seq"], gr["tm"]
            x = xs[gname]
            x = _ffn(x, ffn_w[0][0], ffn_w[0][1], lng, lnb, l, 0, tm)
            k_transposed = gname == "p"
            u_pool, ckv, misc, q, k, v, gla_in = _inproj(
                x, gr["tab"], w_a, w_q2, w_kt if k_transposed else w_k, w_v, qnorm, kvnorm,
                l, tm, gr["tab_period"], k_transposed)
            if gname == "p":
                hist = None
                hist16 = zero_hist
                s0t = zero_state
            else:
                hist = cache_pool[l]
                hist16 = jnp.concatenate(
                    [jnp.zeros((nb, POOL_PAD - POOL_HIST, POOL_WIDTH), F32), hist], 1)
                s0t = _state_to_tiles(state_gla[l])
            y_pool = _pool(u_pool, hist16, pool_bd, pscale, l, nb, seq, gr["tseq"], gr["pos0"])
            if gname == "p":
                y_mla = _mla_causal(q, k, v, nb, seq, gr["tseq"])
            else:
                y_mla = _mla_cached(q, k, v, k_past, v_past, l, nb, seq, past)
            y_gla, s_fin = _gla(gla_in, misc, w_a2, b_a, norm_g, hsel, hblk, s0t, l, nb, seq, gr["tgla"])
            x = _merge(x, y_pool, y_mla, y_gla, w_gates, w_bp, w_bm, w_bg, w_o, lng, lnb, l, tm)
            if gname == "p":
                x = _memattn(x, mem_k, mem_v, l, mw_q, mw_o, lng, lnb, l, nb, seq, gr["tseq"])
            else:
                x = _memattn(x, smem_k, smem_v, l, mw_q, mw_o, lng, lnb, l, nb, seq, gr["tseq"])
            x = _ffn(x, ffn_w[1][0], ffn_w[1][1], lng, lnb, l, 3, tm)
            xs[gname] = x

            u3 = u_pool.reshape(nb, seq, POOL_WIDTH)
            if hist is None:
                hist = jnp.zeros((nb, POOL_HIST, POOL_WIDTH), F32)
            new_pool = jnp.concatenate([hist, u3], 1)[:, -POOL_HIST:]
            o = outs[gname]
            o["pool"].append(new_pool)
            o["ckv"].append(ckv.reshape(nb, seq, MLA_KV_RANK))
            o["kr"].append(misc[:, :MLA_ROPE].reshape(nb, seq, MLA_ROPE))
            o["gla"].append(_tiles_to_state(s_fin))

    st = lambda n, key: jnp.stack(outs[n][key], 0)
    return (xs["p"].reshape(bp, sp, D_MODEL), xs["s"].reshape(bs, ss, D_MODEL),
            st("p", "pool"), st("s", "pool"),
            st("p", "ckv"), st("s", "ckv"),
            st("p", "kr"), st("s", "kr"),
            st("p", "gla"), st("s", "gla"),
            mem_k.reshape(nl, bp, n_mem, MEM_HEADS, MEM_HEAD_DIM),
            mem_v.reshape(nl, bp, n_mem, MEM_HEADS, MEM_HEAD_DIM))
```

```python
import functools
import math

import jax
import jax.numpy as jnp
from jax import lax
from jax.experimental import pallas as pl
from jax.experimental.pallas import tpu as pltpu

F32 = jnp.float32
BF16 = jnp.bfloat16

D_MODEL = 1024
DEPTH = 4
CHUNK = 64
EPS = 1e-5
ALPHA = (2 * DEPTH) ** 0.25

POOL_WINDOWS = (2, 4, 8, 16)
POOL_GROUP = D_MODEL // 16
POOL_WIDTH = len(POOL_WINDOWS) * POOL_GROUP
POOL_HIST = max(POOL_WINDOWS) - 1

MLA_HEADS = 8
MLA_Q_RANK = D_MODEL // 4
MLA_KV_RANK = D_MODEL // 4
MLA_NOPE = 64
MLA_ROPE = 32
MLA_V = 64
MLA_SCALE = (MLA_NOPE + MLA_ROPE) ** -0.5
ROPE_BASE = 10000.0
MLA_HEAD_PAD = 128
MLA_PAIR = 2
MLA_GROUP = 4
V_ONE_EVEN = MLA_V
V_ONE_ODD = MLA_HEAD_PAD
LOG2E = math.log2(math.e)

GLA_HEADS = 4
GLA_DK = 32
GLA_DV = 64
GLA_GATE_RANK = 16
GLA_GATE_TAU = 16.0
GLA_K = GLA_HEADS * GLA_DK
GLA_V = GLA_HEADS * GLA_DV

MEM_HEADS = 4
MEM_HEAD_DIM = D_MODEL // MEM_HEADS
D_FF = 2816

V7X_LANES = 128
V7X_SUBLANES = 8
V7X_VMEM_BYTES = 64 * 1024 * 1024
VMEM_LIMIT = V7X_VMEM_BYTES - 8 * 1024 * 1024

A_POOL = 0
A_QLAT = 256
A_CKV = 512
A_M1 = 768
A_M2 = 896
A_GLA = 1024
A_COLS = 1792
MISC_W = 128
GLA_IN_W = 2 * GLA_K + 2 * GLA_V


def _cparams(sem):
    return pltpu.CompilerParams(dimension_semantics=sem, vmem_limit_bytes=VMEM_LIMIT)


def _resident(shape, index_map):
    return pl.BlockSpec(shape, index_map, pipeline_mode=pl.Buffered(1))


def _layer_norm(y, g, b):
    mu = jnp.mean(y, axis=-1, keepdims=True)
    d = y - mu
    var = jnp.mean(d * d, axis=-1, keepdims=True)
    return d * lax.rsqrt(var + EPS) * g + b


def _rms_norm(y, g):
    return y * lax.rsqrt(jnp.mean(y * y, axis=-1, keepdims=True) + EPS) * g


def _dot(a, b):
    return jnp.dot(a, b, preferred_element_type=F32)


def _dot_nt(a, b):
    return lax.dot_general(a, b, (((1,), (1,)), ((), ())), preferred_element_type=F32)


def _silu(x):
    return x * jax.nn.sigmoid(x)


FFN_CHUNKS = ((0, 1024), (1024, 1024), (2048, 768))


def _ffn_kernel(x_ref, win_ref, wout_ref, g_ref, b_ref, o_ref, act_ref):
    x = x_ref[...]
    xb = x.astype(BF16)
    for c0, cw in FFN_CHUNKS:
        gate = _dot(xb, win_ref[:, c0:c0 + cw])
        up = _dot(xb, win_ref[:, D_FF + c0:D_FF + c0 + cw])
        act_ref[:, c0:c0 + cw] = (_silu(gate) * up).astype(BF16)
    y = _dot(act_ref[...], wout_ref[...])
    o_ref[...] = _layer_norm(ALPHA * x + 0.5 * y, g_ref[...], b_ref[...])


def _ffn(x, w_in, w_out, ln_g, ln_b, layer, which, tm):
    t = x.shape[0]
    return pl.pallas_call(
        _ffn_kernel,
        grid=(t // tm,),
        in_specs=[
            pl.BlockSpec((tm, D_MODEL), lambda i: (i, 0)),
            _resident((None, D_MODEL, 2 * D_FF), lambda i: (layer, 0, 0)),
            _resident((None, D_FF, D_MODEL), lambda i: (layer, 0, 0)),
            _resident((None, None, 1, D_MODEL), lambda i: (layer, which, 0, 0)),
            _resident((None, None, 1, D_MODEL), lambda i: (layer, which, 0, 0)),
        ],
        out_specs=pl.BlockSpec((tm, D_MODEL), lambda i: (i, 0)),
        out_shape=jax.ShapeDtypeStruct((t, D_MODEL), F32),
        scratch_shapes=[pltpu.VMEM((tm, D_FF), BF16)],
        compiler_params=_cparams(("parallel",)),
        name="ffn",
    )(x, w_in, w_out, ln_g, ln_b)


def _v_with_ones(vv):
    lane = lax.broadcasted_iota(jnp.int32, vv.shape, 1) % (MLA_PAIR * MLA_HEAD_PAD)
    return jnp.where((lane == V_ONE_EVEN) | (lane == V_ONE_ODD), 1.0, vv).astype(BF16)


def _inproj_kernel(x_ref, tab_ref, wa_ref, wq_ref, wk_ref, wv_ref, qn_ref, kvn_ref,
                   pool_ref, ckv_ref, misc_ref, q_ref, k_ref, v_ref, gla_ref, *, k_transposed):
    xb = x_ref[...].astype(BF16)
    h = _dot(xb, wa_ref[:, 0:A_GLA])
    gla_ref[...] = _dot(xb, wa_ref[:, A_GLA:A_COLS])
    pool_ref[...] = h[:, A_POOL:A_POOL + POOL_WIDTH]

    cm = tab_ref[:, 0:128]
    sm = tab_ref[:, 128:256]
    misc = h[:, A_M1:A_M1 + MISC_W] * cm + h[:, A_M2:A_M2 + MISC_W] * sm
    misc_ref[...] = misc

    ckv = _rms_norm(h[:, A_CKV:A_CKV + MLA_KV_RANK], kvn_ref[...])
    ckv_ref[...] = ckv
    ckvb = ckv.astype(BF16)
    lat = jnp.concatenate([ckvb, misc.astype(BF16)], axis=-1)
    if k_transposed:
        k_ref[...] = _dot_nt(wk_ref[...], lat).astype(BF16)
    else:
        k_ref[...] = _dot(lat, wk_ref[...]).astype(BF16)
    v_ref[...] = _v_with_ones(_dot(ckvb, wv_ref[...]))

    qn = _rms_norm(h[:, A_QLAT:A_QLAT + MLA_Q_RANK], qn_ref[...]).astype(BF16)
    q2 = _dot(qn, wq_ref[...])
    kw = MLA_HEADS * MLA_HEAD_PAD
    cq = tab_ref[:, 256:384]
    sq = tab_ref[:, 384:512]
    for hd in range(MLA_HEADS):
        a = hd * MLA_HEAD_PAD
        q_ref[:, a:a + MLA_HEAD_PAD] = (
            q2[:, a:a + MLA_HEAD_PAD] * cq + q2[:, kw + a:kw + a + MLA_HEAD_PAD] * sq).astype(BF16)


def _inproj(x, tab, w_a, w_q2, w_k, w_v, qnorm, kvnorm, layer, tm, tab_period, k_transposed):
    t = x.shape[0]
    kw = MLA_HEADS * MLA_HEAD_PAD
    lat_w = MLA_KV_RANK + MISC_W
    row = lambda i: (i, 0)
    if k_transposed:
        k_shape = jax.ShapeDtypeStruct((t // tm, kw, tm), BF16)
        k_spec = pl.BlockSpec((None, kw, tm), lambda i: (i, 0, 0))
        wk_spec = _resident((None, kw, lat_w), lambda i: (layer, 0, 0))
    else:
        k_shape = jax.ShapeDtypeStruct((t, kw), BF16)
        k_spec = pl.BlockSpec((tm, kw), row)
        wk_spec = _resident((None, lat_w, kw), lambda i: (layer, 0, 0))
    out_shapes = (
        jax.ShapeDtypeStruct((t, POOL_WIDTH), F32),
        jax.ShapeDtypeStruct((t, MLA_KV_RANK), F32),
        jax.ShapeDtypeStruct((t, MISC_W), F32),
        jax.ShapeDtypeStruct((t, kw), BF16),
        k_shape,
        jax.ShapeDtypeStruct((t, kw), BF16),
        jax.ShapeDtypeStruct((t, GLA_IN_W), F32),
    )
    out_specs = [pl.BlockSpec((tm, s.shape[-1]), row) for s in out_shapes]
    out_specs[4] = k_spec
    return pl.pallas_call(
        functools.partial(_inproj_kernel, k_transposed=k_transposed),
        grid=(t // tm,),
        in_specs=[
            pl.BlockSpec((tm, D_MODEL), row),
            pl.BlockSpec((tm, 512), lambda i: (i % tab_period, 0)),
            _resident((None, D_MODEL, A_COLS), lambda i: (layer, 0, 0)),
            _resident((None, MLA_Q_RANK, 2 * kw), lambda i: (layer, 0, 0)),
            wk_spec,
            _resident((None, MLA_KV_RANK, kw), lambda i: (layer, 0, 0)),
            _resident((None, 1, MLA_Q_RANK), lambda i: (layer, 0, 0)),
            _resident((None, 1, MLA_KV_RANK), lambda i: (layer, 0, 0)),
        ],
        out_specs=out_specs,
        out_shape=out_shapes,
        compiler_params=_cparams(("parallel",)),
        name="inproj",
    )(x, tab, w_a, w_q2, w_k, w_v, qnorm, kvnorm)


POOL_PAD = 16


def _pool_kernel(u_ref, hist_ref, w_ref, scale_ref, y_ref, ext_ref, *, tp, pos0):
    j = pl.program_id(1)

    @pl.when(j == 0)
    def _():
        ext_ref[0:POOL_PAD, :] = hist_ref[...]

    tok = u_ref[...]
    ext_ref[POOL_PAD:POOL_PAD + tp, :] = tok
    acc = ext_ref[...]
    sums = {}
    w = 1
    while w < max(POOL_WINDOWS):
        acc = acc + pltpu.roll(acc, w, 0)
        w *= 2
        if w in POOL_WINDOWS:
            sums[w] = acc[POOL_PAD:POOL_PAD + tp, :]
    lane = lax.broadcasted_iota(jnp.int32, (tp, POOL_WIDTH), 1)
    pos = pos0 + j * tp + lax.broadcasted_iota(jnp.int32, (tp, POOL_WIDTH), 0)
    total = None
    width = None
    for gi, w in enumerate(POOL_WINDOWS):
        in_group = lane >= gi * POOL_GROUP
        total = sums[w] if total is None else jnp.where(in_group, sums[w], total)
        width = jnp.full_like(pos, w) if width is None else jnp.where(in_group, w, width)
    cnt = jnp.minimum(pos + 1, width).astype(F32)
    pooled = total / cnt - tok
    y = _dot(pooled.astype(BF16), w_ref[...]) * scale_ref[...]
    y_ref[...] = y.astype(BF16)
    ext_ref[0:POOL_PAD, :] = ext_ref[tp:tp + POOL_PAD, :]


def _pool(u, hist16, w_bd, scale, layer, nb, seq, tp, pos0):
    nt = seq // tp
    return pl.pallas_call(
        functools.partial(_pool_kernel, tp=tp, pos0=pos0),
        grid=(nb, nt),
        in_specs=[
            pl.BlockSpec((tp, POOL_WIDTH), lambda b, j: (b * nt + j, 0)),
            pl.BlockSpec((None, POOL_PAD, POOL_WIDTH), lambda b, j: (b, 0, 0)),
            _resident((None, POOL_WIDTH, POOL_WIDTH), lambda b, j: (layer, 0, 0)),
            _resident((None, 1, POOL_WIDTH), lambda b, j: (layer, 0, 0)),
        ],
        out_specs=pl.BlockSpec((tp, POOL_WIDTH), lambda b, j: (b * nt + j, 0)),
        out_shape=jax.ShapeDtypeStruct((nb * seq, POOL_WIDTH), BF16),
        scratch_shapes=[pltpu.VMEM((POOL_PAD + tp, POOL_WIDTH), F32)],
        compiler_params=_cparams(("parallel", "arbitrary")),
        name="pool",
    )(u, hist16, w_bd, scale)


def _pair_output(acc0, acc1):
    lane = lax.broadcasted_iota(jnp.int32, acc0.shape, 1)
    o0 = acc0 / acc0[:, V_ONE_EVEN:V_ONE_EVEN + 1]
    o1 = acc1 / acc1[:, 0:1]
    return jnp.where(lane < MLA_V, o0, o1).astype(BF16)


def _mla_causal_kernel(q_ref, kt_ref, v_ref, o_ref, m_ref, acc_ref, sa_ref, sb_ref, *, tq):
    i = pl.program_id(2)
    m_ref[...] = jnp.full(m_ref.shape, -jnp.inf, F32)
    acc_ref[...] = jnp.zeros(acc_ref.shape, F32)
    reps = tq // V7X_LANES

    def scores(kt, dst_ref):
        for h in range(MLA_GROUP):
            a = h * MLA_HEAD_PAD
            dst_ref[h] = _dot(q_ref[:, a:a + MLA_HEAD_PAD], kt_ref[kt, a:a + MLA_HEAD_PAD, :])

    def consume(src_ref, kt, masked):
        r0 = pl.multiple_of(kt * tq, tq)
        for h in range(MLA_GROUP):
            a = h * MLA_HEAD_PAD
            s = src_ref[h]
            if masked:
                qc = lax.broadcasted_iota(jnp.int32, (tq, tq), 0) // CHUNK
                kc = lax.broadcasted_iota(jnp.int32, (tq, tq), 1) // CHUNK
                s = jnp.where(kc <= qc, s, -jnp.inf)
            m_prev = m_ref[h]
            m_new = jnp.maximum(m_prev, jnp.max(s, axis=-1, keepdims=True))
            p = jnp.exp2(s - jnp.tile(m_new, (1, reps)))
            corr = jnp.exp2(m_prev - m_new)
            acc_ref[h] = corr * acc_ref[h] + _dot(p.astype(BF16),
                                                  v_ref[pl.ds(r0, tq), a:a + MLA_HEAD_PAD])
            m_ref[h] = m_new

    scores(0, sa_ref)

    def body(j, carry):
        scores(2 * j + 1, sb_ref)
        consume(sa_ref, 2 * j, False)
        scores(2 * j + 2, sa_ref)
        consume(sb_ref, 2 * j + 1, False)
        return carry

    lax.fori_loop(0, i // 2, body, 0)

    @pl.when(i % 2 == 0)
    def _():
        consume(sa_ref, i, True)

    @pl.when(i % 2 == 1)
    def _():
        scores(i, sb_ref)
        consume(sa_ref, i - 1, False)
        consume(sb_ref, i, True)

    ow = MLA_PAIR * MLA_V
    for pr in range(MLA_GROUP // MLA_PAIR):
        o_ref[:, pr * ow:(pr + 1) * ow] = _pair_output(acc_ref[MLA_PAIR * pr], acc_ref[MLA_PAIR * pr + 1])


def _mla_causal(q, kt, v, nb, seq, tq):
    nq = seq // tq
    ngrp = MLA_HEADS // MLA_GROUP
    pw = MLA_GROUP * MLA_HEAD_PAD
    return pl.pallas_call(
        functools.partial(_mla_causal_kernel, tq=tq),
        grid=(nb, ngrp, nq),
        in_specs=[
            pl.BlockSpec((tq, pw), lambda b, p, i: (b * nq + i, p)),
            pl.BlockSpec((nq, pw, tq), lambda b, p, i: (b, p, 0)),
            pl.BlockSpec((seq, pw), lambda b, p, i: (b, p)),
        ],
        out_specs=pl.BlockSpec((tq, MLA_GROUP * MLA_V), lambda b, p, i: (b * nq + i, p)),
        out_shape=jax.ShapeDtypeStruct((nb * seq, MLA_HEADS * MLA_V), BF16),
        scratch_shapes=[pltpu.VMEM((MLA_GROUP, tq, V7X_LANES), F32),
                        pltpu.VMEM((MLA_GROUP, tq, MLA_HEAD_PAD), F32),
                        pltpu.VMEM((MLA_GROUP, tq, tq), F32),
                        pltpu.VMEM((MLA_GROUP, tq, tq), F32)],
        compiler_params=_cparams(("parallel", "parallel", "arbitrary")),
        name="mla_causal",
    )(q, kt, v)


def _mla_cached_kernel(q_ref, cp_ref, rp_ref, cn_ref, mn_ref, wabs_ref, wuv_ref, o_ref):
    qa = jnp.concatenate(
        [_dot(q_ref[:, h * MLA_HEAD_PAD:(h + 1) * MLA_HEAD_PAD], wabs_ref[h]) for h in range(MLA_HEADS)],
        axis=0).astype(BF16)
    qa_lat = qa[:, 0:MLA_KV_RANK]
    qa_misc = qa[:, MLA_KV_RANK:]
    cp = cp_ref[...].astype(BF16)
    cn = cn_ref[...].astype(BF16)
    s1 = _dot_nt(qa_lat, cp) + _dot_nt(qa_misc[:, 0:MLA_ROPE], rp_ref[...].astype(BF16))
    s2 = _dot_nt(qa_lat, cn) + _dot_nt(qa_misc, mn_ref[...].astype(BF16))
    m = jnp.maximum(jnp.max(s1, axis=-1, keepdims=True), jnp.max(s2, axis=-1, keepdims=True))
    p1 = jnp.exp2(s1 - m)
    p2 = jnp.exp2(s2 - m)
    l = jnp.sum(p1, axis=-1, keepdims=True) + jnp.sum(p2, axis=-1, keepdims=True)
    lat = ((_dot(p1.astype(BF16), cp) + _dot(p2.astype(BF16), cn)) / l).astype(BF16)
    seq = q_ref.shape[0]
    y = _dot(lat[0:seq], wuv_ref[0])
    for h in range(1, MLA_HEADS):
        y = y + _dot(lat[h * seq:(h + 1) * seq], wuv_ref[h])
    o_ref[...] = y.astype(BF16)


def _mla_cached(q, ckv_new, misc_new, ckv_past, kr_past, w_abs, w_uvp, layer, nb, seq, past):
    kw = MLA_HEADS * MLA_HEAD_PAD
    vw = MLA_HEADS * MLA_V
    lat_w = MLA_KV_RANK + MISC_W
    return pl.pallas_call(
        _mla_cached_kernel,
        grid=(nb,),
        in_specs=[
            pl.BlockSpec((seq, kw), lambda b: (b, 0)),
            pl.BlockSpec((None, None, past, MLA_KV_RANK), lambda b: (layer, b, 0, 0)),
            pl.BlockSpec((None, None, past, MLA_ROPE), lambda b: (layer, b, 0, 0)),
            pl.BlockSpec((seq, MLA_KV_RANK), lambda b: (b, 0)),
            pl.BlockSpec((seq, MISC_W), lambda b: (b, 0)),
            _resident((None, MLA_HEADS, MLA_HEAD_PAD, lat_w), lambda b: (layer, 0, 0, 0)),
            _resident((None, MLA_HEADS, MLA_KV_RANK, vw), lambda b: (layer, 0, 0, 0)),
        ],
        out_specs=pl.BlockSpec((seq, vw), lambda b: (b, 0)),
        out_shape=jax.ShapeDtypeStruct((nb * seq, vw), BF16),
        compiler_params=_cparams(("parallel",)),
        name="mla_cached",
    )(q, ckv_past, kr_past, ckv_new, misc_new, w_abs, w_uvp)


def _gla_kernel(gq_ref, gk_ref, gv_ref, gr_ref, misc_ref, wa2_ref, ba_ref, ng_ref, hsel_ref,
                hblk_ref, s0_ref, y_ref, sfin_ref, st_ref, sec_ref, fst_ref, same_ref, kbd_ref,
                vbd_ref, sbd_ref, o_ref, *, tt):
    j = pl.program_id(1)
    c = min(CHUNK, tt)
    hc = GLA_HEADS * c
    levels = c.bit_length() - 1

    @pl.when(j == 0)
    def _():
        st_ref[...] = s0_ref[...]
        row = lax.broadcasted_iota(jnp.int32, (c, GLA_K), 0)
        trow = lax.broadcasted_iota(jnp.int32, (c, hc), 0)
        tcol = lax.broadcasted_iota(jnp.int32, (c, hc), 1) % c
        for lv in range(levels):
            m = 1 << lv
            sec_ref[lv] = ((row & m) != 0).astype(F32)
            fst_ref[lv] = ((row & m) == 0).astype(F32)
            same_ref[lv] = ((trow // (2 * m)) == (tcol // (2 * m))).astype(F32)
        kbd_ref[...] = (lax.broadcasted_iota(jnp.int32, (hc, GLA_K), 0) // c
                        == lax.broadcasted_iota(jnp.int32, (hc, GLA_K), 1) // GLA_DK).astype(BF16)
        vbd_ref[...] = (lax.broadcasted_iota(jnp.int32, (hc, GLA_V), 0) // c
                        == lax.broadcasted_iota(jnp.int32, (hc, GLA_V), 1) // GLA_DV).astype(BF16)
        sbd_ref[...] = (lax.broadcasted_iota(jnp.int32, (GLA_V, GLA_K), 0) // GLA_DV
                        == lax.broadcasted_iota(jnp.int32, (GLA_V, GLA_K), 1) // GLA_DK).astype(F32)

    z = _dot(misc_ref[...].astype(BF16), wa2_ref[...]) + ba_ref[...]
    la_all = jax.nn.log_sigmoid(z) * (LOG2E / GLA_GATE_TAU)

    st = st_ref[...]
    for r in range(tt // c):
        rs = slice(r * c, (r + 1) * c)
        q = gq_ref[rs, :] * (GLA_DK ** -0.5)
        k = gk_ref[rs, :]
        v = gv_ref[rs, :]
        la = la_all[rs, :]

        scores = jnp.zeros((c, hc), F32)
        p = la
        g = la
        for lv in range(levels):
            m = 1 << lv
            if m < V7X_SUBLANES:
                sec = sec_ref[lv]
                fst = fst_ref[lv]
                qe = (q * jnp.exp2(p) * sec).astype(BF16)
                ke = (k * jnp.exp2(g - p) * fst).astype(BF16)
                ke_bd = jnp.tile(ke, (GLA_HEADS, 1)) * kbd_ref[...]
                scores = scores + _dot_nt(qe, ke_bd) * same_ref[lv]
                up = sec * pltpu.roll(g, m, 0)
                p = p + up
                g = g + up + fst * pltpu.roll(g, c - m, 0)
            else:
                nb2 = c // (2 * m)

                def halves(a):
                    a4 = a.reshape(nb2, 2, m, a.shape[-1])
                    return a4[:, 0], a4[:, 1]

                def join(lo, hi):
                    return jnp.stack([lo, hi], axis=1).reshape(c, lo.shape[-1])

                p_lo, p_hi = halves(p)
                g_lo, g_hi = halves(g)
                qe_hi = (halves(q)[1] * jnp.exp2(p_hi)).reshape(c // 2, GLA_K).astype(BF16)
                ke_lo = halves(k)[0] * jnp.exp2(g_lo - p_lo)
                ke = join(ke_lo, jnp.zeros_like(ke_lo)).astype(BF16)
                ke_bd = jnp.tile(ke, (GLA_HEADS, 1)) * kbd_ref[...]
                a_hi = _dot_nt(qe_hi, ke_bd) * halves(same_ref[lv])[1].reshape(c // 2, hc)
                s_lo, s_hi = halves(scores)
                scores = join(s_lo, s_hi + a_hi.reshape(nb2, m, hc))
                tot = g_lo + g_hi
                p = join(p_lo, p_hi + g_lo)
                g = join(tot, tot)

        o = _dot_nt((q * jnp.exp2(p)).astype(BF16), st.astype(BF16))
        v_bd = jnp.tile(v.astype(BF16), (GLA_HEADS, 1)) * vbd_ref[...]
        o_ref[rs, :] = o + _dot(scores.astype(BF16), v_bd)

        kdec = (k * jnp.exp2(g - p)).astype(BF16)
        upd = _dot(jnp.transpose(v).astype(BF16), kdec)
        st = st * jnp.exp2(g[0:1, :]) + upd * sbd_ref[...]

    st_ref[...] = st
    sfin_ref[...] = st

    qk = (gq_ref[...] * (GLA_DK ** -0.5) * gk_ref[...]).astype(BF16)
    o = o_ref[...] + _dot(qk, hsel_ref[...]) * gv_ref[...]
    o2 = o * o
    o2_hi = o2.astype(BF16)
    o2_lo = (o2 - o2_hi.astype(F32)).astype(BF16)
    ms = (_dot(o2_hi, hblk_ref[...]) + _dot(o2_lo, hblk_ref[...])) * (1.0 / GLA_DV)
    y = o * lax.rsqrt(ms + EPS) * ng_ref[...] * _silu(gr_ref[...])
    y_ref[...] = y.astype(BF16)


def _gla(gla_in, misc, w_a2, b_a, norm_g, hsel, hblk, s0t, layer, nb, seq, tt):
    nt = seq // tt
    row = lambda b, j: (b * nt + j, 0)
    c = min(CHUNK, tt)
    hc = GLA_HEADS * c
    levels = c.bit_length() - 1
    return pl.pallas_call(
        functools.partial(_gla_kernel, tt=tt),
        grid=(nb, nt),
        in_specs=[
            pl.BlockSpec((tt, GLA_K), lambda b, j: (b * nt + j, 0)),
            pl.BlockSpec((tt, GLA_K), lambda b, j: (b * nt + j, 1)),
            pl.BlockSpec((tt, GLA_V), lambda b, j: (b * nt + j, 1)),
            pl.BlockSpec((tt, GLA_V), lambda b, j: (b * nt + j, 2)),
            pl.BlockSpec((tt, MISC_W), row),
            _resident((None, MISC_W, GLA_K), lambda b, j: (layer, 0, 0)),
            _resident((None, 1, GLA_K), lambda b, j: (layer, 0, 0)),
            _resident((None, 1, GLA_V), lambda b, j: (layer, 0, 0)),
            _resident((GLA_K, GLA_V), lambda b, j: (0, 0)),
            _resident((GLA_V, GLA_V), lambda b, j: (0, 0)),
            pl.BlockSpec((None, GLA_V, GLA_K), lambda b, j: (b, 0, 0)),
        ],
        out_specs=[pl.BlockSpec((tt, GLA_V), row),
                   pl.BlockSpec((None, GLA_V, GLA_K), lambda b, j: (b, 0, 0))],
        out_shape=(jax.ShapeDtypeStruct((nb * seq, GLA_V), BF16),
                   jax.ShapeDtypeStruct((nb, GLA_V, GLA_K), F32)),
        scratch_shapes=[pltpu.VMEM((GLA_V, GLA_K), F32),
                        pltpu.VMEM((levels, c, GLA_K), F32),
                        pltpu.VMEM((levels, c, GLA_K), F32),
                        pltpu.VMEM((levels, c, hc), F32),
                        pltpu.VMEM((hc, GLA_K), BF16),
                        pltpu.VMEM((hc, GLA_V), BF16),
                        pltpu.VMEM((GLA_V, GLA_K), F32),
                        pltpu.VMEM((tt, GLA_V), F32)],
        compiler_params=_cparams(("parallel", "arbitrary")),
        name="gla",
    )(gla_in, gla_in, gla_in, gla_in, misc, w_a2, b_a, norm_g, hsel, hblk, s0t)


def _merge_kernel(x_ref, yp_ref, ym_ref, yg_ref, wg_ref, wbp_ref, wbm_ref, wbg_ref, wo_ref,
                  g_ref, b_ref, o_ref):
    x = x_ref[...]
    xb = x.astype(BF16)
    m = jax.nn.sigmoid(_dot(xb, wg_ref[:, 0:D_MODEL])) * _dot(yp_ref[...], wbp_ref[...])
    m = m + jax.nn.sigmoid(_dot(xb, wg_ref[:, D_MODEL:2 * D_MODEL])) * _dot(ym_ref[...], wbm_ref[...])
    m = m + jax.nn.sigmoid(_dot(xb, wg_ref[:, 2 * D_MODEL:3 * D_MODEL])) * _dot(yg_ref[...], wbg_ref[...])
    y = _dot(m.astype(BF16), wo_ref[...])
    o_ref[...] = _layer_norm(ALPHA * x + y, g_ref[...], b_ref[...])


def _merge(x, y_pool, y_mla, y_gla, w_gates, w_bp, w_bm, w_bg, w_out, ln_g, ln_b, layer, tm):
    t = x.shape[0]
    row = lambda i: (i, 0)
    lay = lambda i: (layer, 0, 0)
    return pl.pallas_call(
        _merge_kernel,
        grid=(t // tm,),
        in_specs=[
            pl.BlockSpec((tm, D_MODEL), row),
            pl.BlockSpec((tm, POOL_WIDTH), row),
            pl.BlockSpec((tm, MLA_HEADS * MLA_V), row),
            pl.BlockSpec((tm, GLA_V), row),
            _resident((None, D_MODEL, 3 * D_MODEL), lay),
            _resident((None, POOL_WIDTH, D_MODEL), lay),
            _resident((None, MLA_HEADS * MLA_V, D_MODEL), lay),
            _resident((None, GLA_V, D_MODEL), lay),
            _resident((None, D_MODEL, D_MODEL), lay),
            _resident((None, None, 1, D_MODEL), lambda i: (layer, 1, 0, 0)),
            _resident((None, None, 1, D_MODEL), lambda i: (layer, 1, 0, 0)),
        ],
        out_specs=pl.BlockSpec((tm, D_MODEL), row),
        out_shape=jax.ShapeDtypeStruct((t, D_MODEL), F32),
        compiler_params=_cparams(("parallel",)),
        name="merge",
    )(x, y_pool, y_mla, y_gla, w_gates, w_bp, w_bm, w_bg, w_out, ln_g, ln_b)


def _memkv_kernel(m_ref, wk_ref, wv_ref, k_ref, v_ref):
    mb = m_ref[...].astype(BF16)
    k_ref[...] = _dot(mb, wk_ref[...])
    v_ref[...] = _dot(mb, wv_ref[...])


def _memkv(mem, w_k, w_v, tm):
    rows = mem.shape[0]
    nl = w_k.shape[0]
    return pl.pallas_call(
        _memkv_kernel,
        grid=(nl, rows // tm),
        in_specs=[
            pl.BlockSpec((tm, D_MODEL), lambda l, i: (i, 0)),
            pl.BlockSpec((None, D_MODEL, D_MODEL), lambda l, i: (l, 0, 0)),
            pl.BlockSpec((None, D_MODEL, D_MODEL), lambda l, i: (l, 0, 0)),
        ],
        out_specs=[pl.BlockSpec((None, tm, D_MODEL), lambda l, i: (l, i, 0)),
                   pl.BlockSpec((None, tm, D_MODEL), lambda l, i: (l, i, 0))],
        out_shape=(jax.ShapeDtypeStruct((nl, rows, D_MODEL), F32),
                   jax.ShapeDtypeStruct((nl, rows, D_MODEL), F32)),
        compiler_params=_cparams(("parallel", "parallel")),
        name="memkv",
    )(mem, w_k, w_v)


def _memattn_kernel(x_ref, k_ref, v_ref, wq_ref, wo_ref, g_ref, b_ref, o_ref):
    x = x_ref[...]
    q = _dot(x.astype(BF16), wq_ref[...]) * (MEM_HEAD_DIM ** -0.5)
    kb = k_ref[...].astype(BF16)
    vb = v_ref[...].astype(BF16)
    outs = []
    for h in range(MEM_HEADS):
        a = h * MEM_HEAD_DIM
        s = _dot_nt(q[:, a:a + MEM_HEAD_DIM].astype(BF16), kb[:, a:a + MEM_HEAD_DIM])
        s = s - jnp.max(s, axis=-1, keepdims=True)
        p = jnp.exp(s)
        l = jnp.sum(p, axis=-1, keepdims=True)
        outs.append(_dot(p.astype(BF16), vb[:, a:a + MEM_HEAD_DIM]) / l)
    o = jnp.concatenate(outs, axis=-1).astype(BF16)
    y = _dot(o, wo_ref[...])
    o_ref[...] = _layer_norm(ALPHA * x + y, g_ref[...], b_ref[...])


def _memattn(x, mem_k, mem_v, kv_layer, w_q, w_o, ln_g, ln_b, layer, nb, seq, tm):
    nt = seq // tm
    n_mem = mem_k.shape[2]
    lay = lambda b, i: (layer, 0, 0)
    return pl.pallas_call(
        _memattn_kernel,
        grid=(nb, nt),
        in_specs=[
            pl.BlockSpec((tm, D_MODEL), lambda b, i: (b * nt + i, 0)),
            pl.BlockSpec((None, None, n_mem, D_MODEL), lambda b, i: (kv_layer, b, 0, 0)),
            pl.BlockSpec((None, None, n_mem, D_MODEL), lambda b, i: (kv_layer, b, 0, 0)),
            _resident((None, D_MODEL, D_MODEL), lay),
            _resident((None, D_MODEL, D_MODEL), lay),
            _resident((None, None, 1, D_MODEL), lambda b, i: (layer, 2, 0, 0)),
            _resident((None, None, 1, D_MODEL), lambda b, i: (layer, 2, 0, 0)),
        ],
        out_specs=pl.BlockSpec((tm, D_MODEL), lambda b, i: (b * nt + i, 0)),
        out_shape=jax.ShapeDtypeStruct((nb * seq, D_MODEL), F32),
        compiler_params=_cparams(("parallel", "parallel")),
        name="memattn",
    )(x, mem_k, mem_v, w_q, w_o, ln_g, ln_b)


def _pack_weights(w_in, pool_w, mla_w_uq, mla_w_uk, mla_w_uv, gla_w_a2):
    nl = w_in.shape[0]
    splits = (POOL_WIDTH, MLA_Q_RANK, MLA_KV_RANK, MLA_ROPE, GLA_K, GLA_K, GLA_V,
              GLA_GATE_RANK, GLA_V, 3 * D_MODEL)
    offs = [0]
    for s in splits:
        offs.append(offs[-1] + s)
    col = lambda n: w_in[:, :, offs[n]:offs[n + 1]]
    half = MLA_ROPE // 2
    kr = col(3)
    kr_sw = jnp.concatenate([kr[..., half:], kr[..., :half]], -1)
    zeros = lambda n: jnp.zeros((nl, D_MODEL, n), w_in.dtype)
    w_a = jnp.concatenate(
        [col(0), col(1), col(2),
         kr, col(7), zeros(MISC_W - MLA_ROPE - GLA_GATE_RANK),
         kr_sw, zeros(MISC_W - MLA_ROPE),
         col(4), col(5), col(6), col(8)], -1).astype(BF16)
    w_gates = col(9).astype(BF16)

    uq = mla_w_uq.reshape(nl, MLA_Q_RANK, MLA_HEADS, MLA_NOPE + MLA_ROPE)
    uq_n, uq_r = uq[..., :MLA_NOPE], uq[..., MLA_NOPE:]
    uq_rs = jnp.concatenate([uq_r[..., half:], uq_r[..., :half]], -1)
    padq = jnp.zeros((nl, MLA_Q_RANK, MLA_HEADS, MLA_HEAD_PAD - MLA_NOPE - MLA_ROPE), uq.dtype)
    q_main = jnp.concatenate([uq_n, uq_r, padq], -1)
    q_swap = jnp.concatenate([jnp.zeros_like(uq_n), uq_rs, padq], -1)
    kw = MLA_HEADS * MLA_HEAD_PAD
    w_q2 = jnp.concatenate([q_main.reshape(nl, MLA_Q_RANK, kw),
                            q_swap.reshape(nl, MLA_Q_RANK, kw)], -1).astype(BF16)

    padk = jnp.zeros((nl, MLA_KV_RANK, MLA_HEADS, MLA_HEAD_PAD - MLA_NOPE), mla_w_uk.dtype)
    k_lat = jnp.concatenate([mla_w_uk, padk], -1).reshape(nl, MLA_KV_RANK, kw)
    place = jnp.zeros((MISC_W, MLA_HEADS, MLA_HEAD_PAD), F32)
    idx = jnp.arange(MLA_ROPE)
    place = place.at[idx, :, MLA_NOPE + idx].set(1.0).reshape(MISC_W, kw)
    k_misc = jnp.broadcast_to(place, (nl, MISC_W, kw))
    w_k = jnp.concatenate([k_lat, k_misc], 1).astype(BF16)
    w_kt = jnp.swapaxes(w_k, 1, 2)
    uv = mla_w_uv.reshape(nl, MLA_KV_RANK, MLA_HEADS // MLA_PAIR, MLA_PAIR, MLA_V)
    padv = jnp.zeros_like(uv[:, :, :, 0])
    w_v = jnp.stack([jnp.concatenate([uv[:, :, :, 0], padv], -1),
                     jnp.concatenate([padv, uv[:, :, :, 1]], -1)], 3)
    w_v = w_v.reshape(nl, MLA_KV_RANK, kw).astype(BF16)

    uk_t = jnp.transpose(mla_w_uk, (0, 2, 3, 1))
    a_lat = jnp.concatenate(
        [uk_t, jnp.zeros((nl, MLA_HEADS, MLA_HEAD_PAD - MLA_NOPE, MLA_KV_RANK), uk_t.dtype)], 2)
    a_misc = jnp.zeros((MLA_HEAD_PAD, MISC_W), F32).at[MLA_NOPE + idx, idx].set(1.0)
    a_misc = jnp.broadcast_to(a_misc, (nl, MLA_HEADS, MLA_HEAD_PAD, MISC_W))
    w_abs = jnp.concatenate([a_lat, a_misc], -1).astype(BF16)
    eye_h = jnp.eye(MLA_HEADS, dtype=mla_w_uv.dtype)
    uv_h = jnp.transpose(mla_w_uv, (0, 2, 1, 3))
    w_uvp = (uv_h[:, :, :, None, :] * eye_h[None, :, None, :, None]).reshape(
        nl, MLA_HEADS, MLA_KV_RANK, MLA_HEADS * MLA_V).astype(BF16)

    ng = len(POOL_WINDOWS)
    eye = jnp.eye(ng, dtype=pool_w.dtype)
    pool_bd = (pool_w[:, :, :, None, :] * eye[None, :, None, :, None]).reshape(
        nl, POOL_WIDTH, POOL_WIDTH).astype(BF16)

    w_a2 = jnp.zeros((nl, MISC_W, GLA_K), gla_w_a2.dtype)
    w_a2 = w_a2.at[:, MLA_ROPE:MLA_ROPE + GLA_GATE_RANK, :].set(gla_w_a2).astype(BF16)
    return w_a, w_gates, w_q2, w_k, w_kt, w_v, w_abs, w_uvp, pool_bd, w_a2


def _rope_table(pos):
    half = MLA_ROPE // 2
    inv = ROPE_BASE ** (-jnp.arange(half, dtype=F32) / half)
    ang = pos.astype(F32)[:, None] * inv
    cos, sin = jnp.cos(ang), jnp.sin(ang)
    n = pos.shape[0]
    one = jnp.ones((n, GLA_GATE_RANK), F32)
    cm = jnp.concatenate([cos, cos, one, jnp.zeros((n, MISC_W - MLA_ROPE - GLA_GATE_RANK), F32)], -1)
    sm = jnp.concatenate([-sin, sin, jnp.zeros((n, MISC_W - MLA_ROPE), F32)], -1)
    padq = jnp.zeros((n, MLA_HEAD_PAD - MLA_NOPE - MLA_ROPE), F32)
    qs = MLA_SCALE * LOG2E
    cq = jnp.concatenate([jnp.ones((n, MLA_NOPE), F32), cos, cos, padq], -1) * qs
    sq = jnp.concatenate([jnp.zeros((n, MLA_NOPE), F32), -sin, sin, padq], -1) * qs
    return jnp.concatenate([cm, sm, cq, sq], -1)


def _state_to_tiles(s):
    eye = jnp.eye(GLA_HEADS, dtype=s.dtype)
    st = jnp.swapaxes(s, 2, 3)
    bd = st[:, :, :, None, :] * eye[None, :, None, :, None]
    return bd.reshape(s.shape[0], GLA_V, GLA_K)


def _tiles_to_state(t):
    b = t.shape[0]
    t5 = t.reshape(b, GLA_HEADS, GLA_DV, GLA_HEADS, GLA_DK)
    idx = jnp.arange(GLA_HEADS)
    diag = t5[:, idx, :, idx, :]
    return jnp.transpose(diag, (1, 0, 3, 2))


def _tile(n, pref):
    return pref if n % pref == 0 else n


def kernel(x_prompt, x_sample, mem_prompt, cache_pool, cache_mla_ckv, cache_mla_krope, state_gla,
           cache_mem_k, cache_mem_v, ln_g, ln_b, w_ffn1_in, w_ffn1_out, w_in, pool_w, pool_scale,
           mla_q_norm, mla_kv_norm, mla_w_uq, mla_w_uk, mla_w_uv, gla_w_a2, gla_b_a, gla_norm,
           w_branch_pool, w_branch_mla, w_branch_gla, w_out, mem_w_q, mem_w_k, mem_w_v, mem_w_o,
           w_ffn2_in, w_ffn2_out):
    nl = w_in.shape[0]
    bp, sp, _ = x_prompt.shape
    bs, ss, _ = x_sample.shape
    past = cache_mla_ckv.shape[2]
    n_mem = mem_prompt.shape[1]

    w_a, w_gates, w_q2, w_k, w_kt, w_v, w_abs, w_uvp, pool_bd, w_a2 = _pack_weights(
        w_in, pool_w, mla_w_uq, mla_w_uk, mla_w_uv, gla_w_a2)
    bf = lambda w: w.astype(BF16)
    ffn_w = ((bf(w_ffn1_in), bf(w_ffn1_out)), (bf(w_ffn2_in), bf(w_ffn2_out)))
    w_bp, w_bm, w_bg, w_o = bf(w_branch_pool), bf(w_branch_mla), bf(w_branch_gla), bf(w_out)
    mw_q, mw_k, mw_v, mw_o = bf(mem_w_q), bf(mem_w_k), bf(mem_w_v), bf(mem_w_o)
    lng = ln_g.reshape(nl, 4, 1, D_MODEL)
    lnb = ln_b.reshape(nl, 4, 1, D_MODEL)
    qnorm = mla_q_norm.reshape(nl, 1, MLA_Q_RANK)
    kvnorm = mla_kv_norm.reshape(nl, 1, MLA_KV_RANK)
    pscale = pool_scale.reshape(nl, 1, POOL_WIDTH)
    b_a = gla_b_a.reshape(nl, 1, GLA_K)
    norm_g = jnp.tile(gla_norm, (1, GLA_HEADS)).reshape(nl, 1, GLA_V)
    hsel = (jnp.arange(GLA_K)[:, None] // GLA_DK == jnp.arange(GLA_V)[None, :] // GLA_DV).astype(BF16)
    hblk = (jnp.arange(GLA_V)[:, None] // GLA_DV == jnp.arange(GLA_V)[None, :] // GLA_DV).astype(BF16)

    tmem = _tile(bp * n_mem, 512)
    mem_k, mem_v = _memkv(mem_prompt.reshape(bp * n_mem, D_MODEL), mw_k, mw_v, tmem)
    mem_k = mem_k.reshape(nl, bp, n_mem, D_MODEL)
    mem_v = mem_v.reshape(nl, bp, n_mem, D_MODEL)
    smem_k = cache_mem_k.reshape(nl, bs, n_mem, D_MODEL)
    smem_v = cache_mem_v.reshape(nl, bs, n_mem, D_MODEL)

    groups = {
        "p": dict(nb=bp, seq=sp, pos0=0, tm=_tile(bp * sp, 512)),
        "s": dict(nb=bs, seq=ss, pos0=past, tm=_tile(bs * ss, 512)),
    }
    for gname, gr in groups.items():
        seq, tm = gr["seq"], gr["tm"]
        pos = gr["pos0"] + jnp.arange(seq)
        tab = _rope_table(pos)
        if tm <= seq:
            gr["tab"], gr["tab_period"] = tab, seq // tm
        else:
            gr["tab"], gr["tab_period"] = jnp.tile(tab, (tm // seq, 1)), 1
        gr["tseq"] = _tile(seq, 512)
        gr["tgla"] = _tile(seq, 512)

    xs = {"p": x_prompt.reshape(bp * sp, D_MODEL), "s": x_sample.reshape(bs * ss, D_MODEL)}
    zero_hist = jnp.zeros((bp, POOL_PAD, POOL_WIDTH), F32)
    zero_state = jnp.zeros((bp, GLA_V, GLA_K), F32)
    outs = {n: {"pool": [], "ckv": [], "kr": [], "gla": []} for n in groups}

    for l in range(nl):
        for gname, gr in groups.items():
            nb, seq, tm = gr["nb"], gr["seq"], gr["tm"]
            x = xs[gname]
            x = _ffn(x, ffn_w[0][0], ffn_w[0][1], lng, lnb, l, 0, tm)
            k_transposed = gname == "p"
            u_pool, ckv, misc, q, k, v, gla_in = _inproj(
                x, gr["tab"], w_a, w_q2, w_kt if k_transposed else w_k, w_v, qnorm, kvnorm,
                l, tm, gr["tab_period"], k_transposed)
            if gname == "p":
                hist = None
                hist16 = zero_hist
                s0t = zero_state
            else:
                hist = cache_pool[l]
                hist16 = jnp.concatenate(
                    [jnp.zeros((nb, POOL_PAD - POOL_HIST, POOL_WIDTH), F32), hist], 1)
                s0t = _state_to_tiles(state_gla[l])
            y_pool = _pool(u_pool, hist16, pool_bd, pscale, l, nb, seq, gr["tseq"], gr["pos0"])
            if gname == "p":
                y_mla = _mla_causal(q, k, v, nb, seq, gr["tseq"])
            else:
                y_mla = _mla_cached(q, ckv, misc, cache_mla_ckv, cache_mla_krope, w_abs, w_uvp,
                                    l, nb, seq, past)
            y_gla, s_fin = _gla(gla_in, misc, w_a2, b_a, norm_g, hsel, hblk, s0t, l, nb, seq, gr["tgla"])
            x = _merge(x, y_pool, y_mla, y_gla, w_gates, w_bp, w_bm, w_bg, w_o, lng, lnb, l, tm)
            if gname == "p":
                x = _memattn(x, mem_k, mem_v, l, mw_q, mw_o, lng, lnb, l, nb, seq, gr["tseq"])
            else:
                x = _memattn(x, smem_k, smem_v, l, mw_q, mw_o, lng, lnb, l, nb, seq, gr["tseq"])
            x = _ffn(x, ffn_w[1][0], ffn_w[1][1], lng, lnb, l, 3, tm)
            xs[gname] = x

            u3 = u_pool.reshape(nb, seq, POOL_WIDTH)
            if hist is None:
                hist = jnp.zeros((nb, POOL_HIST, POOL_WIDTH), F32)
            new_pool = jnp.concatenate([hist, u3], 1)[:, -POOL_HIST:]
            o = outs[gname]
            o["pool"].append(new_pool)
            o["ckv"].append(ckv.reshape(nb, seq, MLA_KV_RANK))
            o["kr"].append(misc[:, :MLA_ROPE].reshape(nb, seq, MLA_ROPE))
            o["gla"].append(_tiles_to_state(s_fin))

    st = lambda n, key: jnp.stack(outs[n][key], 0)
    return (xs["p"].reshape(bp, sp, D_MODEL), xs["s"].reshape(bs, ss, D_MODEL),
            st("p", "pool"), st("s", "pool"),
            st("p", "ckv"), st("s", "ckv"),
            st("p", "kr"), st("s", "kr"),
            st("p", "gla"), st("s", "gla"),
            mem_k.reshape(nl, bp, n_mem, MEM_HEADS, MEM_HEAD_DIM),
            mem_v.reshape(nl, bp, n_mem, MEM_HEADS, MEM_HEAD_DIM))
```

```python
import functools
import math

import jax
import jax.numpy as jnp
from jax import lax
from jax.experimental import pallas as pl
from jax.experimental.pallas import tpu as pltpu

F32 = jnp.float32
BF16 = jnp.bfloat16

D_MODEL = 1024
DEPTH = 4
CHUNK = 64
EPS = 1e-5
ALPHA = (2 * DEPTH) ** 0.25

POOL_WINDOWS = (2, 4, 8, 16)
POOL_GROUP = D_MODEL // 16
POOL_WIDTH = len(POOL_WINDOWS) * POOL_GROUP
POOL_HIST = max(POOL_WINDOWS) - 1

MLA_HEADS = 8
MLA_Q_RANK = D_MODEL // 4
MLA_KV_RANK = D_MODEL // 4
MLA_NOPE = 64
MLA_ROPE = 32
MLA_V = 64
MLA_SCALE = (MLA_NOPE + MLA_ROPE) ** -0.5
ROPE_BASE = 10000.0
MLA_HEAD_PAD = 128
MLA_PAIR = 2
MLA_GROUP = 4
V_ONE_EVEN = MLA_V
V_ONE_ODD = MLA_HEAD_PAD
LOG2E = math.log2(math.e)

GLA_HEADS = 4
GLA_DK = 32
GLA_DV = 64
GLA_GATE_RANK = 16
GLA_GATE_TAU = 16.0
GLA_K = GLA_HEADS * GLA_DK
GLA_V = GLA_HEADS * GLA_DV

MEM_HEADS = 4
MEM_HEAD_DIM = D_MODEL // MEM_HEADS
D_FF = 2816

V7X_LANES = 128
V7X_SUBLANES = 8
V7X_VMEM_BYTES = 64 * 1024 * 1024
VMEM_LIMIT = V7X_VMEM_BYTES - 8 * 1024 * 1024

A_POOL = 0
A_QLAT = 256
A_CKV = 512
A_M1 = 768
A_M2 = 896
A_GLA = 1024
A_COLS = 1792
MISC_W = 128
GLA_IN_W = 2 * GLA_K + 2 * GLA_V


def _cparams(sem):
    return pltpu.CompilerParams(dimension_semantics=sem, vmem_limit_bytes=VMEM_LIMIT)


def _resident(shape, index_map):
    return pl.BlockSpec(shape, index_map, pipeline_mode=pl.Buffered(1))


def _layer_norm(y, g, b):
    mu = jnp.mean(y, axis=-1, keepdims=True)
    d = y - mu
    var = jnp.mean(d * d, axis=-1, keepdims=True)
    return d * lax.rsqrt(var + EPS) * g + b


def _rms_norm(y, g):
    return y * lax.rsqrt(jnp.mean(y * y, axis=-1, keepdims=True) + EPS) * g


def _dot(a, b):
    return jnp.dot(a, b, preferred_element_type=F32)


def _dot_nt(a, b):
    return lax.dot_general(a, b, (((1,), (1,)), ((), ())), preferred_element_type=F32)


def _silu(x):
    return x * jax.nn.sigmoid(x)


FFN_CHUNKS = ((0, 1024), (1024, 1024), (2048, 768))


def _ffn_kernel(x_ref, win_ref, wout_ref, g_ref, b_ref, o_ref, act_ref):
    x = x_ref[...]
    xb = x.astype(BF16)
    for c0, cw in FFN_CHUNKS:
        gate = _dot(xb, win_ref[:, c0:c0 + cw])
        up = _dot(xb, win_ref[:, D_FF + c0:D_FF + c0 + cw])
        act_ref[:, c0:c0 + cw] = (_silu(gate) * up).astype(BF16)
    y = _dot(act_ref[...], wout_ref[...])
    o_ref[...] = _layer_norm(ALPHA * x + 0.5 * y, g_ref[...], b_ref[...])


def _ffn(x, w_in, w_out, ln_g, ln_b, layer, which, tm):
    t = x.shape[0]
    return pl.pallas_call(
        _ffn_kernel,
        grid=(t // tm,),
        in_specs=[
            pl.BlockSpec((tm, D_MODEL), lambda i: (i, 0)),
            _resident((None, D_MODEL, 2 * D_FF), lambda i: (layer, 0, 0)),
            _resident((None, D_FF, D_MODEL), lambda i: (layer, 0, 0)),
            _resident((None, None, 1, D_MODEL), lambda i: (layer, which, 0, 0)),
            _resident((None, None, 1, D_MODEL), lambda i: (layer, which, 0, 0)),
        ],
        out_specs=pl.BlockSpec((tm, D_MODEL), lambda i: (i, 0)),
        out_shape=jax.ShapeDtypeStruct((t, D_MODEL), F32),
        scratch_shapes=[pltpu.VMEM((tm, D_FF), BF16)],
        compiler_params=_cparams(("parallel",)),
        name="ffn",
    )(x, w_in, w_out, ln_g, ln_b)


def _v_with_ones(vv):
    lane = lax.broadcasted_iota(jnp.int32, vv.shape, 1) % (MLA_PAIR * MLA_HEAD_PAD)
    return jnp.where((lane == V_ONE_EVEN) | (lane == V_ONE_ODD), 1.0, vv).astype(BF16)


def _inproj_kernel(x_ref, tab_ref, wa_ref, wq_ref, wk_ref, wv_ref, qn_ref, kvn_ref,
                   pool_ref, ckv_ref, misc_ref, q_ref, k_ref, v_ref, gla_ref, *, k_transposed):
    xb = x_ref[...].astype(BF16)
    h = _dot(xb, wa_ref[:, 0:A_GLA])
    gla_ref[...] = _dot(xb, wa_ref[:, A_GLA:A_COLS])
    pool_ref[...] = h[:, A_POOL:A_POOL + POOL_WIDTH]

    cm = tab_ref[:, 0:128]
    sm = tab_ref[:, 128:256]
    misc = h[:, A_M1:A_M1 + MISC_W] * cm + h[:, A_M2:A_M2 + MISC_W] * sm
    misc_ref[...] = misc

    ckv = _rms_norm(h[:, A_CKV:A_CKV + MLA_KV_RANK], kvn_ref[...])
    ckv_ref[...] = ckv
    ckvb = ckv.astype(BF16)
    lat = jnp.concatenate([ckvb, misc.astype(BF16)], axis=-1)
    if k_transposed:
        k_ref[...] = _dot_nt(wk_ref[...], lat).astype(BF16)
    else:
        k_ref[...] = _dot(lat, wk_ref[...]).astype(BF16)
    v_ref[...] = _v_with_ones(_dot(ckvb, wv_ref[...]))

    qn = _rms_norm(h[:, A_QLAT:A_QLAT + MLA_Q_RANK], qn_ref[...]).astype(BF16)
    q2 = _dot(qn, wq_ref[...])
    kw = MLA_HEADS * MLA_HEAD_PAD
    cq = tab_ref[:, 256:384]
    sq = tab_ref[:, 384:512]
    for hd in range(MLA_HEADS):
        a = hd * MLA_HEAD_PAD
        q_ref[:, a:a + MLA_HEAD_PAD] = (
            q2[:, a:a + MLA_HEAD_PAD] * cq + q2[:, kw + a:kw + a + MLA_HEAD_PAD] * sq).astype(BF16)


def _inproj(x, tab, w_a, w_q2, w_k, w_v, qnorm, kvnorm, layer, tm, tab_period, k_transposed):
    t = x.shape[0]
    kw = MLA_HEADS * MLA_HEAD_PAD
    lat_w = MLA_KV_RANK + MISC_W
    row = lambda i: (i, 0)
    if k_transposed:
        k_shape = jax.ShapeDtypeStruct((t // tm, kw, tm), BF16)
        k_spec = pl.BlockSpec((None, kw, tm), lambda i: (i, 0, 0))
        wk_spec = _resident((None, kw, lat_w), lambda i: (layer, 0, 0))
    else:
        k_shape = jax.ShapeDtypeStruct((t, kw), BF16)
        k_spec = pl.BlockSpec((tm, kw), row)
        wk_spec = _resident((None, lat_w, kw), lambda i: (layer, 0, 0))
    out_shapes = (
        jax.ShapeDtypeStruct((t, POOL_WIDTH), F32),
        jax.ShapeDtypeStruct((t, MLA_KV_RANK), F32),
        jax.ShapeDtypeStruct((t, MISC_W), F32),
        jax.ShapeDtypeStruct((t, kw), BF16),
        k_shape,
        jax.ShapeDtypeStruct((t, kw), BF16),
        jax.ShapeDtypeStruct((t, GLA_IN_W), F32),
    )
    out_specs = [pl.BlockSpec((tm, s.shape[-1]), row) for s in out_shapes]
    out_specs[4] = k_spec
    return pl.pallas_call(
        functools.partial(_inproj_kernel, k_transposed=k_transposed),
        grid=(t // tm,),
        in_specs=[
            pl.BlockSpec((tm, D_MODEL), row),
            pl.BlockSpec((tm, 512), lambda i: (i % tab_period, 0)),
            _resident((None, D_MODEL, A_COLS), lambda i: (layer, 0, 0)),
            _resident((None, MLA_Q_RANK, 2 * kw), lambda i: (layer, 0, 0)),
            wk_spec,
            _resident((None, MLA_KV_RANK, kw), lambda i: (layer, 0, 0)),
            _resident((None, 1, MLA_Q_RANK), lambda i: (layer, 0, 0)),
            _resident((None, 1, MLA_KV_RANK), lambda i: (layer, 0, 0)),
        ],
        out_specs=out_specs,
        out_shape=out_shapes,
        compiler_params=_cparams(("parallel",)),
        name="inproj",
    )(x, tab, w_a, w_q2, w_k, w_v, qnorm, kvnorm)


POOL_PAD = 16


def _pool_kernel(u_ref, hist_ref, w_ref, scale_ref, y_ref, ext_ref, *, tp, pos0):
    j = pl.program_id(1)

    @pl.when(j == 0)
    def _():
        ext_ref[0:POOL_PAD, :] = hist_ref[...]

    tok = u_ref[...]
    ext_ref[POOL_PAD:POOL_PAD + tp, :] = tok
    acc = ext_ref[...]
    sums = {}
    w = 1
    while w < max(POOL_WINDOWS):
        acc = acc + pltpu.roll(acc, w, 0)
        w *= 2
        if w in POOL_WINDOWS:
            sums[w] = acc[POOL_PAD:POOL_PAD + tp, :]
    lane = lax.broadcasted_iota(jnp.int32, (tp, POOL_WIDTH), 1)
    pos = pos0 + j * tp + lax.broadcasted_iota(jnp.int32, (tp, POOL_WIDTH), 0)
    total = None
    width = None
    for gi, w in enumerate(POOL_WINDOWS):
        in_group = lane >= gi * POOL_GROUP
        total = sums[w] if total is None else jnp.where(in_group, sums[w], total)
        width = jnp.full_like(pos, w) if width is None else jnp.where(in_group, w, width)
    cnt = jnp.minimum(pos + 1, width).astype(F32)
    pooled = total / cnt - tok
    y = _dot(pooled.astype(BF16), w_ref[...]) * scale_ref[...]
    y_ref[...] = y.astype(BF16)
    ext_ref[0:POOL_PAD, :] = ext_ref[tp:tp + POOL_PAD, :]


def _pool(u, hist16, w_bd, scale, layer, nb, seq, tp, pos0):
    nt = seq // tp
    return pl.pallas_call(
        functools.partial(_pool_kernel, tp=tp, pos0=pos0),
        grid=(nb, nt),
        in_specs=[
            pl.BlockSpec((tp, POOL_WIDTH), lambda b, j: (b * nt + j, 0)),
            pl.BlockSpec((None, POOL_PAD, POOL_WIDTH), lambda b, j: (b, 0, 0)),
            _resident((None, POOL_WIDTH, POOL_WIDTH), lambda b, j: (layer, 0, 0)),
            _resident((None, 1, POOL_WIDTH), lambda b, j: (layer, 0, 0)),
        ],
        out_specs=pl.BlockSpec((tp, POOL_WIDTH), lambda b, j: (b * nt + j, 0)),
        out_shape=jax.ShapeDtypeStruct((nb * seq, POOL_WIDTH), BF16),
        scratch_shapes=[pltpu.VMEM((POOL_PAD + tp, POOL_WIDTH), F32)],
        compiler_params=_cparams(("parallel", "arbitrary")),
        name="pool",
    )(u, hist16, w_bd, scale)


def _pair_output(acc0, acc1):
    lane = lax.broadcasted_iota(jnp.int32, acc0.shape, 1)
    o0 = acc0 / acc0[:, V_ONE_EVEN:V_ONE_EVEN + 1]
    o1 = acc1 / acc1[:, 0:1]
    return jnp.where(lane < MLA_V, o0, o1).astype(BF16)


def _mla_causal_kernel(q_ref, kt_ref, v_ref, o_ref, m_ref, acc_ref, sa_ref, sb_ref, *, tq):
    i = pl.program_id(2)
    m_ref[...] = jnp.full(m_ref.shape, -jnp.inf, F32)
    acc_ref[...] = jnp.zeros(acc_ref.shape, F32)
    reps = tq // V7X_LANES

    def scores(kt, dst_ref):
        for h in range(MLA_GROUP):
            a = h * MLA_HEAD_PAD
            dst_ref[h] = _dot(q_ref[:, a:a + MLA_HEAD_PAD], kt_ref[kt, a:a + MLA_HEAD_PAD, :])

    def consume(src_ref, kt, masked):
        r0 = pl.multiple_of(kt * tq, tq)
        for h in range(MLA_GROUP):
            a = h * MLA_HEAD_PAD
            s = src_ref[h]
            if masked:
                qc = lax.broadcasted_iota(jnp.int32, (tq, tq), 0) // CHUNK
                kc = lax.broadcasted_iota(jnp.int32, (tq, tq), 1) // CHUNK
                s = jnp.where(kc <= qc, s, -jnp.inf)
            m_prev = m_ref[h]
            m_new = jnp.maximum(m_prev, jnp.max(s, axis=-1, keepdims=True))
            p = jnp.exp2(s - jnp.tile(m_new, (1, reps)))
            corr = jnp.exp2(m_prev - m_new)
            acc_ref[h] = corr * acc_ref[h] + _dot(p.astype(BF16),
                                                  v_ref[pl.ds(r0, tq), a:a + MLA_HEAD_PAD])
            m_ref[h] = m_new

    scores(0, sa_ref)

    def body(j, carry):
        scores(2 * j + 1, sb_ref)
        consume(sa_ref, 2 * j, False)
        scores(2 * j + 2, sa_ref)
        consume(sb_ref, 2 * j + 1, False)
        return carry

    lax.fori_loop(0, i // 2, body, 0)

    @pl.when(i % 2 == 0)
    def _():
        consume(sa_ref, i, True)

    @pl.when(i % 2 == 1)
    def _():
        scores(i, sb_ref)
        consume(sa_ref, i - 1, False)
        consume(sb_ref, i, True)

    ow = MLA_PAIR * MLA_V
    for pr in range(MLA_GROUP // MLA_PAIR):
        o_ref[:, pr * ow:(pr + 1) * ow] = _pair_output(acc_ref[MLA_PAIR * pr], acc_ref[MLA_PAIR * pr + 1])


def _mla_causal(q, kt, v, nb, seq, tq):
    nq = seq // tq
    ngrp = MLA_HEADS // MLA_GROUP
    pw = MLA_GROUP * MLA_HEAD_PAD
    return pl.pallas_call(
        functools.partial(_mla_causal_kernel, tq=tq),
        grid=(nb, ngrp, nq),
        in_specs=[
            pl.BlockSpec((tq, pw), lambda b, p, i: (b * nq + i, p)),
            pl.BlockSpec((nq, pw, tq), lambda b, p, i: (b, p, 0)),
            pl.BlockSpec((seq, pw), lambda b, p, i: (b, p)),
        ],
        out_specs=pl.BlockSpec((tq, MLA_GROUP * MLA_V), lambda b, p, i: (b * nq + i, p)),
        out_shape=jax.ShapeDtypeStruct((nb * seq, MLA_HEADS * MLA_V), BF16),
        scratch_shapes=[pltpu.VMEM((MLA_GROUP, tq, V7X_LANES), F32),
                        pltpu.VMEM((MLA_GROUP, tq, MLA_HEAD_PAD), F32),
                        pltpu.VMEM((MLA_GROUP, tq, tq), F32),
                        pltpu.VMEM((MLA_GROUP, tq, tq), F32)],
        compiler_params=_cparams(("parallel", "parallel", "arbitrary")),
        name="mla_causal",
    )(q, kt, v)


def _mla_cached_kernel(q_ref, cp_ref, rp_ref, cn_ref, mn_ref, wabs_ref, wuv_ref, o_ref):
    qa = jnp.concatenate(
        [_dot(q_ref[:, h * MLA_HEAD_PAD:(h + 1) * MLA_HEAD_PAD], wabs_ref[h]) for h in range(MLA_HEADS)],
        axis=0).astype(BF16)
    qa_lat = qa[:, 0:MLA_KV_RANK]
    qa_misc = qa[:, MLA_KV_RANK:]
    cp = cp_ref[...].astype(BF16)
    cn = cn_ref[...].astype(BF16)
    s1 = _dot_nt(qa_lat, cp) + _dot_nt(qa_misc[:, 0:MLA_ROPE], rp_ref[...].astype(BF16))
    s2 = _dot_nt(qa_lat, cn) + _dot_nt(qa_misc, mn_ref[...].astype(BF16))
    m = jnp.maximum(jnp.max(s1, axis=-1, keepdims=True), jnp.max(s2, axis=-1, keepdims=True))
    p1 = jnp.exp2(s1 - m)
    p2 = jnp.exp2(s2 - m)
    l = jnp.sum(p1, axis=-1, keepdims=True) + jnp.sum(p2, axis=-1, keepdims=True)
    lat = ((_dot(p1.astype(BF16), cp) + _dot(p2.astype(BF16), cn)) / l).astype(BF16)
    seq = q_ref.shape[0]
    y = _dot(lat[0:seq], wuv_ref[0])
    for h in range(1, MLA_HEADS):
        y = y + _dot(lat[h * seq:(h + 1) * seq], wuv_ref[h])
    o_ref[...] = y.astype(BF16)


def _mla_cached(q, ckv_new, misc_new, ckv_past, kr_past, w_abs, w_uvp, layer, nb, seq, past):
    kw = MLA_HEADS * MLA_HEAD_PAD
    vw = MLA_HEADS * MLA_V
    lat_w = MLA_KV_RANK + MISC_W
    return pl.pallas_call(
        _mla_cached_kernel,
        grid=(nb,),
        in_specs=[
            pl.BlockSpec((seq, kw), lambda b: (b, 0)),
            pl.BlockSpec((None, None, past, MLA_KV_RANK), lambda b: (layer, b, 0, 0)),
            pl.BlockSpec((None, None, past, MLA_ROPE), lambda b: (layer, b, 0, 0)),
            pl.BlockSpec((seq, MLA_KV_RANK), lambda b: (b, 0)),
            pl.BlockSpec((seq, MISC_W), lambda b: (b, 0)),
            _resident((None, MLA_HEADS, MLA_HEAD_PAD, lat_w), lambda b: (layer, 0, 0, 0)),
            _resident((None, MLA_HEADS, MLA_KV_RANK, vw), lambda b: (layer, 0, 0, 0)),
        ],
        out_specs=pl.BlockSpec((seq, vw), lambda b: (b, 0)),
        out_shape=jax.ShapeDtypeStruct((nb * seq, vw), BF16),
        compiler_params=_cparams(("parallel",)),
        name="mla_cached",
    )(q, ckv_past, kr_past, ckv_new, misc_new, w_abs, w_uvp)


def _gla_kernel(gq_ref, gk_ref, gv_ref, gr_ref, misc_ref, wa2_ref, ba_ref, ng_ref, hsel_ref,
                hblk_ref, s0_ref, y_ref, sfin_ref, st_ref, sec_ref, fst_ref, same_ref, kbd_ref,
                vbd_ref, sbd_ref, o_ref, *, tt):
    j = pl.program_id(1)
    c = min(CHUNK, tt)
    hc = GLA_HEADS * c
    levels = c.bit_length() - 1

    @pl.when(j == 0)
    def _():
        st_ref[...] = s0_ref[...]
        row = lax.broadcasted_iota(jnp.int32, (c, GLA_K), 0)
        trow = lax.broadcasted_iota(jnp.int32, (c, hc), 0)
        tcol = lax.broadcasted_iota(jnp.int32, (c, hc), 1) % c
        for lv in range(levels):
            m = 1 << lv
            sec_ref[lv] = ((row & m) != 0).astype(F32)
            fst_ref[lv] = ((row & m) == 0).astype(F32)
            same_ref[lv] = ((trow // (2 * m)) == (tcol // (2 * m))).astype(F32)
        kbd_ref[...] = (lax.broadcasted_iota(jnp.int32, (hc, GLA_K), 0) // c
                        == lax.broadcasted_iota(jnp.int32, (hc, GLA_K), 1) // GLA_DK).astype(BF16)
        vbd_ref[...] = (lax.broadcasted_iota(jnp.int32, (hc, GLA_V), 0) // c
                        == lax.broadcasted_iota(jnp.int32, (hc, GLA_V), 1) // GLA_DV).astype(BF16)
        sbd_ref[...] = (lax.broadcasted_iota(jnp.int32, (GLA_V, GLA_K), 0) // GLA_DV
                        == lax.broadcasted_iota(jnp.int32, (GLA_V, GLA_K), 1) // GLA_DK).astype(F32)

    z = _dot(misc_ref[...].astype(BF16), wa2_ref[...]) + ba_ref[...]
    la_all = jax.nn.log_sigmoid(z) * (LOG2E / GLA_GATE_TAU)

    st = st_ref[...]
    for r in range(tt // c):
        rs = slice(r * c, (r + 1) * c)
        q = gq_ref[rs, :] * (GLA_DK ** -0.5)
        k = gk_ref[rs, :]
        v = gv_ref[rs, :]
        la = la_all[rs, :]

        scores = jnp.zeros((c, hc), F32)
        p = la
        g = la
        for lv in range(levels):
            m = 1 << lv
            if m < V7X_SUBLANES:
                sec = sec_ref[lv]
                fst = fst_ref[lv]
                qe = (q * jnp.exp2(p) * sec).astype(BF16)
                ke = (k * jnp.exp2(g - p) * fst).astype(BF16)
                ke_bd = jnp.tile(ke, (GLA_HEADS, 1)) * kbd_ref[...]
                scores = scores + _dot_nt(qe, ke_bd) * same_ref[lv]
                up = sec * pltpu.roll(g, m, 0)
                p = p + up
                g = g + up + fst * pltpu.roll(g, c - m, 0)
            else:
                nb2 = c // (2 * m)

                def halves(a):
                    a4 = a.reshape(nb2, 2, m, a.shape[-1])
                    return a4[:, 0], a4[:, 1]

                def join(lo, hi):
                    return jnp.stack([lo, hi], axis=1).reshape(c, lo.shape[-1])

                p_lo, p_hi = halves(p)
                g_lo, g_hi = halves(g)
                qe_hi = (halves(q)[1] * jnp.exp2(p_hi)).reshape(c // 2, GLA_K).astype(BF16)
                ke_lo = halves(k)[0] * jnp.exp2(g_lo - p_lo)
                ke = join(ke_lo, jnp.zeros_like(ke_lo)).astype(BF16)
                ke_bd = jnp.tile(ke, (GLA_HEADS, 1)) * kbd_ref[...]
                a_hi = _dot_nt(qe_hi, ke_bd) * halves(same_ref[lv])[1].reshape(c // 2, hc)
                s_lo, s_hi = halves(scores)
                scores = join(s_lo, s_hi + a_hi.reshape(nb2, m, hc))
                tot = g_lo + g_hi
                p = join(p_lo, p_hi + g_lo)
                g = join(tot, tot)

        o = _dot_nt((q * jnp.exp2(p)).astype(BF16), st.astype(BF16))
        v_bd = jnp.tile(v.astype(BF16), (GLA_HEADS, 1)) * vbd_ref[...]
        o_ref[rs, :] = o + _dot(scores.astype(BF16), v_bd)

        kdec = (k * jnp.exp2(g - p)).astype(BF16)
        upd = _dot(jnp.transpose(v).astype(BF16), kdec)
        st = st * jnp.exp2(g[0:1, :]) + upd * sbd_ref[...]

    st_ref[...] = st
    sfin_ref[...] = st

    qk = (gq_ref[...] * (GLA_DK ** -0.5) * gk_ref[...]).astype(BF16)
    o = o_ref[...] + _dot(qk, hsel_ref[...]) * gv_ref[...]
    o2 = o * o
    o2_hi = o2.astype(BF16)
    o2_lo = (o2 - o2_hi.astype(F32)).astype(BF16)
    ms = (_dot(o2_hi, hblk_ref[...]) + _dot(o2_lo, hblk_ref[...])) * (1.0 / GLA_DV)
    y = o * lax.rsqrt(ms + EPS) * ng_ref[...] * _silu(gr_ref[...])
    y_ref[...] = y.astype(BF16)


def _gla(gla_in, misc, w_a2, b_a, norm_g, hsel, hblk, s0t, layer, nb, seq, tt):
    nt = seq // tt
    row = lambda b, j: (b * nt + j, 0)
    c = min(CHUNK, tt)
    hc = GLA_HEADS * c
    levels = c.bit_length() - 1
    return pl.pallas_call(
        functools.partial(_gla_kernel, tt=tt),
        grid=(nb, nt),
        in_specs=[
            pl.BlockSpec((tt, GLA_K), lambda b, j: (b * nt + j, 0)),
            pl.BlockSpec((tt, GLA_K), lambda b, j: (b * nt + j, 1)),
            pl.BlockSpec((tt, GLA_V), lambda b, j: (b * nt + j, 1)),
            pl.BlockSpec((tt, GLA_V), lambda b, j: (b * nt + j, 2)),
            pl.BlockSpec((tt, MISC_W), row),
            _resident((None, MISC_W, GLA_K), lambda b, j: (layer, 0, 0)),
            _resident((None, 1, GLA_K), lambda b, j: (layer, 0, 0)),
            _resident((None, 1, GLA_V), lambda b, j: (layer, 0, 0)),
            _resident((GLA_K, GLA_V), lambda b, j: (0, 0)),
            _resident((GLA_V, GLA_V), lambda b, j: (0, 0)),
            pl.BlockSpec((None, GLA_V, GLA_K), lambda b, j: (b, 0, 0)),
        ],
        out_specs=[pl.BlockSpec((tt, GLA_V), row),
                   pl.BlockSpec((None, GLA_V, GLA_K), lambda b, j: (b, 0, 0))],
        out_shape=(jax.ShapeDtypeStruct((nb * seq, GLA_V), BF16),
                   jax.ShapeDtypeStruct((nb, GLA_V, GLA_K), F32)),
        scratch_shapes=[pltpu.VMEM((GLA_V, GLA_K), F32),
                        pltpu.VMEM((levels, c, GLA_K), F32),
                        pltpu.VMEM((levels, c, GLA_K), F32),
                        pltpu.VMEM((levels, c, hc), F32),
                        pltpu.VMEM((hc, GLA_K), BF16),
                        pltpu.VMEM((hc, GLA_V), BF16),
                        pltpu.VMEM((GLA_V, GLA_K), F32),
                        pltpu.VMEM((tt, GLA_V), F32)],
        compiler_params=_cparams(("parallel", "arbitrary")),
        name="gla",
    )(gla_in, gla_in, gla_in, gla_in, misc, w_a2, b_a, norm_g, hsel, hblk, s0t)


def _merge_kernel(x_ref, yp_ref, ym_ref, yg_ref, wg_ref, wbp_ref, wbm_ref, wbg_ref, wo_ref,
                  g_ref, b_ref, o_ref):
    x = x_ref[...]
    xb = x.astype(BF16)
    m = jax.nn.sigmoid(_dot(xb, wg_ref[:, 0:D_MODEL])) * _dot(yp_ref[...], wbp_ref[...])
    m = m + jax.nn.sigmoid(_dot(xb, wg_ref[:, D_MODEL:2 * D_MODEL])) * _dot(ym_ref[...], wbm_ref[...])
    m = m + jax.nn.sigmoid(_dot(xb, wg_ref[:, 2 * D_MODEL:3 * D_MODEL])) * _dot(yg_ref[...], wbg_ref[...])
    y = _dot(m.astype(BF16), wo_ref[...])
    o_ref[...] = _layer_norm(ALPHA * x + y, g_ref[...], b_ref[...])


def _merge(x, y_pool, y_mla, y_gla, w_gates, w_bp, w_bm, w_bg, w_out, ln_g, ln_b, layer, tm):
    t = x.shape[0]
    row = lambda i: (i, 0)
    lay = lambda i: (layer, 0, 0)
    return pl.pallas_call(
        _merge_kernel,
        grid=(t // tm,),
        in_specs=[
            pl.BlockSpec((tm, D_MODEL), row),
            pl.BlockSpec((tm, POOL_WIDTH), row),
            pl.BlockSpec((tm, MLA_HEADS * MLA_V), row),
            pl.BlockSpec((tm, GLA_V), row),
            _resident((None, D_MODEL, 3 * D_MODEL), lay),
            _resident((None, POOL_WIDTH, D_MODEL), lay),
            _resident((None, MLA_HEADS * MLA_V, D_MODEL), lay),
            _resident((None, GLA_V, D_MODEL), lay),
            _resident((None, D_MODEL, D_MODEL), lay),
            _resident((None, None, 1, D_MODEL), lambda i: (layer, 1, 0, 0)),
            _resident((None, None, 1, D_MODEL), lambda i: (layer, 1, 0, 0)),
        ],
        out_specs=pl.BlockSpec((tm, D_MODEL), row),
        out_shape=jax.ShapeDtypeStruct((t, D_MODEL), F32),
        compiler_params=_cparams(("parallel",)),
        name="merge",
    )(x, y_pool, y_mla, y_gla, w_gates, w_bp, w_bm, w_bg, w_out, ln_g, ln_b)


def _memkv_kernel(m_ref, wk_ref, wv_ref, k_ref, v_ref, kb_ref, vb_ref):
    mb = m_ref[...].astype(BF16)
    k = _dot(mb, wk_ref[...])
    v = _dot(mb, wv_ref[...])
    kb_ref[...] = k.astype(BF16)
    vb_ref[...] = v.astype(BF16)
    for h in range(MEM_HEADS):
        a = h * MEM_HEAD_DIM
        k_ref[:, h, :] = k[:, a:a + MEM_HEAD_DIM]
        v_ref[:, h, :] = v[:, a:a + MEM_HEAD_DIM]


def _memkv(mem, w_k, w_v, tm):
    rows = mem.shape[0]
    nl = w_k.shape[0]
    split = pl.BlockSpec((None, tm, MEM_HEADS, MEM_HEAD_DIM), lambda l, i: (l, i, 0, 0))
    flat = pl.BlockSpec((None, tm, D_MODEL), lambda l, i: (l, i, 0))
    return pl.pallas_call(
        _memkv_kernel,
        grid=(nl, rows // tm),
        in_specs=[
            pl.BlockSpec((tm, D_MODEL), lambda l, i: (i, 0)),
            pl.BlockSpec((None, D_MODEL, D_MODEL), lambda l, i: (l, 0, 0)),
            pl.BlockSpec((None, D_MODEL, D_MODEL), lambda l, i: (l, 0, 0)),
        ],
        out_specs=[split, split, flat, flat],
        out_shape=(jax.ShapeDtypeStruct((nl, rows, MEM_HEADS, MEM_HEAD_DIM), F32),
                   jax.ShapeDtypeStruct((nl, rows, MEM_HEADS, MEM_HEAD_DIM), F32),
                   jax.ShapeDtypeStruct((nl, rows, D_MODEL), BF16),
                   jax.ShapeDtypeStruct((nl, rows, D_MODEL), BF16)),
        compiler_params=_cparams(("parallel", "parallel")),
        name="memkv",
    )(mem, w_k, w_v)


def _memattn_kernel(x_ref, k_ref, v_ref, wq_ref, wo_ref, g_ref, b_ref, o_ref):
    x = x_ref[...]
    q = (_dot(x.astype(BF16), wq_ref[...]) * (MEM_HEAD_DIM ** -0.5 * LOG2E)).astype(BF16)
    kb = k_ref[...].astype(BF16)
    vb = v_ref[...].astype(BF16)
    outs = []
    for h in range(MEM_HEADS):
        a = h * MEM_HEAD_DIM
        s = _dot_nt(q[:, a:a + MEM_HEAD_DIM], kb[:, a:a + MEM_HEAD_DIM])
        p = jnp.exp2(s - jnp.max(s, axis=-1, keepdims=True))
        inv_l = 1.0 / jnp.sum(p, axis=-1, keepdims=True)
        outs.append(_dot(p.astype(BF16), vb[:, a:a + MEM_HEAD_DIM]) * inv_l)
    o = jnp.concatenate(outs, axis=-1).astype(BF16)
    y = _dot(o, wo_ref[...])
    o_ref[...] = _layer_norm(ALPHA * x + y, g_ref[...], b_ref[...])


def _memattn(x, mem_k, mem_v, kv_layer, w_q, w_o, ln_g, ln_b, layer, nb, seq, tm):
    nt = seq // tm
    n_mem = mem_k.shape[2]
    lay = lambda b, i: (layer, 0, 0)
    return pl.pallas_call(
        _memattn_kernel,
        grid=(nb, nt),
        in_specs=[
            pl.BlockSpec((tm, D_MODEL), lambda b, i: (b * nt + i, 0)),
            pl.BlockSpec((None, None, n_mem, D_MODEL), lambda b, i: (kv_layer, b, 0, 0)),
            pl.BlockSpec((None, None, n_mem, D_MODEL), lambda b, i: (kv_layer, b, 0, 0)),
            _resident((None, D_MODEL, D_MODEL), lay),
            _resident((None, D_MODEL, D_MODEL), lay),
            _resident((None, None, 1, D_MODEL), lambda b, i: (layer, 2, 0, 0)),
            _resident((None, None, 1, D_MODEL), lambda b, i: (layer, 2, 0, 0)),
        ],
        out_specs=pl.BlockSpec((tm, D_MODEL), lambda b, i: (b * nt + i, 0)),
        out_shape=jax.ShapeDtypeStruct((nb * seq, D_MODEL), F32),
        compiler_params=_cparams(("parallel", "parallel")),
        name="memattn",
    )(x, mem_k, mem_v, w_q, w_o, ln_g, ln_b)


def _pack_weights(w_in, pool_w, mla_w_uq, mla_w_uk, mla_w_uv, gla_w_a2):
    nl = w_in.shape[0]
    splits = (POOL_WIDTH, MLA_Q_RANK, MLA_KV_RANK, MLA_ROPE, GLA_K, GLA_K, GLA_V,
              GLA_GATE_RANK, GLA_V, 3 * D_MODEL)
    offs = [0]
    for s in splits:
        offs.append(offs[-1] + s)
    col = lambda n: w_in[:, :, offs[n]:offs[n + 1]]
    half = MLA_ROPE // 2
    kr = col(3)
    kr_sw = jnp.concatenate([kr[..., half:], kr[..., :half]], -1)
    zeros = lambda n: jnp.zeros((nl, D_MODEL, n), w_in.dtype)
    w_a = jnp.concatenate(
        [col(0), col(1), col(2),
         kr, col(7), zeros(MISC_W - MLA_ROPE - GLA_GATE_RANK),
         kr_sw, zeros(MISC_W - MLA_ROPE),
         col(4), col(5), col(6), col(8)], -1).astype(BF16)
    w_gates = col(9).astype(BF16)

    uq = mla_w_uq.reshape(nl, MLA_Q_RANK, MLA_HEADS, MLA_NOPE + MLA_ROPE)
    uq_n, uq_r = uq[..., :MLA_NOPE], uq[..., MLA_NOPE:]
    uq_rs = jnp.concatenate([uq_r[..., half:], uq_r[..., :half]], -1)
    padq = jnp.zeros((nl, MLA_Q_RANK, MLA_HEADS, MLA_HEAD_PAD - MLA_NOPE - MLA_ROPE), uq.dtype)
    q_main = jnp.concatenate([uq_n, uq_r, padq], -1)
    q_swap = jnp.concatenate([jnp.zeros_like(uq_n), uq_rs, padq], -1)
    kw = MLA_HEADS * MLA_HEAD_PAD
    w_q2 = jnp.concatenate([q_main.reshape(nl, MLA_Q_RANK, kw),
                            q_swap.reshape(nl, MLA_Q_RANK, kw)], -1).astype(BF16)

    padk = jnp.zeros((nl, MLA_KV_RANK, MLA_HEADS, MLA_HEAD_PAD - MLA_NOPE), mla_w_uk.dtype)
    k_lat = jnp.concatenate([mla_w_uk, padk], -1).reshape(nl, MLA_KV_RANK, kw)
    place = jnp.zeros((MISC_W, MLA_HEADS, MLA_HEAD_PAD), F32)
    idx = jnp.arange(MLA_ROPE)
    place = place.at[idx, :, MLA_NOPE + idx].set(1.0).reshape(MISC_W, kw)
    k_misc = jnp.broadcast_to(place, (nl, MISC_W, kw))
    w_k = jnp.concatenate([k_lat, k_misc], 1).astype(BF16)
    w_kt = jnp.swapaxes(w_k, 1, 2)
    uv = mla_w_uv.reshape(nl, MLA_KV_RANK, MLA_HEADS // MLA_PAIR, MLA_PAIR, MLA_V)
    padv = jnp.zeros_like(uv[:, :, :, 0])
    w_v = jnp.stack([jnp.concatenate([uv[:, :, :, 0], padv], -1),
                     jnp.concatenate([padv, uv[:, :, :, 1]], -1)], 3)
    w_v = w_v.reshape(nl, MLA_KV_RANK, kw).astype(BF16)

    uk_t = jnp.transpose(mla_w_uk, (0, 2, 3, 1))
    a_lat = jnp.concatenate(
        [uk_t, jnp.zeros((nl, MLA_HEADS, MLA_HEAD_PAD - MLA_NOPE, MLA_KV_RANK), uk_t.dtype)], 2)
    a_misc = jnp.zeros((MLA_HEAD_PAD, MISC_W), F32).at[MLA_NOPE + idx, idx].set(1.0)
    a_misc = jnp.broadcast_to(a_misc, (nl, MLA_HEADS, MLA_HEAD_PAD, MISC_W))
    w_abs = jnp.concatenate([a_lat, a_misc], -1).astype(BF16)
    eye_h = jnp.eye(MLA_HEADS, dtype=mla_w_uv.dtype)
    uv_h = jnp.transpose(mla_w_uv, (0, 2, 1, 3))
    w_uvp = (uv_h[:, :, :, None, :] * eye_h[None, :, None, :, None]).reshape(
        nl, MLA_HEADS, MLA_KV_RANK, MLA_HEADS * MLA_V).astype(BF16)

    ng = len(POOL_WINDOWS)
    eye = jnp.eye(ng, dtype=pool_w.dtype)
    pool_bd = (pool_w[:, :, :, None, :] * eye[None, :, None, :, None]).reshape(
        nl, POOL_WIDTH, POOL_WIDTH).astype(BF16)

    w_a2 = jnp.zeros((nl, MISC_W, GLA_K), gla_w_a2.dtype)
    w_a2 = w_a2.at[:, MLA_ROPE:MLA_ROPE + GLA_GATE_RANK, :].set(gla_w_a2).astype(BF16)
    return w_a, w_gates, w_q2, w_k, w_kt, w_v, w_abs, w_uvp, pool_bd, w_a2


def _rope_table(pos):
    half = MLA_ROPE // 2
    inv = ROPE_BASE ** (-jnp.arange(half, dtype=F32) / half)
    ang = pos.astype(F32)[:, None] * inv
    cos, sin = jnp.cos(ang), jnp.sin(ang)
    n = pos.shape[0]
    one = jnp.ones((n, GLA_GATE_RANK), F32)
    cm = jnp.concatenate([cos, cos, one, jnp.zeros((n, MISC_W - MLA_ROPE - GLA_GATE_RANK), F32)], -1)
    sm = jnp.concatenate([-sin, sin, jnp.zeros((n, MISC_W - MLA_ROPE), F32)], -1)
    padq = jnp.zeros((n, MLA_HEAD_PAD - MLA_NOPE - MLA_ROPE), F32)
    qs = MLA_SCALE * LOG2E
    cq = jnp.concatenate([jnp.ones((n, MLA_NOPE), F32), cos, cos, padq], -1) * qs
    sq = jnp.concatenate([jnp.zeros((n, MLA_NOPE), F32), -sin, sin, padq], -1) * qs
    return jnp.concatenate([cm, sm, cq, sq], -1)


def _state_to_tiles(s):
    eye = jnp.eye(GLA_HEADS, dtype=s.dtype)
    st = jnp.swapaxes(s, 2, 3)
    bd = st[:, :, :, None, :] * eye[None, :, None, :, None]
    return bd.reshape(s.shape[0], GLA_V, GLA_K)


def _tiles_to_state(t):
    b = t.shape[0]
    t5 = t.reshape(b, GLA_HEADS, GLA_DV, GLA_HEADS, GLA_DK)
    idx = jnp.arange(GLA_HEADS)
    diag = t5[:, idx, :, idx, :]
    return jnp.transpose(diag, (1, 0, 3, 2))


def _tile(n, pref):
    return pref if n % pref == 0 else n


def kernel(x_prompt, x_sample, mem_prompt, cache_pool, cache_mla_ckv, cache_mla_krope, state_gla,
           cache_mem_k, cache_mem_v, ln_g, ln_b, w_ffn1_in, w_ffn1_out, w_in, pool_w, pool_scale,
           mla_q_norm, mla_kv_norm, mla_w_uq, mla_w_uk, mla_w_uv, gla_w_a2, gla_b_a, gla_norm,
           w_branch_pool, w_branch_mla, w_branch_gla, w_out, mem_w_q, mem_w_k, mem_w_v, mem_w_o,
           w_ffn2_in, w_ffn2_out):
    nl = w_in.shape[0]
    bp, sp, _ = x_prompt.shape
    bs, ss, _ = x_sample.shape
    past = cache_mla_ckv.shape[2]
    n_mem = mem_prompt.shape[1]

    w_a, w_gates, w_q2, w_k, w_kt, w_v, w_abs, w_uvp, pool_bd, w_a2 = _pack_weights(
        w_in, pool_w, mla_w_uq, mla_w_uk, mla_w_uv, gla_w_a2)
    bf = lambda w: w.astype(BF16)
    ffn_w = ((bf(w_ffn1_in), bf(w_ffn1_out)), (bf(w_ffn2_in), bf(w_ffn2_out)))
    w_bp, w_bm, w_bg, w_o = bf(w_branch_pool), bf(w_branch_mla), bf(w_branch_gla), bf(w_out)
    mw_q, mw_k, mw_v, mw_o = bf(mem_w_q), bf(mem_w_k), bf(mem_w_v), bf(mem_w_o)
    lng = ln_g.reshape(nl, 4, 1, D_MODEL)
    lnb = ln_b.reshape(nl, 4, 1, D_MODEL)
    qnorm = mla_q_norm.reshape(nl, 1, MLA_Q_RANK)
    kvnorm = mla_kv_norm.reshape(nl, 1, MLA_KV_RANK)
    pscale = pool_scale.reshape(nl, 1, POOL_WIDTH)
    b_a = gla_b_a.reshape(nl, 1, GLA_K)
    norm_g = jnp.tile(gla_norm, (1, GLA_HEADS)).reshape(nl, 1, GLA_V)
    hsel = (jnp.arange(GLA_K)[:, None] // GLA_DK == jnp.arange(GLA_V)[None, :] // GLA_DV).astype(BF16)
    hblk = (jnp.arange(GLA_V)[:, None] // GLA_DV == jnp.arange(GLA_V)[None, :] // GLA_DV).astype(BF16)

    tmem = _tile(bp * n_mem, 512)
    mem_k_out, mem_v_out, mem_k, mem_v = _memkv(
        mem_prompt.reshape(bp * n_mem, D_MODEL), mw_k, mw_v, tmem)
    mem_k = mem_k.reshape(nl, bp, n_mem, D_MODEL)
    mem_v = mem_v.reshape(nl, bp, n_mem, D_MODEL)
    smem_k = cache_mem_k.reshape(nl, bs, n_mem, D_MODEL)
    smem_v = cache_mem_v.reshape(nl, bs, n_mem, D_MODEL)

    groups = {
        "p": dict(nb=bp, seq=sp, pos0=0, tm=_tile(bp * sp, 512)),
        "s": dict(nb=bs, seq=ss, pos0=past, tm=_tile(bs * ss, 512)),
    }
    for gname, gr in groups.items():
        seq, tm = gr["seq"], gr["tm"]
        pos = gr["pos0"] + jnp.arange(seq)
        tab = _rope_table(pos)
        if tm <= seq:
            gr["tab"], gr["tab_period"] = tab, seq // tm
        else:
            gr["tab"], gr["tab_period"] = jnp.tile(tab, (tm // seq, 1)), 1
        gr["tseq"] = _tile(seq, 512)
        gr["tgla"] = _tile(seq, 512)
        gr["ttok"] = _tile(gr["nb"] * seq, 1024)
        gr["tmem"] = _tile(seq, 1024)

    xs = {"p": x_prompt.reshape(bp * sp, D_MODEL), "s": x_sample.reshape(bs * ss, D_MODEL)}
    zero_hist = jnp.zeros((bp, POOL_PAD, POOL_WIDTH), F32)
    zero_state = jnp.zeros((bp, GLA_V, GLA_K), F32)
    outs = {n: {"pool": [], "ckv": [], "kr": [], "gla": []} for n in groups}

    for l in range(nl):
        for gname, gr in groups.items():
            nb, seq, tm = gr["nb"], gr["seq"], gr["tm"]
            x = xs[gname]
            x = _ffn(x, ffn_w[0][0], ffn_w[0][1], lng, lnb, l, 0, gr["ttok"])
            k_transposed = gname == "p"
            u_pool, ckv, misc, q, k, v, gla_in = _inproj(
                x, gr["tab"], w_a, w_q2, w_kt if k_transposed else w_k, w_v, qnorm, kvnorm,
                l, tm, gr["tab_period"], k_transposed)
            if gname == "p":
                hist = None
                hist16 = zero_hist
                s0t = zero_state
            else:
                hist = cache_pool[l]
                hist16 = jnp.concatenate(
                    [jnp.zeros((nb, POOL_PAD - POOL_HIST, POOL_WIDTH), F32), hist], 1)
                s0t = _state_to_tiles(state_gla[l])
            y_pool = _pool(u_pool, hist16, pool_bd, pscale, l, nb, seq, gr["tseq"], gr["pos0"])
            if gname == "p":
                y_mla = _mla_causal(q, k, v, nb, seq, gr["tseq"])
            else:
                y_mla = _mla_cached(q, ckv, misc, cache_mla_ckv, cache_mla_krope, w_abs, w_uvp,
                                    l, nb, seq, past)
            y_gla, s_fin = _gla(gla_in, misc, w_a2, b_a, norm_g, hsel, hblk, s0t, l, nb, seq, gr["tgla"])
            x = _merge(x, y_pool, y_mla, y_gla, w_gates, w_bp, w_bm, w_bg, w_o, lng, lnb, l,
                       gr["ttok"])
            if gname == "p":
                x = _memattn(x, mem_k, mem_v, l, mw_q, mw_o, lng, lnb, l, nb, seq, gr["tmem"])
            else:
                x = _memattn(x, smem_k, smem_v, l, mw_q, mw_o, lng, lnb, l, nb, seq, gr["tmem"])
            x = _ffn(x, ffn_w[1][0], ffn_w[1][1], lng, lnb, l, 3, gr["ttok"])
            xs[gname] = x

            u3 = u_pool.reshape(nb, seq, POOL_WIDTH)
            if hist is None:
                hist = jnp.zeros((nb, POOL_HIST, POOL_WIDTH), F32)
            new_pool = jnp.concatenate([hist, u3], 1)[:, -POOL_HIST:]
            o = outs[gname]
            o["pool"].append(new_pool)
            o["ckv"].append(ckv.reshape(nb, seq, MLA_KV_RANK))
            o["kr"].append(misc[:, :MLA_ROPE].reshape(nb, seq, MLA_ROPE))
            o["gla"].append(_tiles_to_state(s_fin))

    st = lambda n, key: jnp.stack(outs[n][key], 0)
    return (xs["p"].reshape(bp, sp, D_MODEL), xs["s"].reshape(bs, ss, D_MODEL),
            st("p", "pool"), st("s", "pool"),
            st("p", "ckv"), st("s", "ckv"),
            st("p", "kr"), st("s", "kr"),
            st("p", "gla"), st("s", "gla"),
            mem_k_out.reshape(nl, bp, n_mem, MEM_HEADS, MEM_HEAD_DIM),
            mem_v_out.reshape(nl, bp, n_mem, MEM_HEADS, MEM_HEAD_DIM))
```

```python
import functools
import math

import jax
import jax.numpy as jnp
from jax import lax
from jax.experimental import pallas as pl
from jax.experimental.pallas import tpu as pltpu

F32 = jnp.float32
BF16 = jnp.bfloat16

D_MODEL = 1024
DEPTH = 4
CHUNK = 64
EPS = 1e-5
ALPHA = (2 * DEPTH) ** 0.25

POOL_WINDOWS = (2, 4, 8, 16)
POOL_GROUP = D_MODEL // 16
POOL_WIDTH = len(POOL_WINDOWS) * POOL_GROUP
POOL_HIST = max(POOL_WINDOWS) - 1

MLA_HEADS = 8
MLA_Q_RANK = D_MODEL // 4
MLA_KV_RANK = D_MODEL // 4
MLA_NOPE = 64
MLA_ROPE = 32
MLA_V = 64
MLA_SCALE = (MLA_NOPE + MLA_ROPE) ** -0.5
ROPE_BASE = 10000.0
MLA_HEAD_PAD = 128
MLA_PAIR = 2
MLA_GROUP = 4
V_ONE_EVEN = MLA_V
V_ONE_ODD = MLA_HEAD_PAD
LOG2E = math.log2(math.e)

GLA_HEADS = 4
GLA_DK = 32
GLA_DV = 64
GLA_GATE_RANK = 16
GLA_GATE_TAU = 16.0
GLA_K = GLA_HEADS * GLA_DK
GLA_V = GLA_HEADS * GLA_DV

MEM_HEADS = 4
MEM_HEAD_DIM = D_MODEL // MEM_HEADS
D_FF = 2816

V7X_LANES = 128
V7X_SUBLANES = 8
V7X_VMEM_BYTES = 64 * 1024 * 1024
VMEM_LIMIT = V7X_VMEM_BYTES - 8 * 1024 * 1024

A_POOL = 0
A_QLAT = 256
A_CKV = 512
A_M1 = 768
A_M2 = 896
A_GLA = 1024
A_COLS = 1792
MISC_W = 128
GLA_IN_W = 2 * GLA_K + 2 * GLA_V


def _cparams(sem):
    return pltpu.CompilerParams(dimension_semantics=sem, vmem_limit_bytes=VMEM_LIMIT)


def _resident(shape, index_map):
    return pl.BlockSpec(shape, index_map, pipeline_mode=pl.Buffered(1))


def _layer_norm(y, g, b):
    mu = jnp.mean(y, axis=-1, keepdims=True)
    d = y - mu
    var = jnp.mean(d * d, axis=-1, keepdims=True)
    return d * lax.rsqrt(var + EPS) * g + b


def _rms_norm(y, g):
    return y * lax.rsqrt(jnp.mean(y * y, axis=-1, keepdims=True) + EPS) * g


def _dot(a, b):
    return jnp.dot(a, b, preferred_element_type=F32)


def _dot_nt(a, b):
    return lax.dot_general(a, b, (((1,), (1,)), ((), ())), preferred_element_type=F32)


def _silu(x):
    return x * jax.nn.sigmoid(x)


FFN_CHUNKS = ((0, 1024), (1024, 1024), (2048, 768))


def _ffn_kernel(x_ref, win_ref, wout_ref, g_ref, b_ref, o_ref, act_ref):
    x = x_ref[...]
    xb = x.astype(BF16)
    for c0, cw in FFN_CHUNKS:
        gate = _dot(xb, win_ref[:, c0:c0 + cw])
        up = _dot(xb, win_ref[:, D_FF + c0:D_FF + c0 + cw])
        act_ref[:, c0:c0 + cw] = (_silu(gate) * up).astype(BF16)
    y = _dot(act_ref[...], wout_ref[...])
    o_ref[...] = _layer_norm(ALPHA * x + 0.5 * y, g_ref[...], b_ref[...])


def _ffn(x, w_in, w_out, ln_g, ln_b, layer, which, tm):
    t = x.shape[0]
    return pl.pallas_call(
        _ffn_kernel,
        grid=(t // tm,),
        in_specs=[
            pl.BlockSpec((tm, D_MODEL), lambda i: (i, 0)),
            _resident((None, D_MODEL, 2 * D_FF), lambda i: (layer, 0, 0)),
            _resident((None, D_FF, D_MODEL), lambda i: (layer, 0, 0)),
            _resident((None, None, 1, D_MODEL), lambda i: (layer, which, 0, 0)),
            _resident((None, None, 1, D_MODEL), lambda i: (layer, which, 0, 0)),
        ],
        out_specs=pl.BlockSpec((tm, D_MODEL), lambda i: (i, 0)),
        out_shape=jax.ShapeDtypeStruct((t, D_MODEL), F32),
        scratch_shapes=[pltpu.VMEM((tm, D_FF), BF16)],
        compiler_params=_cparams(("parallel",)),
        name="ffn",
    )(x, w_in, w_out, ln_g, ln_b)


def _v_with_ones(vv):
    lane = lax.broadcasted_iota(jnp.int32, vv.shape, 1) % (MLA_PAIR * MLA_HEAD_PAD)
    return jnp.where((lane == V_ONE_EVEN) | (lane == V_ONE_ODD), 1.0, vv).astype(BF16)


def _inproj_kernel(x_ref, tab_ref, wa_ref, wq_ref, wk_ref, wv_ref, qn_ref, kvn_ref,
                   pool_ref, ckv_ref, misc_ref, q_ref, k_ref, v_ref, gla_ref, *, k_transposed):
    xb = x_ref[...].astype(BF16)
    h = _dot(xb, wa_ref[:, 0:A_GLA])
    gla_ref[...] = _dot(xb, wa_ref[:, A_GLA:A_COLS])
    pool_ref[...] = h[:, A_POOL:A_POOL + POOL_WIDTH]

    cm = tab_ref[:, 0:128]
    sm = tab_ref[:, 128:256]
    misc = h[:, A_M1:A_M1 + MISC_W] * cm + h[:, A_M2:A_M2 + MISC_W] * sm
    misc_ref[...] = misc

    ckv = _rms_norm(h[:, A_CKV:A_CKV + MLA_KV_RANK], kvn_ref[...])
    ckv_ref[...] = ckv
    ckvb = ckv.astype(BF16)
    lat = jnp.concatenate([ckvb, misc.astype(BF16)], axis=-1)
    if k_transposed:
        kt = _dot_nt(wk_ref[...], lat).astype(BF16)
        k_tile = k_ref.shape[-1]
        for c in range(k_ref.shape[0]):
            k_ref[c] = kt[:, c * k_tile:(c + 1) * k_tile]
    else:
        k_ref[...] = _dot(lat, wk_ref[...]).astype(BF16)
    v_ref[...] = _v_with_ones(_dot(ckvb, wv_ref[...]))

    qn = _rms_norm(h[:, A_QLAT:A_QLAT + MLA_Q_RANK], qn_ref[...]).astype(BF16)
    q2 = _dot(qn, wq_ref[...])
    kw = MLA_HEADS * MLA_HEAD_PAD
    cq = tab_ref[:, 256:384]
    sq = tab_ref[:, 384:512]
    for hd in range(MLA_HEADS):
        a = hd * MLA_HEAD_PAD
        q_ref[:, a:a + MLA_HEAD_PAD] = (
            q2[:, a:a + MLA_HEAD_PAD] * cq + q2[:, kw + a:kw + a + MLA_HEAD_PAD] * sq).astype(BF16)


def _inproj(x, tab, w_a, w_q2, w_k, w_v, qnorm, kvnorm, layer, tm, tab_period, k_tile):
    t = x.shape[0]
    kw = MLA_HEADS * MLA_HEAD_PAD
    lat_w = MLA_KV_RANK + MISC_W
    row = lambda i: (i, 0)
    k_transposed = k_tile is not None
    if k_transposed:
        k_shape = jax.ShapeDtypeStruct((t // k_tile, kw, k_tile), BF16)
        k_spec = pl.BlockSpec((tm // k_tile, kw, k_tile), lambda i: (i, 0, 0))
        wk_spec = _resident((None, kw, lat_w), lambda i: (layer, 0, 0))
    else:
        k_shape = jax.ShapeDtypeStruct((t, kw), BF16)
        k_spec = pl.BlockSpec((tm, kw), row)
        wk_spec = _resident((None, lat_w, kw), lambda i: (layer, 0, 0))
    out_shapes = (
        jax.ShapeDtypeStruct((t, POOL_WIDTH), F32),
        jax.ShapeDtypeStruct((t, MLA_KV_RANK), F32),
        jax.ShapeDtypeStruct((t, MISC_W), F32),
        jax.ShapeDtypeStruct((t, kw), BF16),
        k_shape,
        jax.ShapeDtypeStruct((t, kw), BF16),
        jax.ShapeDtypeStruct((t, GLA_IN_W), F32),
    )
    out_specs = [pl.BlockSpec((tm, s.shape[-1]), row) for s in out_shapes]
    out_specs[4] = k_spec
    return pl.pallas_call(
        functools.partial(_inproj_kernel, k_transposed=k_transposed),
        grid=(t // tm,),
        in_specs=[
            pl.BlockSpec((tm, D_MODEL), row),
            pl.BlockSpec((tm, 512), lambda i: (i % tab_period, 0)),
            _resident((None, D_MODEL, A_COLS), lambda i: (layer, 0, 0)),
            _resident((None, MLA_Q_RANK, 2 * kw), lambda i: (layer, 0, 0)),
            wk_spec,
            _resident((None, MLA_KV_RANK, kw), lambda i: (layer, 0, 0)),
            _resident((None, 1, MLA_Q_RANK), lambda i: (layer, 0, 0)),
            _resident((None, 1, MLA_KV_RANK), lambda i: (layer, 0, 0)),
        ],
        out_specs=out_specs,
        out_shape=out_shapes,
        compiler_params=_cparams(("parallel",)),
        name="inproj",
    )(x, tab, w_a, w_q2, w_k, w_v, qnorm, kvnorm)


POOL_PAD = 16


def _pool_kernel(u_ref, hist_ref, w_ref, scale_ref, y_ref, ext_ref, *, tp, pos0):
    j = pl.program_id(1)

    @pl.when(j == 0)
    def _():
        ext_ref[0:POOL_PAD, :] = hist_ref[...]

    tok = u_ref[...]
    ext_ref[POOL_PAD:POOL_PAD + tp, :] = tok
    acc = ext_ref[...]
    sums = {}
    w = 1
    while w < max(POOL_WINDOWS):
        acc = acc + pltpu.roll(acc, w, 0)
        w *= 2
        if w in POOL_WINDOWS:
            sums[w] = acc[POOL_PAD:POOL_PAD + tp, :]
    lane = lax.broadcasted_iota(jnp.int32, (tp, POOL_WIDTH), 1)
    pos = pos0 + j * tp + lax.broadcasted_iota(jnp.int32, (tp, POOL_WIDTH), 0)
    total = None
    width = None
    for gi, w in enumerate(POOL_WINDOWS):
        in_group = lane >= gi * POOL_GROUP
        total = sums[w] if total is None else jnp.where(in_group, sums[w], total)
        width = jnp.full_like(pos, w) if width is None else jnp.where(in_group, w, width)
    cnt = jnp.minimum(pos + 1, width).astype(F32)
    pooled = total / cnt - tok
    y = _dot(pooled.astype(BF16), w_ref[...]) * scale_ref[...]
    y_ref[...] = y.astype(BF16)
    ext_ref[0:POOL_PAD, :] = ext_ref[tp:tp + POOL_PAD, :]


def _pool(u, hist16, w_bd, scale, layer, nb, seq, tp, pos0):
    nt = seq // tp
    return pl.pallas_call(
        functools.partial(_pool_kernel, tp=tp, pos0=pos0),
        grid=(nb, nt),
        in_specs=[
            pl.BlockSpec((tp, POOL_WIDTH), lambda b, j: (b * nt + j, 0)),
            pl.BlockSpec((None, POOL_PAD, POOL_WIDTH), lambda b, j: (b, 0, 0)),
            _resident((None, POOL_WIDTH, POOL_WIDTH), lambda b, j: (layer, 0, 0)),
            _resident((None, 1, POOL_WIDTH), lambda b, j: (layer, 0, 0)),
        ],
        out_specs=pl.BlockSpec((tp, POOL_WIDTH), lambda b, j: (b * nt + j, 0)),
        out_shape=jax.ShapeDtypeStruct((nb * seq, POOL_WIDTH), BF16),
        scratch_shapes=[pltpu.VMEM((POOL_PAD + tp, POOL_WIDTH), F32)],
        compiler_params=_cparams(("parallel", "arbitrary")),
        name="pool",
    )(u, hist16, w_bd, scale)


def _pair_output(acc0, acc1):
    lane = lax.broadcasted_iota(jnp.int32, acc0.shape, 1)
    o0 = acc0 / acc0[:, V_ONE_EVEN:V_ONE_EVEN + 1]
    o1 = acc1 / acc1[:, 0:1]
    return jnp.where(lane < MLA_V, o0, o1).astype(BF16)


def _mla_causal_kernel(q_ref, kt_ref, v_ref, o_ref, m_ref, acc_ref, sa_ref, sb_ref, *, tq):
    i = pl.program_id(2)
    m_ref[...] = jnp.full(m_ref.shape, -jnp.inf, F32)
    acc_ref[...] = jnp.zeros(acc_ref.shape, F32)
    reps = tq // V7X_LANES

    def scores(kt, dst_ref):
        for h in range(MLA_GROUP):
            a = h * MLA_HEAD_PAD
            dst_ref[h] = _dot(q_ref[:, a:a + MLA_HEAD_PAD], kt_ref[kt, a:a + MLA_HEAD_PAD, :])

    def consume(src_ref, kt, masked):
        r0 = pl.multiple_of(kt * tq, tq)
        for h in range(MLA_GROUP):
            a = h * MLA_HEAD_PAD
            s = src_ref[h]
            if masked:
                qc = lax.broadcasted_iota(jnp.int32, (tq, tq), 0) // CHUNK
                kc = lax.broadcasted_iota(jnp.int32, (tq, tq), 1) // CHUNK
                s = jnp.where(kc <= qc, s, -jnp.inf)
            m_prev = m_ref[h]
            m_new = jnp.maximum(m_prev, jnp.max(s, axis=-1, keepdims=True))
            p = jnp.exp2(s - jnp.tile(m_new, (1, reps)))
            corr = jnp.exp2(m_prev - m_new)
            acc_ref[h] = corr * acc_ref[h] + _dot(p.astype(BF16),
                                                  v_ref[pl.ds(r0, tq), a:a + MLA_HEAD_PAD])
            m_ref[h] = m_new

    scores(0, sa_ref)

    def body(j, carry):
        scores(2 * j + 1, sb_ref)
        consume(sa_ref, 2 * j, False)
        scores(2 * j + 2, sa_ref)
        consume(sb_ref, 2 * j + 1, False)
        return carry

    lax.fori_loop(0, i // 2, body, 0)

    @pl.when(i % 2 == 0)
    def _():
        consume(sa_ref, i, True)

    @pl.when(i % 2 == 1)
    def _():
        scores(i, sb_ref)
        consume(sa_ref, i - 1, False)
        consume(sb_ref, i, True)

    ow = MLA_PAIR * MLA_V
    for pr in range(MLA_GROUP // MLA_PAIR):
        o_ref[:, pr * ow:(pr + 1) * ow] = _pair_output(acc_ref[MLA_PAIR * pr], acc_ref[MLA_PAIR * pr + 1])


def _mla_causal(q, kt, v, nb, seq, tq):
    nq = seq // tq
    ngrp = MLA_HEADS // MLA_GROUP
    pw = MLA_GROUP * MLA_HEAD_PAD
    return pl.pallas_call(
        functools.partial(_mla_causal_kernel, tq=tq),
        grid=(nb, ngrp, nq),
        in_specs=[
            pl.BlockSpec((tq, pw), lambda b, p, i: (b * nq + i, p)),
            pl.BlockSpec((nq, pw, tq), lambda b, p, i: (b, p, 0)),
            pl.BlockSpec((seq, pw), lambda b, p, i: (b, p)),
        ],
        out_specs=pl.BlockSpec((tq, MLA_GROUP * MLA_V), lambda b, p, i: (b * nq + i, p)),
        out_shape=jax.ShapeDtypeStruct((nb * seq, MLA_HEADS * MLA_V), BF16),
        scratch_shapes=[pltpu.VMEM((MLA_GROUP, tq, V7X_LANES), F32),
                        pltpu.VMEM((MLA_GROUP, tq, MLA_HEAD_PAD), F32),
                        pltpu.VMEM((MLA_GROUP, tq, tq), F32),
                        pltpu.VMEM((MLA_GROUP, tq, tq), F32)],
        compiler_params=_cparams(("parallel", "parallel", "arbitrary")),
        name="mla_causal",
    )(q, kt, v)


def _mla_cached_kernel(q_ref, cp_ref, rp_ref, cn_ref, mn_ref, wabs_ref, wuv_ref, o_ref):
    qa = jnp.concatenate(
        [_dot(q_ref[:, h * MLA_HEAD_PAD:(h + 1) * MLA_HEAD_PAD], wabs_ref[h]) for h in range(MLA_HEADS)],
        axis=0).astype(BF16)
    qa_lat = qa[:, 0:MLA_KV_RANK]
    qa_misc = qa[:, MLA_KV_RANK:]
    cp = cp_ref[...].astype(BF16)
    cn = cn_ref[...].astype(BF16)
    s1 = _dot_nt(qa_lat, cp) + _dot_nt(qa_misc[:, 0:MLA_ROPE], rp_ref[...].astype(BF16))
    s2 = _dot_nt(qa_lat, cn) + _dot_nt(qa_misc, mn_ref[...].astype(BF16))
    m = jnp.maximum(jnp.max(s1, axis=-1, keepdims=True), jnp.max(s2, axis=-1, keepdims=True))
    p1 = jnp.exp2(s1 - m)
    p2 = jnp.exp2(s2 - m)
    l = jnp.sum(p1, axis=-1, keepdims=True) + jnp.sum(p2, axis=-1, keepdims=True)
    lat = ((_dot(p1.astype(BF16), cp) + _dot(p2.astype(BF16), cn)) / l).astype(BF16)
    seq = q_ref.shape[0]
    y = _dot(lat[0:seq], wuv_ref[0])
    for h in range(1, MLA_HEADS):
        y = y + _dot(lat[h * seq:(h + 1) * seq], wuv_ref[h])
    o_ref[...] = y.astype(BF16)


def _mla_cached(q, ckv_new, misc_new, ckv_past, kr_past, w_abs, w_uvp, layer, nb, seq, past):
    kw = MLA_HEADS * MLA_HEAD_PAD
    vw = MLA_HEADS * MLA_V
    lat_w = MLA_KV_RANK + MISC_W
    return pl.pallas_call(
        _mla_cached_kernel,
        grid=(nb,),
        in_specs=[
            pl.BlockSpec((seq, kw), lambda b: (b, 0)),
            pl.BlockSpec((None, None, past, MLA_KV_RANK), lambda b: (layer, b, 0, 0)),
            pl.BlockSpec((None, None, past, MLA_ROPE), lambda b: (layer, b, 0, 0)),
            pl.BlockSpec((seq, MLA_KV_RANK), lambda b: (b, 0)),
            pl.BlockSpec((seq, MISC_W), lambda b: (b, 0)),
            _resident((None, MLA_HEADS, MLA_HEAD_PAD, lat_w), lambda b: (layer, 0, 0, 0)),
            _resident((None, MLA_HEADS, MLA_KV_RANK, vw), lambda b: (layer, 0, 0, 0)),
        ],
        out_specs=pl.BlockSpec((seq, vw), lambda b: (b, 0)),
        out_shape=jax.ShapeDtypeStruct((nb * seq, vw), BF16),
        compiler_params=_cparams(("parallel",)),
        name="mla_cached",
    )(q, ckv_past, kr_past, ckv_new, misc_new, w_abs, w_uvp)


def _gla_kernel(gq_ref, gk_ref, gv_ref, gr_ref, misc_ref, wa2_ref, ba_ref, ng_ref, hsel_ref,
                hblk_ref, s0_ref, y_ref, sfin_ref, st_ref, sec_ref, fst_ref, same_ref, kbd_ref,
                vbd_ref, sbd_ref, o_ref, *, tt):
    j = pl.program_id(1)
    c = min(CHUNK, tt)
    hc = GLA_HEADS * c
    levels = c.bit_length() - 1

    @pl.when(j == 0)
    def _():
        st_ref[...] = s0_ref[...]
        row = lax.broadcasted_iota(jnp.int32, (c, GLA_K), 0)
        trow = lax.broadcasted_iota(jnp.int32, (c, hc), 0)
        tcol = lax.broadcasted_iota(jnp.int32, (c, hc), 1) % c
        for lv in range(levels):
            m = 1 << lv
            sec_ref[lv] = ((row & m) != 0).astype(F32)
            fst_ref[lv] = ((row & m) == 0).astype(F32)
            same_ref[lv] = ((trow // (2 * m)) == (tcol // (2 * m))).astype(F32)
        kbd_ref[...] = (lax.broadcasted_iota(jnp.int32, (hc, GLA_K), 0) // c
                        == lax.broadcasted_iota(jnp.int32, (hc, GLA_K), 1) // GLA_DK).astype(BF16)
        vbd_ref[...] = (lax.broadcasted_iota(jnp.int32, (hc, GLA_V), 0) // c
                        == lax.broadcasted_iota(jnp.int32, (hc, GLA_V), 1) // GLA_DV).astype(BF16)
        sbd_ref[...] = (lax.broadcasted_iota(jnp.int32, (GLA_V, GLA_K), 0) // GLA_DV
                        == lax.broadcasted_iota(jnp.int32, (GLA_V, GLA_K), 1) // GLA_DK).astype(F32)

    z = _dot(misc_ref[...].astype(BF16), wa2_ref[...]) + ba_ref[...]
    la_all = jax.nn.log_sigmoid(z) * (LOG2E / GLA_GATE_TAU)

    st = st_ref[...]
    for r in range(tt // c):
        rs = slice(r * c, (r + 1) * c)
        q = gq_ref[rs, :] * (GLA_DK ** -0.5)
        k = gk_ref[rs, :]
        v = gv_ref[rs, :]
        la = la_all[rs, :]

        scores = jnp.zeros((c, hc), F32)
        p = la
        g = la
        for lv in range(levels):
            m = 1 << lv
            if m < V7X_SUBLANES:
                sec = sec_ref[lv]
                fst = fst_ref[lv]
                qe = (q * jnp.exp2(p) * sec).astype(BF16)
                ke = (k * jnp.exp2(g - p) * fst).astype(BF16)
                ke_bd = jnp.tile(ke, (GLA_HEADS, 1)) * kbd_ref[...]
                scores = scores + _dot_nt(qe, ke_bd) * same_ref[lv]
                up = sec * pltpu.roll(g, m, 0)
                p = p + up
                g = g + up + fst * pltpu.roll(g, c - m, 0)
            else:
                nb2 = c // (2 * m)

                def halves(a):
                    a4 = a.reshape(nb2, 2, m, a.shape[-1])
                    return a4[:, 0], a4[:, 1]

                def join(lo, hi):
                    return jnp.stack([lo, hi], axis=1).reshape(c, lo.shape[-1])

                p_lo, p_hi = halves(p)
                g_lo, g_hi = halves(g)
                qe_hi = (halves(q)[1] * jnp.exp2(p_hi)).reshape(c // 2, GLA_K).astype(BF16)
                ke_lo = halves(k)[0] * jnp.exp2(g_lo - p_lo)
                ke = join(ke_lo, jnp.zeros_like(ke_lo)).astype(BF16)
                ke_bd = jnp.tile(ke, (GLA_HEADS, 1)) * kbd_ref[...]
                a_hi = _dot_nt(qe_hi, ke_bd) * halves(same_ref[lv])[1].reshape(c // 2, hc)
                s_lo, s_hi = halves(scores)
                scores = join(s_lo, s_hi + a_hi.reshape(nb2, m, hc))
                tot = g_lo + g_hi
                p = join(p_lo, p_hi + g_lo)
                g = join(tot, tot)

        o = _dot_nt((q * jnp.exp2(p)).astype(BF16), st.astype(BF16))
        v_bd = jnp.tile(v.astype(BF16), (GLA_HEADS, 1)) * vbd_ref[...]
        o_ref[rs, :] = o + _dot(scores.astype(BF16), v_bd)

        kdec = (k * jnp.exp2(g - p)).astype(BF16)
        upd = _dot(jnp.transpose(v).astype(BF16), kdec)
        st = st * jnp.exp2(g[0:1, :]) + upd * sbd_ref[...]

    st_ref[...] = st
    sfin_ref[...] = st

    qk = (gq_ref[...] * (GLA_DK ** -0.5) * gk_ref[...]).astype(BF16)
    o = o_ref[...] + _dot(qk, hsel_ref[...]) * gv_ref[...]
    o2 = o * o
    o2_hi = o2.astype(BF16)
    o2_lo = (o2 - o2_hi.astype(F32)).astype(BF16)
    ms = (_dot(o2_hi, hblk_ref[...]) + _dot(o2_lo, hblk_ref[...])) * (1.0 / GLA_DV)
    y = o * lax.rsqrt(ms + EPS) * ng_ref[...] * _silu(gr_ref[...])
    y_ref[...] = y.astype(BF16)


def _gla(gla_in, misc, w_a2, b_a, norm_g, hsel, hblk, s0t, layer, nb, seq, tt):
    nt = seq // tt
    row = lambda b, j: (b * nt + j, 0)
    c = min(CHUNK, tt)
    hc = GLA_HEADS * c
    levels = c.bit_length() - 1
    return pl.pallas_call(
        functools.partial(_gla_kernel, tt=tt),
        grid=(nb, nt),
        in_specs=[
            pl.BlockSpec((tt, GLA_K), lambda b, j: (b * nt + j, 0)),
            pl.BlockSpec((tt, GLA_K), lambda b, j: (b * nt + j, 1)),
            pl.BlockSpec((tt, GLA_V), lambda b, j: (b * nt + j, 1)),
            pl.BlockSpec((tt, GLA_V), lambda b, j: (b * nt + j, 2)),
            pl.BlockSpec((tt, MISC_W), row),
            _resident((None, MISC_W, GLA_K), lambda b, j: (layer, 0, 0)),
            _resident((None, 1, GLA_K), lambda b, j: (layer, 0, 0)),
            _resident((None, 1, GLA_V), lambda b, j: (layer, 0, 0)),
            _resident((GLA_K, GLA_V), lambda b, j: (0, 0)),
            _resident((GLA_V, GLA_V), lambda b, j: (0, 0)),
            pl.BlockSpec((None, GLA_V, GLA_K), lambda b, j: (b, 0, 0)),
        ],
        out_specs=[pl.BlockSpec((tt, GLA_V), row),
                   pl.BlockSpec((None, GLA_V, GLA_K), lambda b, j: (b, 0, 0))],
        out_shape=(jax.ShapeDtypeStruct((nb * seq, GLA_V), BF16),
                   jax.ShapeDtypeStruct((nb, GLA_V, GLA_K), F32)),
        scratch_shapes=[pltpu.VMEM((GLA_V, GLA_K), F32),
                        pltpu.VMEM((levels, c, GLA_K), F32),
                        pltpu.VMEM((levels, c, GLA_K), F32),
                        pltpu.VMEM((levels, c, hc), F32),
                        pltpu.VMEM((hc, GLA_K), BF16),
                        pltpu.VMEM((hc, GLA_V), BF16),
                        pltpu.VMEM((GLA_V, GLA_K), F32),
                        pltpu.VMEM((tt, GLA_V), F32)],
        compiler_params=_cparams(("parallel", "arbitrary")),
        name="gla",
    )(gla_in, gla_in, gla_in, gla_in, misc, w_a2, b_a, norm_g, hsel, hblk, s0t)


def _merge_kernel(x_ref, yp_ref, ym_ref, yg_ref, wg_ref, wbp_ref, wbm_ref, wbg_ref, wo_ref,
                  g_ref, b_ref, o_ref):
    x = x_ref[...]
    xb = x.astype(BF16)
    m = jax.nn.sigmoid(_dot(xb, wg_ref[:, 0:D_MODEL])) * _dot(yp_ref[...], wbp_ref[...])
    m = m + jax.nn.sigmoid(_dot(xb, wg_ref[:, D_MODEL:2 * D_MODEL])) * _dot(ym_ref[...], wbm_ref[...])
    m = m + jax.nn.sigmoid(_dot(xb, wg_ref[:, 2 * D_MODEL:3 * D_MODEL])) * _dot(yg_ref[...], wbg_ref[...])
    y = _dot(m.astype(BF16), wo_ref[...])
    o_ref[...] = _layer_norm(ALPHA * x + y, g_ref[...], b_ref[...])


def _merge(x, y_pool, y_mla, y_gla, w_gates, w_bp, w_bm, w_bg, w_out, ln_g, ln_b, layer, tm):
    t = x.shape[0]
    row = lambda i: (i, 0)
    lay = lambda i: (layer, 0, 0)
    return pl.pallas_call(
        _merge_kernel,
        grid=(t // tm,),
        in_specs=[
            pl.BlockSpec((tm, D_MODEL), row),
            pl.BlockSpec((tm, POOL_WIDTH), row),
            pl.BlockSpec((tm, MLA_HEADS * MLA_V), row),
            pl.BlockSpec((tm, GLA_V), row),
            _resident((None, D_MODEL, 3 * D_MODEL), lay),
            _resident((None, POOL_WIDTH, D_MODEL), lay),
            _resident((None, MLA_HEADS * MLA_V, D_MODEL), lay),
            _resident((None, GLA_V, D_MODEL), lay),
            _resident((None, D_MODEL, D_MODEL), lay),
            _resident((None, None, 1, D_MODEL), lambda i: (layer, 1, 0, 0)),
            _resident((None, None, 1, D_MODEL), lambda i: (layer, 1, 0, 0)),
        ],
        out_specs=pl.BlockSpec((tm, D_MODEL), row),
        out_shape=jax.ShapeDtypeStruct((t, D_MODEL), F32),
        compiler_params=_cparams(("parallel",)),
        name="merge",
    )(x, y_pool, y_mla, y_gla, w_gates, w_bp, w_bm, w_bg, w_out, ln_g, ln_b)


def _memkv_kernel(m_ref, wk_ref, wv_ref, k_ref, v_ref, kb_ref, vb_ref):
    mb = m_ref[...].astype(BF16)
    k = _dot(mb, wk_ref[...])
    v = _dot(mb, wv_ref[...])
    kb_ref[...] = k.astype(BF16)
    vb_ref[...] = v.astype(BF16)
    for h in range(MEM_HEADS):
        a = h * MEM_HEAD_DIM
        k_ref[:, h, :] = k[:, a:a + MEM_HEAD_DIM]
        v_ref[:, h, :] = v[:, a:a + MEM_HEAD_DIM]


def _memkv(mem, w_k, w_v, tm):
    rows = mem.shape[0]
    nl = w_k.shape[0]
    split = pl.BlockSpec((None, tm, MEM_HEADS, MEM_HEAD_DIM), lambda l, i: (l, i, 0, 0))
    flat = pl.BlockSpec((None, tm, D_MODEL), lambda l, i: (l, i, 0))
    return pl.pallas_call(
        _memkv_kernel,
        grid=(nl, rows // tm),
        in_specs=[
            pl.BlockSpec((tm, D_MODEL), lambda l, i: (i, 0)),
            pl.BlockSpec((None, D_MODEL, D_MODEL), lambda l, i: (l, 0, 0)),
            pl.BlockSpec((None, D_MODEL, D_MODEL), lambda l, i: (l, 0, 0)),
        ],
        out_specs=[split, split, flat, flat],
        out_shape=(jax.ShapeDtypeStruct((nl, rows, MEM_HEADS, MEM_HEAD_DIM), F32),
                   jax.ShapeDtypeStruct((nl, rows, MEM_HEADS, MEM_HEAD_DIM), F32),
                   jax.ShapeDtypeStruct((nl, rows, D_MODEL), BF16),
                   jax.ShapeDtypeStruct((nl, rows, D_MODEL), BF16)),
        compiler_params=_cparams(("parallel", "parallel")),
        name="memkv",
    )(mem, w_k, w_v)


def _memattn_kernel(x_ref, k_ref, v_ref, wq_ref, wo_ref, g_ref, b_ref, o_ref):
    x = x_ref[...]
    q = (_dot(x.astype(BF16), wq_ref[...]) * (MEM_HEAD_DIM ** -0.5 * LOG2E)).astype(BF16)
    kb = k_ref[...].astype(BF16)
    vb = v_ref[...].astype(BF16)
    outs = []
    for h in range(MEM_HEADS):
        a = h * MEM_HEAD_DIM
        s = _dot_nt(q[:, a:a + MEM_HEAD_DIM], kb[:, a:a + MEM_HEAD_DIM])
        p = jnp.exp2(s - jnp.max(s, axis=-1, keepdims=True))
        inv_l = 1.0 / jnp.sum(p, axis=-1, keepdims=True)
        outs.append(_dot(p.astype(BF16), vb[:, a:a + MEM_HEAD_DIM]) * inv_l)
    o = jnp.concatenate(outs, axis=-1).astype(BF16)
    y = _dot(o, wo_ref[...])
    o_ref[...] = _layer_norm(ALPHA * x + y, g_ref[...], b_ref[...])


def _memattn(x, mem_k, mem_v, kv_layer, w_q, w_o, ln_g, ln_b, layer, nb, seq, tm):
    nt = seq // tm
    n_mem = mem_k.shape[2]
    lay = lambda b, i: (layer, 0, 0)
    return pl.pallas_call(
        _memattn_kernel,
        grid=(nb, nt),
        in_specs=[
            pl.BlockSpec((tm, D_MODEL), lambda b, i: (b * nt + i, 0)),
            pl.BlockSpec((None, None, n_mem, D_MODEL), lambda b, i: (kv_layer, b, 0, 0)),
            pl.BlockSpec((None, None, n_mem, D_MODEL), lambda b, i: (kv_layer, b, 0, 0)),
            _resident((None, D_MODEL, D_MODEL), lay),
            _resident((None, D_MODEL, D_MODEL), lay),
            _resident((None, None, 1, D_MODEL), lambda b, i: (layer, 2, 0, 0)),
            _resident((None, None, 1, D_MODEL), lambda b, i: (layer, 2, 0, 0)),
        ],
        out_specs=pl.BlockSpec((tm, D_MODEL), lambda b, i: (b * nt + i, 0)),
        out_shape=jax.ShapeDtypeStruct((nb * seq, D_MODEL), F32),
        compiler_params=_cparams(("parallel", "parallel")),
        name="memattn",
    )(x, mem_k, mem_v, w_q, w_o, ln_g, ln_b)


def _pack_weights(w_in, pool_w, mla_w_uq, mla_w_uk, mla_w_uv, gla_w_a2):
    nl = w_in.shape[0]
    splits = (POOL_WIDTH, MLA_Q_RANK, MLA_KV_RANK, MLA_ROPE, GLA_K, GLA_K, GLA_V,
              GLA_GATE_RANK, GLA_V, 3 * D_MODEL)
    offs = [0]
    for s in splits:
        offs.append(offs[-1] + s)
    col = lambda n: w_in[:, :, offs[n]:offs[n + 1]]
    half = MLA_ROPE // 2
    kr = col(3)
    kr_sw = jnp.concatenate([kr[..., half:], kr[..., :half]], -1)
    zeros = lambda n: jnp.zeros((nl, D_MODEL, n), w_in.dtype)
    w_a = jnp.concatenate(
        [col(0), col(1), col(2),
         kr, col(7), zeros(MISC_W - MLA_ROPE - GLA_GATE_RANK),
         kr_sw, zeros(MISC_W - MLA_ROPE),
         col(4), col(5), col(6), col(8)], -1).astype(BF16)
    w_gates = col(9).astype(BF16)

    uq = mla_w_uq.reshape(nl, MLA_Q_RANK, MLA_HEADS, MLA_NOPE + MLA_ROPE)
    uq_n, uq_r = uq[..., :MLA_NOPE], uq[..., MLA_NOPE:]
    uq_rs = jnp.concatenate([uq_r[..., half:], uq_r[..., :half]], -1)
    padq = jnp.zeros((nl, MLA_Q_RANK, MLA_HEADS, MLA_HEAD_PAD - MLA_NOPE - MLA_ROPE), uq.dtype)
    q_main = jnp.concatenate([uq_n, uq_r, padq], -1)
    q_swap = jnp.concatenate([jnp.zeros_like(uq_n), uq_rs, padq], -1)
    kw = MLA_HEADS * MLA_HEAD_PAD
    w_q2 = jnp.concatenate([q_main.reshape(nl, MLA_Q_RANK, kw),
                            q_swap.reshape(nl, MLA_Q_RANK, kw)], -1).astype(BF16)

    padk = jnp.zeros((nl, MLA_KV_RANK, MLA_HEADS, MLA_HEAD_PAD - MLA_NOPE), mla_w_uk.dtype)
    k_lat = jnp.concatenate([mla_w_uk, padk], -1).reshape(nl, MLA_KV_RANK, kw)
    place = jnp.zeros((MISC_W, MLA_HEADS, MLA_HEAD_PAD), F32)
    idx = jnp.arange(MLA_ROPE)
    place = place.at[idx, :, MLA_NOPE + idx].set(1.0).reshape(MISC_W, kw)
    k_misc = jnp.broadcast_to(place, (nl, MISC_W, kw))
    w_k = jnp.concatenate([k_lat, k_misc], 1).astype(BF16)
    w_kt = jnp.swapaxes(w_k, 1, 2)
    uv = mla_w_uv.reshape(nl, MLA_KV_RANK, MLA_HEADS // MLA_PAIR, MLA_PAIR, MLA_V)
    padv = jnp.zeros_like(uv[:, :, :, 0])
    w_v = jnp.stack([jnp.concatenate([uv[:, :, :, 0], padv], -1),
                     jnp.concatenate([padv, uv[:, :, :, 1]], -1)], 3)
    w_v = w_v.reshape(nl, MLA_KV_RANK, kw).astype(BF16)

    uk_t = jnp.transpose(mla_w_uk, (0, 2, 3, 1))
    a_lat = jnp.concatenate(
        [uk_t, jnp.zeros((nl, MLA_HEADS, MLA_HEAD_PAD - MLA_NOPE, MLA_KV_RANK), uk_t.dtype)], 2)
    a_misc = jnp.zeros((MLA_HEAD_PAD, MISC_W), F32).at[MLA_NOPE + idx, idx].set(1.0)
    a_misc = jnp.broadcast_to(a_misc, (nl, MLA_HEADS, MLA_HEAD_PAD, MISC_W))
    w_abs = jnp.concatenate([a_lat, a_misc], -1).astype(BF16)
    eye_h = jnp.eye(MLA_HEADS, dtype=mla_w_uv.dtype)
    uv_h = jnp.transpose(mla_w_uv, (0, 2, 1, 3))
    w_uvp = (uv_h[:, :, :, None, :] * eye_h[None, :, None, :, None]).reshape(
        nl, MLA_HEADS, MLA_KV_RANK, MLA_HEADS * MLA_V).astype(BF16)

    ng = len(POOL_WINDOWS)
    eye = jnp.eye(ng, dtype=pool_w.dtype)
    pool_bd = (pool_w[:, :, :, None, :] * eye[None, :, None, :, None]).reshape(
        nl, POOL_WIDTH, POOL_WIDTH).astype(BF16)

    w_a2 = jnp.zeros((nl, MISC_W, GLA_K), gla_w_a2.dtype)
    w_a2 = w_a2.at[:, MLA_ROPE:MLA_ROPE + GLA_GATE_RANK, :].set(gla_w_a2).astype(BF16)
    return w_a, w_gates, w_q2, w_k, w_kt, w_v, w_abs, w_uvp, pool_bd, w_a2


def _rope_table(pos):
    half = MLA_ROPE // 2
    inv = ROPE_BASE ** (-jnp.arange(half, dtype=F32) / half)
    ang = pos.astype(F32)[:, None] * inv
    cos, sin = jnp.cos(ang), jnp.sin(ang)
    n = pos.shape[0]
    one = jnp.ones((n, GLA_GATE_RANK), F32)
    cm = jnp.concatenate([cos, cos, one, jnp.zeros((n, MISC_W - MLA_ROPE - GLA_GATE_RANK), F32)], -1)
    sm = jnp.concatenate([-sin, sin, jnp.zeros((n, MISC_W - MLA_ROPE), F32)], -1)
    padq = jnp.zeros((n, MLA_HEAD_PAD - MLA_NOPE - MLA_ROPE), F32)
    qs = MLA_SCALE * LOG2E
    cq = jnp.concatenate([jnp.ones((n, MLA_NOPE), F32), cos, cos, padq], -1) * qs
    sq = jnp.concatenate([jnp.zeros((n, MLA_NOPE), F32), -sin, sin, padq], -1) * qs
    return jnp.concatenate([cm, sm, cq, sq], -1)


def _state_to_tiles(s):
    eye = jnp.eye(GLA_HEADS, dtype=s.dtype)
    st = jnp.swapaxes(s, 2, 3)
    bd = st[:, :, :, None, :] * eye[None, :, None, :, None]
    return bd.reshape(s.shape[0], GLA_V, GLA_K)


def _tiles_to_state(t):
    b = t.shape[0]
    t5 = t.reshape(b, GLA_HEADS, GLA_DV, GLA_HEADS, GLA_DK)
    idx = jnp.arange(GLA_HEADS)
    diag = t5[:, idx, :, idx, :]
    return jnp.transpose(diag, (1, 0, 3, 2))


def _tile(n, pref):
    return pref if n % pref == 0 else n


def kernel(x_prompt, x_sample, mem_prompt, cache_pool, cache_mla_ckv, cache_mla_krope, state_gla,
           cache_mem_k, cache_mem_v, ln_g, ln_b, w_ffn1_in, w_ffn1_out, w_in, pool_w, pool_scale,
           mla_q_norm, mla_kv_norm, mla_w_uq, mla_w_uk, mla_w_uv, gla_w_a2, gla_b_a, gla_norm,
           w_branch_pool, w_branch_mla, w_branch_gla, w_out, mem_w_q, mem_w_k, mem_w_v, mem_w_o,
           w_ffn2_in, w_ffn2_out):
    nl = w_in.shape[0]
    bp, sp, _ = x_prompt.shape
    bs, ss, _ = x_sample.shape
    past = cache_mla_ckv.shape[2]
    n_mem = mem_prompt.shape[1]

    w_a, w_gates, w_q2, w_k, w_kt, w_v, w_abs, w_uvp, pool_bd, w_a2 = _pack_weights(
        w_in, pool_w, mla_w_uq, mla_w_uk, mla_w_uv, gla_w_a2)
    bf = lambda w: w.astype(BF16)
    ffn_w = ((bf(w_ffn1_in), bf(w_ffn1_out)), (bf(w_ffn2_in), bf(w_ffn2_out)))
    w_bp, w_bm, w_bg, w_o = bf(w_branch_pool), bf(w_branch_mla), bf(w_branch_gla), bf(w_out)
    mw_q, mw_k, mw_v, mw_o = bf(mem_w_q), bf(mem_w_k), bf(mem_w_v), bf(mem_w_o)
    lng = ln_g.reshape(nl, 4, 1, D_MODEL)
    lnb = ln_b.reshape(nl, 4, 1, D_MODEL)
    qnorm = mla_q_norm.reshape(nl, 1, MLA_Q_RANK)
    kvnorm = mla_kv_norm.reshape(nl, 1, MLA_KV_RANK)
    pscale = pool_scale.reshape(nl, 1, POOL_WIDTH)
    b_a = gla_b_a.reshape(nl, 1, GLA_K)
    norm_g = jnp.tile(gla_norm, (1, GLA_HEADS)).reshape(nl, 1, GLA_V)
    hsel = (jnp.arange(GLA_K)[:, None] // GLA_DK == jnp.arange(GLA_V)[None, :] // GLA_DV).astype(BF16)
    hblk = (jnp.arange(GLA_V)[:, None] // GLA_DV == jnp.arange(GLA_V)[None, :] // GLA_DV).astype(BF16)

    tmem = _tile(bp * n_mem, 512)
    mem_k_out, mem_v_out, mem_k, mem_v = _memkv(
        mem_prompt.reshape(bp * n_mem, D_MODEL), mw_k, mw_v, tmem)
    mem_k = mem_k.reshape(nl, bp, n_mem, D_MODEL)
    mem_v = mem_v.reshape(nl, bp, n_mem, D_MODEL)
    smem_k = cache_mem_k.reshape(nl, bs, n_mem, D_MODEL)
    smem_v = cache_mem_v.reshape(nl, bs, n_mem, D_MODEL)

    groups = {
        "p": dict(nb=bp, seq=sp, pos0=0, tm=_tile(sp, 1024)),
        "s": dict(nb=bs, seq=ss, pos0=past, tm=_tile(bs * ss, 512)),
    }
    for gname, gr in groups.items():
        seq, tm = gr["seq"], gr["tm"]
        pos = gr["pos0"] + jnp.arange(seq)
        tab = _rope_table(pos)
        if tm <= seq:
            gr["tab"], gr["tab_period"] = tab, seq // tm
        else:
            gr["tab"], gr["tab_period"] = jnp.tile(tab, (tm // seq, 1)), 1
        gr["tseq"] = _tile(seq, 512)
        gr["tgla"] = _tile(seq, 1024)
        gr["tpool"] = _tile(seq, 2048)
        gr["ttok"] = _tile(gr["nb"] * seq, 1024)
        gr["tmem"] = _tile(seq, 1024)

    xs = {"p": x_prompt.reshape(bp * sp, D_MODEL), "s": x_sample.reshape(bs * ss, D_MODEL)}
    zero_hist = jnp.zeros((bp, POOL_PAD, POOL_WIDTH), F32)
    zero_state = jnp.zeros((bp, GLA_V, GLA_K), F32)
    outs = {n: {"pool": [], "ckv": [], "kr": [], "gla": []} for n in groups}

    for l in range(nl):
        for gname, gr in groups.items():
            nb, seq, tm = gr["nb"], gr["seq"], gr["tm"]
            x = xs[gname]
            x = _ffn(x, ffn_w[0][0], ffn_w[0][1], lng, lnb, l, 0, gr["ttok"])
            k_tile = gr["tseq"] if gname == "p" else None
            u_pool, ckv, misc, q, k, v, gla_in = _inproj(
                x, gr["tab"], w_a, w_q2, w_k if k_tile is None else w_kt, w_v, qnorm, kvnorm,
                l, tm, gr["tab_period"], k_tile)
            if gname == "p":
                hist = None
                hist16 = zero_hist
                s0t = zero_state
            else:
                hist = cache_pool[l]
                hist16 = jnp.concatenate(
                    [jnp.zeros((nb, POOL_PAD - POOL_HIST, POOL_WIDTH), F32), hist], 1)
                s0t = _state_to_tiles(state_gla[l])
            y_pool = _pool(u_pool, hist16, pool_bd, pscale, l, nb, seq, gr["tpool"], gr["pos0"])
            if gname == "p":
                y_mla = _mla_causal(q, k, v, nb, seq, gr["tseq"])
            else:
                y_mla = _mla_cached(q, ckv, misc, cache_mla_ckv, cache_mla_krope, w_abs, w_uvp,
                                    l, nb, seq, past)
            y_gla, s_fin = _gla(gla_in, misc, w_a2, b_a, norm_g, hsel, hblk, s0t, l, nb, seq, gr["tgla"])
            x = _merge(x, y_pool, y_mla, y_gla, w_gates, w_bp, w_bm, w_bg, w_o, lng, lnb, l,
                       gr["ttok"])
            if gname == "p":
                x = _memattn(x, mem_k, mem_v, l, mw_q, mw_o, lng, lnb, l, nb, seq, gr["tmem"])
            else:
                x = _memattn(x, smem_k, smem_v, l, mw_q, mw_o, lng, lnb, l, nb, seq, gr["tmem"])
            x = _ffn(x, ffn_w[1][0], ffn_w[1][1], lng, lnb, l, 3, gr["ttok"])
            xs[gname] = x

            u3 = u_pool.reshape(nb, seq, POOL_WIDTH)
            if hist is None:
                hist = jnp.zeros((nb, POOL_HIST, POOL_WIDTH), F32)
            new_pool = jnp.concatenate([hist, u3], 1)[:, -POOL_HIST:]
            o = outs[gname]
            o["pool"].append(new_pool)
            o["ckv"].append(ckv.reshape(nb, seq, MLA_KV_RANK))
            o["kr"].append(misc[:, :MLA_ROPE].reshape(nb, seq, MLA_ROPE))
            o["gla"].append(_tiles_to_state(s_fin))

    st = lambda n, key: jnp.stack(outs[n][key], 0)
    return (xs["p"].reshape(bp, sp, D_MODEL), xs["s"].reshape(bs, ss, D_MODEL),
            st("p", "pool"), st("s", "pool"),
            st("p", "ckv"), st("s", "ckv"),
            st("p", "kr"), st("s", "kr"),
            st("p", "gla"), st("s", "gla"),
            mem_k_out.reshape(nl, bp, n_mem, MEM_HEADS, MEM_HEAD_DIM),
            mem_v_out.reshape(nl, bp, n_mem, MEM_HEADS, MEM_HEAD_DIM))
```

```python
import functools
import math

import jax
import jax.numpy as jnp
from jax import lax
from jax.experimental import pallas as pl
from jax.experimental.pallas import tpu as pltpu

F32 = jnp.float32
BF16 = jnp.bfloat16

D_MODEL = 1024
DEPTH = 4
CHUNK = 64
EPS = 1e-5
ALPHA = (2 * DEPTH) ** 0.25

POOL_WINDOWS = (2, 4, 8, 16)
POOL_GROUP = D_MODEL // 16
POOL_WIDTH = len(POOL_WINDOWS) * POOL_GROUP
POOL_HIST = max(POOL_WINDOWS) - 1

MLA_HEADS = 8
MLA_Q_RANK = D_MODEL // 4
MLA_KV_RANK = D_MODEL // 4
MLA_NOPE = 64
MLA_ROPE = 32
MLA_V = 64
MLA_SCALE = (MLA_NOPE + MLA_ROPE) ** -0.5
ROPE_BASE = 10000.0
MLA_HEAD_PAD = 128
MLA_PAIR = 2
MLA_GROUP = 4
V_ONE_EVEN = MLA_V
V_ONE_ODD = MLA_HEAD_PAD
LOG2E = math.log2(math.e)

GLA_HEADS = 4
GLA_DK = 32
GLA_DV = 64
GLA_GATE_RANK = 16
GLA_GATE_TAU = 16.0
GLA_K = GLA_HEADS * GLA_DK
GLA_V = GLA_HEADS * GLA_DV

MEM_HEADS = 4
MEM_HEAD_DIM = D_MODEL // MEM_HEADS
D_FF = 2816

V7X_LANES = 128
V7X_SUBLANES = 8
V7X_VMEM_BYTES = 64 * 1024 * 1024
VMEM_LIMIT = V7X_VMEM_BYTES - 8 * 1024 * 1024

A_POOL = 0
A_QLAT = 256
A_CKV = 512
A_M1 = 768
A_M2 = 896
A_GLA = 1024
A_COLS = 1792
MISC_W = 128
GLA_IN_W = 2 * GLA_K + 2 * GLA_V


def _cparams(sem):
    return pltpu.CompilerParams(dimension_semantics=sem, vmem_limit_bytes=VMEM_LIMIT)


def _resident(shape, index_map):
    return pl.BlockSpec(shape, index_map, pipeline_mode=pl.Buffered(1))


def _layer_norm(y, g, b):
    mu = jnp.mean(y, axis=-1, keepdims=True)
    d = y - mu
    var = jnp.mean(d * d, axis=-1, keepdims=True)
    return d * lax.rsqrt(var + EPS) * g + b


def _rms_norm(y, g):
    return y * lax.rsqrt(jnp.mean(y * y, axis=-1, keepdims=True) + EPS) * g


def _dot(a, b):
    return jnp.dot(a, b, preferred_element_type=F32)


def _dot_nt(a, b):
    return lax.dot_general(a, b, (((1,), (1,)), ((), ())), preferred_element_type=F32)


def _silu(x):
    return x * jax.nn.sigmoid(x)


ROW_SUB = 256


def _row_blocks(rows):
    sub = min(ROW_SUB, rows)
    return [slice(r, r + sub) for r in range(0, rows, sub)]


FFN_CHUNKS = ((0, 1024), (1024, 1024), (2048, 768))


def _ffn_kernel(x_ref, win_ref, wout_ref, g_ref, b_ref, o_ref, act_ref):
    x = x_ref[...]
    xb = x.astype(BF16)
    for c0, cw in FFN_CHUNKS:
        gate = _dot(xb, win_ref[:, c0:c0 + cw])
        up = _dot(xb, win_ref[:, D_FF + c0:D_FF + c0 + cw])
        act_ref[:, c0:c0 + cw] = (_silu(gate) * up).astype(BF16)
    for rs in _row_blocks(x_ref.shape[0]):
        y = _dot(act_ref[rs, :], wout_ref[...])
        o_ref[rs, :] = _layer_norm(ALPHA * x_ref[rs, :] + 0.5 * y, g_ref[...], b_ref[...])


def _ffn(x, w_in, w_out, ln_g, ln_b, layer, which, tm):
    t = x.shape[0]
    return pl.pallas_call(
        _ffn_kernel,
        grid=(t // tm,),
        in_specs=[
            pl.BlockSpec((tm, D_MODEL), lambda i: (i, 0)),
            _resident((None, D_MODEL, 2 * D_FF), lambda i: (layer, 0, 0)),
            _resident((None, D_FF, D_MODEL), lambda i: (layer, 0, 0)),
            _resident((None, None, 1, D_MODEL), lambda i: (layer, which, 0, 0)),
            _resident((None, None, 1, D_MODEL), lambda i: (layer, which, 0, 0)),
        ],
        out_specs=pl.BlockSpec((tm, D_MODEL), lambda i: (i, 0)),
        out_shape=jax.ShapeDtypeStruct((t, D_MODEL), F32),
        scratch_shapes=[pltpu.VMEM((tm, D_FF), BF16)],
        compiler_params=_cparams(("parallel",)),
        name="ffn",
    )(x, w_in, w_out, ln_g, ln_b)


def _v_with_ones(vv):
    lane = lax.broadcasted_iota(jnp.int32, vv.shape, 1) % (MLA_PAIR * MLA_HEAD_PAD)
    return jnp.where((lane == V_ONE_EVEN) | (lane == V_ONE_ODD), 1.0, vv).astype(BF16)


def _inproj_kernel(x_ref, tab_ref, wa_ref, wq_ref, wk_ref, wv_ref, qn_ref, kvn_ref,
                   pool_ref, ckv_ref, misc_ref, q_ref, k_ref, v_ref, gla_ref, *, k_transposed):
    xb = x_ref[...].astype(BF16)
    h = _dot(xb, wa_ref[:, 0:A_GLA])
    gla_ref[...] = _dot(xb, wa_ref[:, A_GLA:A_COLS])
    pool_ref[...] = h[:, A_POOL:A_POOL + POOL_WIDTH]

    cm = tab_ref[:, 0:128]
    sm = tab_ref[:, 128:256]
    misc = h[:, A_M1:A_M1 + MISC_W] * cm + h[:, A_M2:A_M2 + MISC_W] * sm
    misc_ref[...] = misc

    ckv = _rms_norm(h[:, A_CKV:A_CKV + MLA_KV_RANK], kvn_ref[...])
    ckv_ref[...] = ckv
    ckvb = ckv.astype(BF16)
    lat = jnp.concatenate([ckvb, misc.astype(BF16)], axis=-1)
    if k_transposed:
        kt = _dot_nt(wk_ref[...], lat).astype(BF16)
        k_tile = k_ref.shape[-1]
        for c in range(k_ref.shape[0]):
            k_ref[c] = kt[:, c * k_tile:(c + 1) * k_tile]
    else:
        k_ref[...] = _dot(lat, wk_ref[...]).astype(BF16)
    v_ref[...] = _v_with_ones(_dot(ckvb, wv_ref[...]))

    qn = _rms_norm(h[:, A_QLAT:A_QLAT + MLA_Q_RANK], qn_ref[...]).astype(BF16)
    q2 = _dot(qn, wq_ref[...])
    kw = MLA_HEADS * MLA_HEAD_PAD
    cq = tab_ref[:, 256:384]
    sq = tab_ref[:, 384:512]
    for hd in range(MLA_HEADS):
        a = hd * MLA_HEAD_PAD
        q_ref[:, a:a + MLA_HEAD_PAD] = (
            q2[:, a:a + MLA_HEAD_PAD] * cq + q2[:, kw + a:kw + a + MLA_HEAD_PAD] * sq).astype(BF16)


def _inproj(x, tab, w_a, w_q2, w_k, w_v, qnorm, kvnorm, layer, tm, tab_period, k_tile):
    t = x.shape[0]
    kw = MLA_HEADS * MLA_HEAD_PAD
    lat_w = MLA_KV_RANK + MISC_W
    row = lambda i: (i, 0)
    k_transposed = k_tile is not None
    if k_transposed:
        k_shape = jax.ShapeDtypeStruct((t // k_tile, kw, k_tile), BF16)
        k_spec = pl.BlockSpec((tm // k_tile, kw, k_tile), lambda i: (i, 0, 0))
        wk_spec = _resident((None, kw, lat_w), lambda i: (layer, 0, 0))
    else:
        k_shape = jax.ShapeDtypeStruct((t, kw), BF16)
        k_spec = pl.BlockSpec((tm, kw), row)
        wk_spec = _resident((None, lat_w, kw), lambda i: (layer, 0, 0))
    out_shapes = (
        jax.ShapeDtypeStruct((t, POOL_WIDTH), F32),
        jax.ShapeDtypeStruct((t, MLA_KV_RANK), F32),
        jax.ShapeDtypeStruct((t, MISC_W), F32),
        jax.ShapeDtypeStruct((t, kw), BF16),
        k_shape,
        jax.ShapeDtypeStruct((t, kw), BF16),
        jax.ShapeDtypeStruct((t, GLA_IN_W), F32),
    )
    out_specs = [pl.BlockSpec((tm, s.shape[-1]), row) for s in out_shapes]
    out_specs[4] = k_spec
    return pl.pallas_call(
        functools.partial(_inproj_kernel, k_transposed=k_transposed),
        grid=(t // tm,),
        in_specs=[
            pl.BlockSpec((tm, D_MODEL), row),
            pl.BlockSpec((tm, 512), lambda i: (i % tab_period, 0)),
            _resident((None, D_MODEL, A_COLS), lambda i: (layer, 0, 0)),
            _resident((None, MLA_Q_RANK, 2 * kw), lambda i: (layer, 0, 0)),
            wk_spec,
            _resident((None, MLA_KV_RANK, kw), lambda i: (layer, 0, 0)),
            _resident((None, 1, MLA_Q_RANK), lambda i: (layer, 0, 0)),
            _resident((None, 1, MLA_KV_RANK), lambda i: (layer, 0, 0)),
        ],
        out_specs=out_specs,
        out_shape=out_shapes,
        compiler_params=_cparams(("parallel",)),
        name="inproj",
    )(x, tab, w_a, w_q2, w_k, w_v, qnorm, kvnorm)


POOL_PAD = 16


def _pool_kernel(u_ref, hist_ref, w_ref, scale_ref, y_ref, ext_ref, *, tp, pos0):
    j = pl.program_id(1)

    @pl.when(j == 0)
    def _():
        ext_ref[0:POOL_PAD, :] = hist_ref[...]

    tok = u_ref[...]
    ext_ref[POOL_PAD:POOL_PAD + tp, :] = tok
    acc = ext_ref[...]
    sums = {}
    w = 1
    while w < max(POOL_WINDOWS):
        acc = acc + pltpu.roll(acc, w, 0)
        w *= 2
        if w in POOL_WINDOWS:
            sums[w] = acc[POOL_PAD:POOL_PAD + tp, :]
    lane = lax.broadcasted_iota(jnp.int32, (tp, POOL_WIDTH), 1)
    pos = pos0 + j * tp + lax.broadcasted_iota(jnp.int32, (tp, POOL_WIDTH), 0)
    total = None
    width = None
    for gi, w in enumerate(POOL_WINDOWS):
        in_group = lane >= gi * POOL_GROUP
        total = sums[w] if total is None else jnp.where(in_group, sums[w], total)
        width = jnp.full_like(pos, w) if width is None else jnp.where(in_group, w, width)
    cnt = jnp.minimum(pos + 1, width).astype(F32)
    pooled = total / cnt - tok
    y = _dot(pooled.astype(BF16), w_ref[...]) * scale_ref[...]
    y_ref[...] = y.astype(BF16)
    ext_ref[0:POOL_PAD, :] = ext_ref[tp:tp + POOL_PAD, :]


def _pool(u, hist16, w_bd, scale, layer, nb, seq, tp, pos0):
    nt = seq // tp
    return pl.pallas_call(
        functools.partial(_pool_kernel, tp=tp, pos0=pos0),
        grid=(nb, nt),
        in_specs=[
            pl.BlockSpec((tp, POOL_WIDTH), lambda b, j: (b * nt + j, 0)),
            pl.BlockSpec((None, POOL_PAD, POOL_WIDTH), lambda b, j: (b, 0, 0)),
            _resident((None, POOL_WIDTH, POOL_WIDTH), lambda b, j: (layer, 0, 0)),
            _resident((None, 1, POOL_WIDTH), lambda b, j: (layer, 0, 0)),
        ],
        out_specs=pl.BlockSpec((tp, POOL_WIDTH), lambda b, j: (b * nt + j, 0)),
        out_shape=jax.ShapeDtypeStruct((nb * seq, POOL_WIDTH), BF16),
        scratch_shapes=[pltpu.VMEM((POOL_PAD + tp, POOL_WIDTH), F32)],
        compiler_params=_cparams(("parallel", "arbitrary")),
        name="pool",
    )(u, hist16, w_bd, scale)


def _pair_output(acc0, acc1):
    lane = lax.broadcasted_iota(jnp.int32, acc0.shape, 1)
    o0 = acc0 / acc0[:, V_ONE_EVEN:V_ONE_EVEN + 1]
    o1 = acc1 / acc1[:, 0:1]
    return jnp.where(lane < MLA_V, o0, o1).astype(BF16)


def _mla_causal_kernel(q_ref, kt_ref, v_ref, o_ref, m_ref, acc_ref, sa_ref, sb_ref, *, tq):
    i = pl.program_id(2)
    m_ref[...] = jnp.full(m_ref.shape, -jnp.inf, F32)
    acc_ref[...] = jnp.zeros(acc_ref.shape, F32)
    reps = tq // V7X_LANES

    def scores(kt, dst_ref):
        for h in range(MLA_GROUP):
            a = h * MLA_HEAD_PAD
            dst_ref[h] = _dot(q_ref[:, a:a + MLA_HEAD_PAD], kt_ref[kt, a:a + MLA_HEAD_PAD, :])

    def consume(src_ref, kt, masked):
        r0 = pl.multiple_of(kt * tq, tq)
        for h in range(MLA_GROUP):
            a = h * MLA_HEAD_PAD
            s = src_ref[h]
            if masked:
                qc = lax.broadcasted_iota(jnp.int32, (tq, tq), 0) // CHUNK
                kc = lax.broadcasted_iota(jnp.int32, (tq, tq), 1) // CHUNK
                s = jnp.where(kc <= qc, s, -jnp.inf)
            m_prev = m_ref[h]
            m_new = jnp.maximum(m_prev, jnp.max(s, axis=-1, keepdims=True))
            p = jnp.exp2(s - jnp.tile(m_new, (1, reps)))
            corr = jnp.exp2(m_prev - m_new)
            acc_ref[h] = corr * acc_ref[h] + _dot(p.astype(BF16),
                                                  v_ref[pl.ds(r0, tq), a:a + MLA_HEAD_PAD])
            m_ref[h] = m_new

    scores(0, sa_ref)

    def body(j, carry):
        scores(2 * j + 1, sb_ref)
        consume(sa_ref, 2 * j, False)
        scores(2 * j + 2, sa_ref)
        consume(sb_ref, 2 * j + 1, False)
        return carry

    lax.fori_loop(0, i // 2, body, 0)

    @pl.when(i % 2 == 0)
    def _():
        consume(sa_ref, i, True)

    @pl.when(i % 2 == 1)
    def _():
        scores(i, sb_ref)
        consume(sa_ref, i - 1, False)
        consume(sb_ref, i, True)

    ow = MLA_PAIR * MLA_V
    for pr in range(MLA_GROUP // MLA_PAIR):
        o_ref[:, pr * ow:(pr + 1) * ow] = _pair_output(acc_ref[MLA_PAIR * pr], acc_ref[MLA_PAIR * pr + 1])


def _mla_causal(q, kt, v, nb, seq, tq):
    nq = seq // tq
    ngrp = MLA_HEADS // MLA_GROUP
    pw = MLA_GROUP * MLA_HEAD_PAD
    return pl.pallas_call(
        functools.partial(_mla_causal_kernel, tq=tq),
        grid=(nb, ngrp, nq),
        in_specs=[
            pl.BlockSpec((tq, pw), lambda b, p, i: (b * nq + i, p)),
            pl.BlockSpec((nq, pw, tq), lambda b, p, i: (b, p, 0)),
            pl.BlockSpec((seq, pw), lambda b, p, i: (b, p)),
        ],
        out_specs=pl.BlockSpec((tq, MLA_GROUP * MLA_V), lambda b, p, i: (b * nq + i, p)),
        out_shape=jax.ShapeDtypeStruct((nb * seq, MLA_HEADS * MLA_V), BF16),
        scratch_shapes=[pltpu.VMEM((MLA_GROUP, tq, V7X_LANES), F32),
                        pltpu.VMEM((MLA_GROUP, tq, MLA_HEAD_PAD), F32),
                        pltpu.VMEM((MLA_GROUP, tq, tq), F32),
                        pltpu.VMEM((MLA_GROUP, tq, tq), F32)],
        compiler_params=_cparams(("parallel", "parallel", "arbitrary")),
        name="mla_causal",
    )(q, kt, v)


def _mla_cached_kernel(q_ref, cp_ref, rp_ref, cn_ref, mn_ref, wabs_ref, wuv_ref, o_ref):
    qa = jnp.concatenate(
        [_dot(q_ref[:, h * MLA_HEAD_PAD:(h + 1) * MLA_HEAD_PAD], wabs_ref[h]) for h in range(MLA_HEADS)],
        axis=0).astype(BF16)
    qa_lat = qa[:, 0:MLA_KV_RANK]
    qa_misc = qa[:, MLA_KV_RANK:]
    cp = cp_ref[...].astype(BF16)
    cn = cn_ref[...].astype(BF16)
    s1 = _dot_nt(qa_lat, cp) + _dot_nt(qa_misc[:, 0:MLA_ROPE], rp_ref[...].astype(BF16))
    s2 = _dot_nt(qa_lat, cn) + _dot_nt(qa_misc, mn_ref[...].astype(BF16))
    m = jnp.maximum(jnp.max(s1, axis=-1, keepdims=True), jnp.max(s2, axis=-1, keepdims=True))
    p1 = jnp.exp2(s1 - m)
    p2 = jnp.exp2(s2 - m)
    l = jnp.sum(p1, axis=-1, keepdims=True) + jnp.sum(p2, axis=-1, keepdims=True)
    lat = ((_dot(p1.astype(BF16), cp) + _dot(p2.astype(BF16), cn)) / l).astype(BF16)
    seq = q_ref.shape[0]
    y = _dot(lat[0:seq], wuv_ref[0])
    for h in range(1, MLA_HEADS):
        y = y + _dot(lat[h * seq:(h + 1) * seq], wuv_ref[h])
    o_ref[...] = y.astype(BF16)


def _mla_cached(q, ckv_new, misc_new, ckv_past, kr_past, w_abs, w_uvp, layer, nb, seq, past):
    kw = MLA_HEADS * MLA_HEAD_PAD
    vw = MLA_HEADS * MLA_V
    lat_w = MLA_KV_RANK + MISC_W
    return pl.pallas_call(
        _mla_cached_kernel,
        grid=(nb,),
        in_specs=[
            pl.BlockSpec((seq, kw), lambda b: (b, 0)),
            pl.BlockSpec((None, None, past, MLA_KV_RANK), lambda b: (layer, b, 0, 0)),
            pl.BlockSpec((None, None, past, MLA_ROPE), lambda b: (layer, b, 0, 0)),
            pl.BlockSpec((seq, MLA_KV_RANK), lambda b: (b, 0)),
            pl.BlockSpec((seq, MISC_W), lambda b: (b, 0)),
            _resident((None, MLA_HEADS, MLA_HEAD_PAD, lat_w), lambda b: (layer, 0, 0, 0)),
            _resident((None, MLA_HEADS, MLA_KV_RANK, vw), lambda b: (layer, 0, 0, 0)),
        ],
        out_specs=pl.BlockSpec((seq, vw), lambda b: (b, 0)),
        out_shape=jax.ShapeDtypeStruct((nb * seq, vw), BF16),
        compiler_params=_cparams(("parallel",)),
        name="mla_cached",
    )(q, ckv_past, kr_past, ckv_new, misc_new, w_abs, w_uvp)


def _gla_kernel(gq_ref, gk_ref, gv_ref, gr_ref, misc_ref, wa2_ref, ba_ref, ng_ref, hsel_ref,
                hblk_ref, s0_ref, y_ref, sfin_ref, st_ref, sec_ref, fst_ref, same_ref, kbd_ref,
                vbd_ref, sbd_ref, o_ref, *, tt):
    j = pl.program_id(1)
    c = min(CHUNK, tt)
    hc = GLA_HEADS * c
    levels = c.bit_length() - 1

    @pl.when(j == 0)
    def _():
        st_ref[...] = s0_ref[...]
        row = lax.broadcasted_iota(jnp.int32, (c, GLA_K), 0)
        trow = lax.broadcasted_iota(jnp.int32, (c, hc), 0)
        tcol = lax.broadcasted_iota(jnp.int32, (c, hc), 1) % c
        for lv in range(levels):
            m = 1 << lv
            sec_ref[lv] = ((row & m) != 0).astype(F32)
            fst_ref[lv] = ((row & m) == 0).astype(F32)
            same_ref[lv] = ((trow // (2 * m)) == (tcol // (2 * m))).astype(F32)
        kbd_ref[...] = (lax.broadcasted_iota(jnp.int32, (hc, GLA_K), 0) // c
                        == lax.broadcasted_iota(jnp.int32, (hc, GLA_K), 1) // GLA_DK).astype(BF16)
        vbd_ref[...] = (lax.broadcasted_iota(jnp.int32, (hc, GLA_V), 0) // c
                        == lax.broadcasted_iota(jnp.int32, (hc, GLA_V), 1) // GLA_DV).astype(BF16)
        sbd_ref[...] = (lax.broadcasted_iota(jnp.int32, (GLA_V, GLA_K), 0) // GLA_DV
                        == lax.broadcasted_iota(jnp.int32, (GLA_V, GLA_K), 1) // GLA_DK).astype(F32)

    z = _dot(misc_ref[...].astype(BF16), wa2_ref[...]) + ba_ref[...]
    la_all = jax.nn.log_sigmoid(z) * (LOG2E / GLA_GATE_TAU)

    st = st_ref[...]
    for r in range(tt // c):
        rs = slice(r * c, (r + 1) * c)
        q = gq_ref[rs, :] * (GLA_DK ** -0.5)
        k = gk_ref[rs, :]
        v = gv_ref[rs, :]
        la = la_all[rs, :]

        scores = jnp.zeros((c, hc), F32)
        p = la
        g = la
        for lv in range(levels):
            m = 1 << lv
            if m < V7X_SUBLANES:
                sec = sec_ref[lv]
                fst = fst_ref[lv]
                qe = (q * jnp.exp2(p) * sec).astype(BF16)
                ke = (k * jnp.exp2(g - p) * fst).astype(BF16)
                ke_bd = jnp.tile(ke, (GLA_HEADS, 1)) * kbd_ref[...]
                scores = scores + _dot_nt(qe, ke_bd) * same_ref[lv]
                up = sec * pltpu.roll(g, m, 0)
                p = p + up
                g = g + up + fst * pltpu.roll(g, c - m, 0)
            else:
                nb2 = c // (2 * m)

                def halves(a):
                    a4 = a.reshape(nb2, 2, m, a.shape[-1])
                    return a4[:, 0], a4[:, 1]

                def join(lo, hi):
                    return jnp.stack([lo, hi], axis=1).reshape(c, lo.shape[-1])

                p_lo, p_hi = halves(p)
                g_lo, g_hi = halves(g)
                qe_hi = (halves(q)[1] * jnp.exp2(p_hi)).reshape(c // 2, GLA_K).astype(BF16)
                ke_lo = halves(k)[0] * jnp.exp2(g_lo - p_lo)
                ke = join(ke_lo, jnp.zeros_like(ke_lo)).astype(BF16)
                ke_bd = jnp.tile(ke, (GLA_HEADS, 1)) * kbd_ref[...]
                a_hi = _dot_nt(qe_hi, ke_bd) * halves(same_ref[lv])[1].reshape(c // 2, hc)
                s_lo, s_hi = halves(scores)
                scores = join(s_lo, s_hi + a_hi.reshape(nb2, m, hc))
                tot = g_lo + g_hi
                p = join(p_lo, p_hi + g_lo)
                g = join(tot, tot)

        o = _dot_nt((q * jnp.exp2(p)).astype(BF16), st.astype(BF16))
        v_bd = jnp.tile(v.astype(BF16), (GLA_HEADS, 1)) * vbd_ref[...]
        o_ref[rs, :] = o + _dot(scores.astype(BF16), v_bd)

        kdec = (k * jnp.exp2(g - p)).astype(BF16)
        upd = _dot(jnp.transpose(v).astype(BF16), kdec)
        st = st * jnp.exp2(g[0:1, :]) + upd * sbd_ref[...]

    st_ref[...] = st
    sfin_ref[...] = st

    qk = (gq_ref[...] * (GLA_DK ** -0.5) * gk_ref[...]).astype(BF16)
    o = o_ref[...] + _dot(qk, hsel_ref[...]) * gv_ref[...]
    o2 = o * o
    o2_hi = o2.astype(BF16)
    o2_lo = (o2 - o2_hi.astype(F32)).astype(BF16)
    ms = (_dot(o2_hi, hblk_ref[...]) + _dot(o2_lo, hblk_ref[...])) * (1.0 / GLA_DV)
    y = o * lax.rsqrt(ms + EPS) * ng_ref[...] * _silu(gr_ref[...])
    y_ref[...] = y.astype(BF16)


def _gla(gla_in, misc, w_a2, b_a, norm_g, hsel, hblk, s0t, layer, nb, seq, tt):
    nt = seq // tt
    row = lambda b, j: (b * nt + j, 0)
    c = min(CHUNK, tt)
    hc = GLA_HEADS * c
    levels = c.bit_length() - 1
    return pl.pallas_call(
        functools.partial(_gla_kernel, tt=tt),
        grid=(nb, nt),
        in_specs=[
            pl.BlockSpec((tt, GLA_K), lambda b, j: (b * nt + j, 0)),
            pl.BlockSpec((tt, GLA_K), lambda b, j: (b * nt + j, 1)),
            pl.BlockSpec((tt, GLA_V), lambda b, j: (b * nt + j, 1)),
            pl.BlockSpec((tt, GLA_V), lambda b, j: (b * nt + j, 2)),
            pl.BlockSpec((tt, MISC_W), row),
            _resident((None, MISC_W, GLA_K), lambda b, j: (layer, 0, 0)),
            _resident((None, 1, GLA_K), lambda b, j: (layer, 0, 0)),
            _resident((None, 1, GLA_V), lambda b, j: (layer, 0, 0)),
            _resident((GLA_K, GLA_V), lambda b, j: (0, 0)),
            _resident((GLA_V, GLA_V), lambda b, j: (0, 0)),
            pl.BlockSpec((None, GLA_V, GLA_K), lambda b, j: (b, 0, 0)),
        ],
        out_specs=[pl.BlockSpec((tt, GLA_V), row),
                   pl.BlockSpec((None, GLA_V, GLA_K), lambda b, j: (b, 0, 0))],
        out_shape=(jax.ShapeDtypeStruct((nb * seq, GLA_V), BF16),
                   jax.ShapeDtypeStruct((nb, GLA_V, GLA_K), F32)),
        scratch_shapes=[pltpu.VMEM((GLA_V, GLA_K), F32),
                        pltpu.VMEM((levels, c, GLA_K), F32),
                        pltpu.VMEM((levels, c, GLA_K), F32),
                        pltpu.VMEM((levels, c, hc), F32),
                        pltpu.VMEM((hc, GLA_K), BF16),
                        pltpu.VMEM((hc, GLA_V), BF16),
                        pltpu.VMEM((GLA_V, GLA_K), F32),
                        pltpu.VMEM((tt, GLA_V), F32)],
        compiler_params=_cparams(("parallel", "arbitrary")),
        name="gla",
    )(gla_in, gla_in, gla_in, gla_in, misc, w_a2, b_a, norm_g, hsel, hblk, s0t)


def _merge_kernel(x_ref, yp_ref, ym_ref, yg_ref, wg_ref, wbp_ref, wbm_ref, wbg_ref, wo_ref,
                  g_ref, b_ref, o_ref):
    x = x_ref[...]
    xb = x.astype(BF16)
    m = jax.nn.sigmoid(_dot(xb, wg_ref[:, 0:D_MODEL])) * _dot(yp_ref[...], wbp_ref[...])
    m = m + jax.nn.sigmoid(_dot(xb, wg_ref[:, D_MODEL:2 * D_MODEL])) * _dot(ym_ref[...], wbm_ref[...])
    m = m + jax.nn.sigmoid(_dot(xb, wg_ref[:, 2 * D_MODEL:3 * D_MODEL])) * _dot(yg_ref[...], wbg_ref[...])
    mb = m.astype(BF16)
    for rs in _row_blocks(x_ref.shape[0]):
        y = _dot(mb[rs, :], wo_ref[...])
        o_ref[rs, :] = _layer_norm(ALPHA * x_ref[rs, :] + y, g_ref[...], b_ref[...])


def _merge(x, y_pool, y_mla, y_gla, w_gates, w_bp, w_bm, w_bg, w_out, ln_g, ln_b, layer, tm):
    t = x.shape[0]
    row = lambda i: (i, 0)
    lay = lambda i: (layer, 0, 0)
    return pl.pallas_call(
        _merge_kernel,
        grid=(t // tm,),
        in_specs=[
            pl.BlockSpec((tm, D_MODEL), row),
            pl.BlockSpec((tm, POOL_WIDTH), row),
            pl.BlockSpec((tm, MLA_HEADS * MLA_V), row),
            pl.BlockSpec((tm, GLA_V), row),
            _resident((None, D_MODEL, 3 * D_MODEL), lay),
            _resident((None, POOL_WIDTH, D_MODEL), lay),
            _resident((None, MLA_HEADS * MLA_V, D_MODEL), lay),
            _resident((None, GLA_V, D_MODEL), lay),
            _resident((None, D_MODEL, D_MODEL), lay),
            _resident((None, None, 1, D_MODEL), lambda i: (layer, 1, 0, 0)),
            _resident((None, None, 1, D_MODEL), lambda i: (layer, 1, 0, 0)),
        ],
        out_specs=pl.BlockSpec((tm, D_MODEL), row),
        out_shape=jax.ShapeDtypeStruct((t, D_MODEL), F32),
        compiler_params=_cparams(("parallel",)),
        name="merge",
    )(x, y_pool, y_mla, y_gla, w_gates, w_bp, w_bm, w_bg, w_out, ln_g, ln_b)


def _memkv_kernel(m_ref, wk_ref, wv_ref, k_ref, v_ref, kb_ref, vb_ref):
    mb = m_ref[...].astype(BF16)
    k = _dot(mb, wk_ref[...])
    v = _dot(mb, wv_ref[...])
    kb_ref[...] = k.astype(BF16)
    vb_ref[...] = v.astype(BF16)
    for h in range(MEM_HEADS):
        a = h * MEM_HEAD_DIM
        k_ref[:, h, :] = k[:, a:a + MEM_HEAD_DIM]
        v_ref[:, h, :] = v[:, a:a + MEM_HEAD_DIM]


def _memkv(mem, w_k, w_v, tm):
    rows = mem.shape[0]
    nl = w_k.shape[0]
    split = pl.BlockSpec((None, tm, MEM_HEADS, MEM_HEAD_DIM), lambda l, i: (l, i, 0, 0))
    flat = pl.BlockSpec((None, tm, D_MODEL), lambda l, i: (l, i, 0))
    return pl.pallas_call(
        _memkv_kernel,
        grid=(nl, rows // tm),
        in_specs=[
            pl.BlockSpec((tm, D_MODEL), lambda l, i: (i, 0)),
            pl.BlockSpec((None, D_MODEL, D_MODEL), lambda l, i: (l, 0, 0)),
            pl.BlockSpec((None, D_MODEL, D_MODEL), lambda l, i: (l, 0, 0)),
        ],
        out_specs=[split, split, flat, flat],
        out_shape=(jax.ShapeDtypeStruct((nl, rows, MEM_HEADS, MEM_HEAD_DIM), F32),
                   jax.ShapeDtypeStruct((nl, rows, MEM_HEADS, MEM_HEAD_DIM), F32),
                   jax.ShapeDtypeStruct((nl, rows, D_MODEL), BF16),
                   jax.ShapeDtypeStruct((nl, rows, D_MODEL), BF16)),
        compiler_params=_cparams(("parallel", "parallel")),
        name="memkv",
    )(mem, w_k, w_v)


def _memattn_kernel(x_ref, k_ref, v_ref, wq_ref, wo_ref, g_ref, b_ref, o_ref):
    x = x_ref[...]
    q = (_dot(x.astype(BF16), wq_ref[...]) * (MEM_HEAD_DIM ** -0.5 * LOG2E)).astype(BF16)
    kb = k_ref[...].astype(BF16)
    vb = v_ref[...].astype(BF16)
    outs = []
    for h in range(MEM_HEADS):
        a = h * MEM_HEAD_DIM
        s = _dot_nt(q[:, a:a + MEM_HEAD_DIM], kb[:, a:a + MEM_HEAD_DIM])
        p = jnp.exp2(s - jnp.max(s, axis=-1, keepdims=True))
        inv_l = 1.0 / jnp.sum(p, axis=-1, keepdims=True)
        outs.append(_dot(p.astype(BF16), vb[:, a:a + MEM_HEAD_DIM]) * inv_l)
    o = jnp.concatenate(outs, axis=-1).astype(BF16)
    for rs in _row_blocks(x_ref.shape[0]):
        y = _dot(o[rs, :], wo_ref[...])
        o_ref[rs, :] = _layer_norm(ALPHA * x_ref[rs, :] + y, g_ref[...], b_ref[...])


def _memattn(x, mem_k, mem_v, kv_layer, w_q, w_o, ln_g, ln_b, layer, nb, seq, tm):
    nt = seq // tm
    n_mem = mem_k.shape[2]
    lay = lambda b, i: (layer, 0, 0)
    return pl.pallas_call(
        _memattn_kernel,
        grid=(nb, nt),
        in_specs=[
            pl.BlockSpec((tm, D_MODEL), lambda b, i: (b * nt + i, 0)),
            pl.BlockSpec((None, None, n_mem, D_MODEL), lambda b, i: (kv_layer, b, 0, 0)),
            pl.BlockSpec((None, None, n_mem, D_MODEL), lambda b, i: (kv_layer, b, 0, 0)),
            _resident((None, D_MODEL, D_MODEL), lay),
            _resident((None, D_MODEL, D_MODEL), lay),
            _resident((None, None, 1, D_MODEL), lambda b, i: (layer, 2, 0, 0)),
            _resident((None, None, 1, D_MODEL), lambda b, i: (layer, 2, 0, 0)),
        ],
        out_specs=pl.BlockSpec((tm, D_MODEL), lambda b, i: (b * nt + i, 0)),
        out_shape=jax.ShapeDtypeStruct((nb * seq, D_MODEL), F32),
        compiler_params=_cparams(("parallel", "parallel")),
        name="memattn",
    )(x, mem_k, mem_v, w_q, w_o, ln_g, ln_b)


def _pack_weights(w_in, pool_w, mla_w_uq, mla_w_uk, mla_w_uv, gla_w_a2):
    nl = w_in.shape[0]
    splits = (POOL_WIDTH, MLA_Q_RANK, MLA_KV_RANK, MLA_ROPE, GLA_K, GLA_K, GLA_V,
              GLA_GATE_RANK, GLA_V, 3 * D_MODEL)
    offs = [0]
    for s in splits:
        offs.append(offs[-1] + s)
    col = lambda n: w_in[:, :, offs[n]:offs[n + 1]]
    half = MLA_ROPE // 2
    kr = col(3)
    kr_sw = jnp.concatenate([kr[..., half:], kr[..., :half]], -1)
    zeros = lambda n: jnp.zeros((nl, D_MODEL, n), w_in.dtype)
    w_a = jnp.concatenate(
        [col(0), col(1), col(2),
         kr, col(7), zeros(MISC_W - MLA_ROPE - GLA_GATE_RANK),
         kr_sw, zeros(MISC_W - MLA_ROPE),
         col(4), col(5), col(6), col(8)], -1).astype(BF16)
    w_gates = col(9).astype(BF16)

    uq = mla_w_uq.reshape(nl, MLA_Q_RANK, MLA_HEADS, MLA_NOPE + MLA_ROPE)
    uq_n, uq_r = uq[..., :MLA_NOPE], uq[..., MLA_NOPE:]
    uq_rs = jnp.concatenate([uq_r[..., half:], uq_r[..., :half]], -1)
    padq = jnp.zeros((nl, MLA_Q_RANK, MLA_HEADS, MLA_HEAD_PAD - MLA_NOPE - MLA_ROPE), uq.dtype)
    q_main = jnp.concatenate([uq_n, uq_r, padq], -1)
    q_swap = jnp.concatenate([jnp.zeros_like(uq_n), uq_rs, padq], -1)
    kw = MLA_HEADS * MLA_HEAD_PAD
    w_q2 = jnp.concatenate([q_main.reshape(nl, MLA_Q_RANK, kw),
                            q_swap.reshape(nl, MLA_Q_RANK, kw)], -1).astype(BF16)

    padk = jnp.zeros((nl, MLA_KV_RANK, MLA_HEADS, MLA_HEAD_PAD - MLA_NOPE), mla_w_uk.dtype)
    k_lat = jnp.concatenate([mla_w_uk, padk], -1).reshape(nl, MLA_KV_RANK, kw)
    place = jnp.zeros((MISC_W, MLA_HEADS, MLA_HEAD_PAD), F32)
    idx = jnp.arange(MLA_ROPE)
    place = place.at[idx, :, MLA_NOPE + idx].set(1.0).reshape(MISC_W, kw)
    k_misc = jnp.broadcast_to(place, (nl, MISC_W, kw))
    w_k = jnp.concatenate([k_lat, k_misc], 1).astype(BF16)
    w_kt = jnp.swapaxes(w_k, 1, 2)
    uv = mla_w_uv.reshape(nl, MLA_KV_RANK, MLA_HEADS // MLA_PAIR, MLA_PAIR, MLA_V)
    padv = jnp.zeros_like(uv[:, :, :, 0])
    w_v = jnp.stack([jnp.concatenate([uv[:, :, :, 0], padv], -1),
                     jnp.concatenate([padv, uv[:, :, :, 1]], -1)], 3)
    w_v = w_v.reshape(nl, MLA_KV_RANK, kw).astype(BF16)

    uk_t = jnp.transpose(mla_w_uk, (0, 2, 3, 1))
    a_lat = jnp.concatenate(
        [uk_t, jnp.zeros((nl, MLA_HEADS, MLA_HEAD_PAD - MLA_NOPE, MLA_KV_RANK), uk_t.dtype)], 2)
    a_misc = jnp.zeros((MLA_HEAD_PAD, MISC_W), F32).at[MLA_NOPE + idx, idx].set(1.0)
    a_misc = jnp.broadcast_to(a_misc, (nl, MLA_HEADS, MLA_HEAD_PAD, MISC_W))
    w_abs = jnp.concatenate([a_lat, a_misc], -1).astype(BF16)
    eye_h = jnp.eye(MLA_HEADS, dtype=mla_w_uv.dtype)
    uv_h = jnp.transpose(mla_w_uv, (0, 2, 1, 3))
    w_uvp = (uv_h[:, :, :, None, :] * eye_h[None, :, None, :, None]).reshape(
        nl, MLA_HEADS, MLA_KV_RANK, MLA_HEADS * MLA_V).astype(BF16)

    ng = len(POOL_WINDOWS)
    eye = jnp.eye(ng, dtype=pool_w.dtype)
    pool_bd = (pool_w[:, :, :, None, :] * eye[None, :, None, :, None]).reshape(
        nl, POOL_WIDTH, POOL_WIDTH).astype(BF16)

    w_a2 = jnp.zeros((nl, MISC_W, GLA_K), gla_w_a2.dtype)
    w_a2 = w_a2.at[:, MLA_ROPE:MLA_ROPE + GLA_GATE_RANK, :].set(gla_w_a2).astype(BF16)
    return w_a, w_gates, w_q2, w_k, w_kt, w_v, w_abs, w_uvp, pool_bd, w_a2


def _rope_table(pos):
    half = MLA_ROPE // 2
    inv = ROPE_BASE ** (-jnp.arange(half, dtype=F32) / half)
    ang = pos.astype(F32)[:, None] * inv
    cos, sin = jnp.cos(ang), jnp.sin(ang)
    n = pos.shape[0]
    one = jnp.ones((n, GLA_GATE_RANK), F32)
    cm = jnp.concatenate([cos, cos, one, jnp.zeros((n, MISC_W - MLA_ROPE - GLA_GATE_RANK), F32)], -1)
    sm = jnp.concatenate([-sin, sin, jnp.zeros((n, MISC_W - MLA_ROPE), F32)], -1)
    padq = jnp.zeros((n, MLA_HEAD_PAD - MLA_NOPE - MLA_ROPE), F32)
    qs = MLA_SCALE * LOG2E
    cq = jnp.concatenate([jnp.ones((n, MLA_NOPE), F32), cos, cos, padq], -1) * qs
    sq = jnp.concatenate([jnp.zeros((n, MLA_NOPE), F32), -sin, sin, padq], -1) * qs
    return jnp.concatenate([cm, sm, cq, sq], -1)


def _state_to_tiles(s):
    eye = jnp.eye(GLA_HEADS, dtype=s.dtype)
    st = jnp.swapaxes(s, 2, 3)
    bd = st[:, :, :, None, :] * eye[None, :, None, :, None]
    return bd.reshape(s.shape[0], GLA_V, GLA_K)


def _tiles_to_state(t):
    b = t.shape[0]
    t5 = t.reshape(b, GLA_HEADS, GLA_DV, GLA_HEADS, GLA_DK)
    idx = jnp.arange(GLA_HEADS)
    diag = t5[:, idx, :, idx, :]
    return jnp.transpose(diag, (1, 0, 3, 2))


def _tile(n, pref):
    return pref if n % pref == 0 else n


def kernel(x_prompt, x_sample, mem_prompt, cache_pool, cache_mla_ckv, cache_mla_krope, state_gla,
           cache_mem_k, cache_mem_v, ln_g, ln_b, w_ffn1_in, w_ffn1_out, w_in, pool_w, pool_scale,
           mla_q_norm, mla_kv_norm, mla_w_uq, mla_w_uk, mla_w_uv, gla_w_a2, gla_b_a, gla_norm,
           w_branch_pool, w_branch_mla, w_branch_gla, w_out, mem_w_q, mem_w_k, mem_w_v, mem_w_o,
           w_ffn2_in, w_ffn2_out):
    nl = w_in.shape[0]
    bp, sp, _ = x_prompt.shape
    bs, ss, _ = x_sample.shape
    past = cache_mla_ckv.shape[2]
    n_mem = mem_prompt.shape[1]

    w_a, w_gates, w_q2, w_k, w_kt, w_v, w_abs, w_uvp, pool_bd, w_a2 = _pack_weights(
        w_in, pool_w, mla_w_uq, mla_w_uk, mla_w_uv, gla_w_a2)
    bf = lambda w: w.astype(BF16)
    ffn_w = ((bf(w_ffn1_in), bf(w_ffn1_out)), (bf(w_ffn2_in), bf(w_ffn2_out)))
    w_bp, w_bm, w_bg, w_o = bf(w_branch_pool), bf(w_branch_mla), bf(w_branch_gla), bf(w_out)
    mw_q, mw_k, mw_v, mw_o = bf(mem_w_q), bf(mem_w_k), bf(mem_w_v), bf(mem_w_o)
    lng = ln_g.reshape(nl, 4, 1, D_MODEL)
    lnb = ln_b.reshape(nl, 4, 1, D_MODEL)
    qnorm = mla_q_norm.reshape(nl, 1, MLA_Q_RANK)
    kvnorm = mla_kv_norm.reshape(nl, 1, MLA_KV_RANK)
    pscale = pool_scale.reshape(nl, 1, POOL_WIDTH)
    b_a = gla_b_a.reshape(nl, 1, GLA_K)
    norm_g = jnp.tile(gla_norm, (1, GLA_HEADS)).reshape(nl, 1, GLA_V)
    hsel = (jnp.arange(GLA_K)[:, None] // GLA_DK == jnp.arange(GLA_V)[None, :] // GLA_DV).astype(BF16)
    hblk = (jnp.arange(GLA_V)[:, None] // GLA_DV == jnp.arange(GLA_V)[None, :] // GLA_DV).astype(BF16)

    tmem = _tile(bp * n_mem, 512)
    mem_k_out, mem_v_out, mem_k, mem_v = _memkv(
        mem_prompt.reshape(bp * n_mem, D_MODEL), mw_k, mw_v, tmem)
    mem_k = mem_k.reshape(nl, bp, n_mem, D_MODEL)
    mem_v = mem_v.reshape(nl, bp, n_mem, D_MODEL)
    smem_k = cache_mem_k.reshape(nl, bs, n_mem, D_MODEL)
    smem_v = cache_mem_v.reshape(nl, bs, n_mem, D_MODEL)

    groups = {
        "p": dict(nb=bp, seq=sp, pos0=0, tm=_tile(sp, 1024)),
        "s": dict(nb=bs, seq=ss, pos0=past, tm=_tile(bs * ss, 512)),
    }
    for gname, gr in groups.items():
        seq, tm = gr["seq"], gr["tm"]
        pos = gr["pos0"] + jnp.arange(seq)
        tab = _rope_table(pos)
        if tm <= seq:
            gr["tab"], gr["tab_period"] = tab, seq // tm
        else:
            gr["tab"], gr["tab_period"] = jnp.tile(tab, (tm // seq, 1)), 1
        gr["tseq"] = _tile(seq, 512)
        gr["tgla"] = _tile(seq, 1024)
        gr["tpool"] = _tile(seq, 2048)
        gr["ttok"] = _tile(gr["nb"] * seq, 1024)
        gr["tmem"] = _tile(seq, 1024)

    xs = {"p": x_prompt.reshape(bp * sp, D_MODEL), "s": x_sample.reshape(bs * ss, D_MODEL)}
    zero_hist = jnp.zeros((bp, POOL_PAD, POOL_WIDTH), F32)
    zero_state = jnp.zeros((bp, GLA_V, GLA_K), F32)
    outs = {n: {"pool": [], "ckv": [], "kr": [], "gla": []} for n in groups}

    for l in range(nl):
        for gname, gr in groups.items():
            nb, seq, tm = gr["nb"], gr["seq"], gr["tm"]
            x = xs[gname]
            x = _ffn(x, ffn_w[0][0], ffn_w[0][1], lng, lnb, l, 0, gr["ttok"])
            k_tile = gr["tseq"] if gname == "p" else None
            u_pool, ckv, misc, q, k, v, gla_in = _inproj(
                x, gr["tab"], w_a, w_q2, w_k if k_tile is None else w_kt, w_v, qnorm, kvnorm,
                l, tm, gr["tab_period"], k_tile)
            if gname == "p":
                hist = None
                hist16 = zero_hist
                s0t = zero_state
            else:
                hist = cache_pool[l]
                hist16 = jnp.concatenate(
                    [jnp.zeros((nb, POOL_PAD - POOL_HIST, POOL_WIDTH), F32), hist], 1)
                s0t = _state_to_tiles(state_gla[l])
            y_pool = _pool(u_pool, hist16, pool_bd, pscale, l, nb, seq, gr["tpool"], gr["pos0"])
            if gname == "p":
                y_mla = _mla_causal(q, k, v, nb, seq, gr["tseq"])
            else:
                y_mla = _mla_cached(q, ckv, misc, cache_mla_ckv, cache_mla_krope, w_abs, w_uvp,
                                    l, nb, seq, past)
            y_gla, s_fin = _gla(gla_in, misc, w_a2, b_a, norm_g, hsel, hblk, s0t, l, nb, seq, gr["tgla"])
            x = _merge(x, y_pool, y_mla, y_gla, w_gates, w_bp, w_bm, w_bg, w_o, lng, lnb, l,
                       gr["ttok"])
            if gname == "p":
                x = _memattn(x, mem_k, mem_v, l, mw_q, mw_o, lng, lnb, l, nb, seq, gr["tmem"])
            else:
                x = _memattn(x, smem_k, smem_v, l, mw_q, mw_o, lng, lnb, l, nb, seq, gr["tmem"])
            x = _ffn(x, ffn_w[1][0], ffn_w[1][1], lng, lnb, l, 3, gr["ttok"])
            xs[gname] = x

            u3 = u_pool.reshape(nb, seq, POOL_WIDTH)
            if hist is None:
                hist = jnp.zeros((nb, POOL_HIST, POOL_WIDTH), F32)
            new_pool = jnp.concatenate([hist, u3], 1)[:, -POOL_HIST:]
            o = outs[gname]
            o["pool"].append(new_pool)
            o["ckv"].append(ckv.reshape(nb, seq, MLA_KV_RANK))
            o["kr"].append(misc[:, :MLA_ROPE].reshape(nb, seq, MLA_ROPE))
            o["gla"].append(_tiles_to_state(s_fin))

    st = lambda n, key: jnp.stack(outs[n][key], 0)
    return (xs["p"].reshape(bp, sp, D_MODEL), xs["s"].reshape(bs, ss, D_MODEL),
            st("p", "pool"), st("s", "pool"),
            st("p", "ckv"), st("s", "ckv"),
            st("p", "kr"), st("s", "kr"),
            st("p", "gla"), st("s", "gla"),
            mem_k_out.reshape(nl, bp, n_mem, MEM_HEADS, MEM_HEAD_DIM),
            mem_v_out.reshape(nl, bp, n_mem, MEM_HEADS, MEM_HEAD_DIM))
```

```python
import functools
import math

import jax
import jax.numpy as jnp
from jax import lax
from jax.experimental import pallas as pl
from jax.experimental.pallas import tpu as pltpu

F32 = jnp.float32
BF16 = jnp.bfloat16

D_MODEL = 1024
DEPTH = 4
CHUNK = 64
EPS = 1e-5
ALPHA = (2 * DEPTH) ** 0.25

POOL_WINDOWS = (2, 4, 8, 16)
POOL_GROUP = D_MODEL // 16
POOL_WIDTH = len(POOL_WINDOWS) * POOL_GROUP
POOL_HIST = max(POOL_WINDOWS) - 1

MLA_HEADS = 8
MLA_Q_RANK = D_MODEL // 4
MLA_KV_RANK = D_MODEL // 4
MLA_NOPE = 64
MLA_ROPE = 32
MLA_V = 64
MLA_SCALE = (MLA_NOPE + MLA_ROPE) ** -0.5
ROPE_BASE = 10000.0
MLA_HEAD_PAD = 128
MLA_PAIR = 2
MLA_GROUP = 4
V_ONE_EVEN = MLA_V
V_ONE_ODD = MLA_HEAD_PAD
LOG2E = math.log2(math.e)

GLA_HEADS = 4
GLA_DK = 32
GLA_DV = 64
GLA_GATE_RANK = 16
GLA_GATE_TAU = 16.0
GLA_K = GLA_HEADS * GLA_DK
GLA_V = GLA_HEADS * GLA_DV

MEM_HEADS = 4
MEM_HEAD_DIM = D_MODEL // MEM_HEADS
D_FF = 2816

V7X_LANES = 128
V7X_SUBLANES = 8
V7X_VMEM_BYTES = 64 * 1024 * 1024
VMEM_LIMIT = V7X_VMEM_BYTES - 8 * 1024 * 1024

A_POOL = 0
A_QLAT = 256
A_CKV = 512
A_M1 = 768
A_M2 = 896
A_GLA = 1024
A_COLS = 1792
MISC_W = 128
GLA_IN_W = 2 * GLA_K + 2 * GLA_V


def _cparams(sem):
    return pltpu.CompilerParams(dimension_semantics=sem, vmem_limit_bytes=VMEM_LIMIT)


def _resident(shape, index_map):
    return pl.BlockSpec(shape, index_map, pipeline_mode=pl.Buffered(1))


def _layer_norm(y, g, b):
    mu = jnp.mean(y, axis=-1, keepdims=True)
    d = y - mu
    var = jnp.mean(d * d, axis=-1, keepdims=True)
    return d * lax.rsqrt(var + EPS) * g + b


def _rms_norm(y, g):
    return y * lax.rsqrt(jnp.mean(y * y, axis=-1, keepdims=True) + EPS) * g


def _dot(a, b):
    return jnp.dot(a, b, preferred_element_type=F32)


def _dot_nt(a, b):
    return lax.dot_general(a, b, (((1,), (1,)), ((), ())), preferred_element_type=F32)


def _silu(x):
    return x * jax.nn.sigmoid(x)


ROW_SUB = 256


def _row_blocks(rows):
    sub = min(ROW_SUB, rows)
    return [slice(r, r + sub) for r in range(0, rows, sub)]


FFN_CHUNKS = ((0, 1024), (1024, 1024), (2048, 768))


def _ffn_kernel(x_ref, win_ref, wout_ref, g_ref, b_ref, o_ref, act_ref):
    x = x_ref[...]
    xb = x.astype(BF16)
    for c0, cw in FFN_CHUNKS:
        gate = _dot(xb, win_ref[:, c0:c0 + cw])
        up = _dot(xb, win_ref[:, D_FF + c0:D_FF + c0 + cw])
        act_ref[:, c0:c0 + cw] = (_silu(gate) * up).astype(BF16)
    for rs in _row_blocks(x_ref.shape[0]):
        y = _dot(act_ref[rs, :], wout_ref[...])
        o_ref[rs, :] = _layer_norm(ALPHA * x_ref[rs, :] + 0.5 * y, g_ref[...], b_ref[...])


def _ffn(x, w_in, w_out, ln_g, ln_b, layer, which, tm):
    t = x.shape[0]
    return pl.pallas_call(
        _ffn_kernel,
        grid=(t // tm,),
        in_specs=[
            pl.BlockSpec((tm, D_MODEL), lambda i: (i, 0)),
            _resident((None, D_MODEL, 2 * D_FF), lambda i: (layer, 0, 0)),
            _resident((None, D_FF, D_MODEL), lambda i: (layer, 0, 0)),
            _resident((None, None, 1, D_MODEL), lambda i: (layer, which, 0, 0)),
            _resident((None, None, 1, D_MODEL), lambda i: (layer, which, 0, 0)),
        ],
        out_specs=pl.BlockSpec((tm, D_MODEL), lambda i: (i, 0)),
        out_shape=jax.ShapeDtypeStruct((t, D_MODEL), F32),
        scratch_shapes=[pltpu.VMEM((tm, D_FF), BF16)],
        compiler_params=_cparams(("parallel",)),
        name="ffn",
    )(x, w_in, w_out, ln_g, ln_b)


def _v_with_ones(vv):
    lane = lax.broadcasted_iota(jnp.int32, vv.shape, 1) % (MLA_PAIR * MLA_HEAD_PAD)
    return jnp.where((lane == V_ONE_EVEN) | (lane == V_ONE_ODD), 1.0, vv).astype(BF16)


def _inproj_kernel(x_ref, tab_ref, wa_ref, wq_ref, wk_ref, wv_ref, qn_ref, kvn_ref,
                   pool_ref, ckv_ref, misc_ref, q_ref, k_ref, v_ref, gla_ref, *, k_transposed):
    xb = x_ref[...].astype(BF16)
    h = _dot(xb, wa_ref[:, 0:A_GLA])
    gla_ref[...] = _dot(xb, wa_ref[:, A_GLA:A_COLS])
    pool_ref[...] = h[:, A_POOL:A_POOL + POOL_WIDTH]

    cm = tab_ref[:, 0:128]
    sm = tab_ref[:, 128:256]
    misc = h[:, A_M1:A_M1 + MISC_W] * cm + h[:, A_M2:A_M2 + MISC_W] * sm
    misc_ref[...] = misc

    ckv = _rms_norm(h[:, A_CKV:A_CKV + MLA_KV_RANK], kvn_ref[...])
    ckv_ref[...] = ckv
    ckvb = ckv.astype(BF16)
    lat = jnp.concatenate([ckvb, misc.astype(BF16)], axis=-1)
    if k_transposed:
        kt = _dot_nt(wk_ref[...], lat).astype(BF16)
        k_tile = k_ref.shape[-1]
        for c in range(k_ref.shape[0]):
            k_ref[c] = kt[:, c * k_tile:(c + 1) * k_tile]
    else:
        k_ref[...] = _dot(lat, wk_ref[...]).astype(BF16)
    v_ref[...] = _v_with_ones(_dot(ckvb, wv_ref[...]))

    qn = _rms_norm(h[:, A_QLAT:A_QLAT + MLA_Q_RANK], qn_ref[...]).astype(BF16)
    q2 = _dot(qn, wq_ref[...])
    kw = MLA_HEADS * MLA_HEAD_PAD
    cq = tab_ref[:, 256:384]
    sq = tab_ref[:, 384:512]
    for hd in range(MLA_HEADS):
        a = hd * MLA_HEAD_PAD
        q_ref[:, a:a + MLA_HEAD_PAD] = (
            q2[:, a:a + MLA_HEAD_PAD] * cq + q2[:, kw + a:kw + a + MLA_HEAD_PAD] * sq).astype(BF16)


def _inproj(x, tab, w_a, w_q2, w_k, w_v, qnorm, kvnorm, layer, tm, tab_period, k_tile):
    t = x.shape[0]
    kw = MLA_HEADS * MLA_HEAD_PAD
    lat_w = MLA_KV_RANK + MISC_W
    row = lambda i: (i, 0)
    k_transposed = k_tile is not None
    if k_transposed:
        k_shape = jax.ShapeDtypeStruct((t // k_tile, kw, k_tile), BF16)
        k_spec = pl.BlockSpec((tm // k_tile, kw, k_tile), lambda i: (i, 0, 0))
        wk_spec = _resident((None, kw, lat_w), lambda i: (layer, 0, 0))
    else:
        k_shape = jax.ShapeDtypeStruct((t, kw), BF16)
        k_spec = pl.BlockSpec((tm, kw), row)
        wk_spec = _resident((None, lat_w, kw), lambda i: (layer, 0, 0))
    out_shapes = (
        jax.ShapeDtypeStruct((t, POOL_WIDTH), F32),
        jax.ShapeDtypeStruct((t, MLA_KV_RANK), F32),
        jax.ShapeDtypeStruct((t, MISC_W), F32),
        jax.ShapeDtypeStruct((t, kw), BF16),
        k_shape,
        jax.ShapeDtypeStruct((t, kw), BF16),
        jax.ShapeDtypeStruct((t, GLA_IN_W), F32),
    )
    out_specs = [pl.BlockSpec((tm, s.shape[-1]), row) for s in out_shapes]
    out_specs[4] = k_spec
    return pl.pallas_call(
        functools.partial(_inproj_kernel, k_transposed=k_transposed),
        grid=(t // tm,),
        in_specs=[
            pl.BlockSpec((tm, D_MODEL), row),
            pl.BlockSpec((tm, 512), lambda i: (i % tab_period, 0)),
            _resident((None, D_MODEL, A_COLS), lambda i: (layer, 0, 0)),
            _resident((None, MLA_Q_RANK, 2 * kw), lambda i: (layer, 0, 0)),
            wk_spec,
            _resident((None, MLA_KV_RANK, kw), lambda i: (layer, 0, 0)),
            _resident((None, 1, MLA_Q_RANK), lambda i: (layer, 0, 0)),
            _resident((None, 1, MLA_KV_RANK), lambda i: (layer, 0, 0)),
        ],
        out_specs=out_specs,
        out_shape=out_shapes,
        compiler_params=_cparams(("parallel",)),
        name="inproj",
    )(x, tab, w_a, w_q2, w_k, w_v, qnorm, kvnorm)


POOL_PAD = 16


def _pool_kernel(u_ref, hist_ref, w_ref, scale_ref, y_ref, ext_ref, *, tp, pos0):
    j = pl.program_id(1)

    @pl.when(j == 0)
    def _():
        ext_ref[0:POOL_PAD, :] = hist_ref[...]

    tok = u_ref[...]
    ext_ref[POOL_PAD:POOL_PAD + tp, :] = tok
    acc = ext_ref[...]
    sums = {}
    w = 1
    while w < max(POOL_WINDOWS):
        acc = acc + pltpu.roll(acc, w, 0)
        w *= 2
        if w in POOL_WINDOWS:
            sums[w] = acc[POOL_PAD:POOL_PAD + tp, :]
    lane = lax.broadcasted_iota(jnp.int32, (tp, POOL_WIDTH), 1)
    pos = pos0 + j * tp + lax.broadcasted_iota(jnp.int32, (tp, POOL_WIDTH), 0)
    total = None
    width = None
    for gi, w in enumerate(POOL_WINDOWS):
        in_group = lane >= gi * POOL_GROUP
        total = sums[w] if total is None else jnp.where(in_group, sums[w], total)
        width = jnp.full_like(pos, w) if width is None else jnp.where(in_group, w, width)
    cnt = jnp.minimum(pos + 1, width).astype(F32)
    pooled = total / cnt - tok
    y = _dot(pooled.astype(BF16), w_ref[...]) * scale_ref[...]
    y_ref[...] = y.astype(BF16)
    ext_ref[0:POOL_PAD, :] = ext_ref[tp:tp + POOL_PAD, :]


def _pool(u, hist16, w_bd, scale, layer, nb, seq, tp, pos0):
    nt = seq // tp
    return pl.pallas_call(
        functools.partial(_pool_kernel, tp=tp, pos0=pos0),
        grid=(nb, nt),
        in_specs=[
            pl.BlockSpec((tp, POOL_WIDTH), lambda b, j: (b * nt + j, 0)),
            pl.BlockSpec((None, POOL_PAD, POOL_WIDTH), lambda b, j: (b, 0, 0)),
            _resident((None, POOL_WIDTH, POOL_WIDTH), lambda b, j: (layer, 0, 0)),
            _resident((None, 1, POOL_WIDTH), lambda b, j: (layer, 0, 0)),
        ],
        out_specs=pl.BlockSpec((tp, POOL_WIDTH), lambda b, j: (b * nt + j, 0)),
        out_shape=jax.ShapeDtypeStruct((nb * seq, POOL_WIDTH), BF16),
        scratch_shapes=[pltpu.VMEM((POOL_PAD + tp, POOL_WIDTH), F32)],
        compiler_params=_cparams(("parallel", "arbitrary")),
        name="pool",
    )(u, hist16, w_bd, scale)


def _pair_output(acc0, acc1):
    lane = lax.broadcasted_iota(jnp.int32, acc0.shape, 1)
    o0 = acc0 / acc0[:, V_ONE_EVEN:V_ONE_EVEN + 1]
    o1 = acc1 / acc1[:, 0:1]
    return jnp.where(lane < MLA_V, o0, o1).astype(BF16)


def _mla_causal_kernel(q_ref, kt_ref, v_ref, o_ref, m_ref, acc_ref, sa_ref, sb_ref, *, tq):
    i = pl.program_id(2)
    m_ref[...] = jnp.full(m_ref.shape, -jnp.inf, F32)
    acc_ref[...] = jnp.zeros(acc_ref.shape, F32)
    reps = tq // V7X_LANES

    def scores(kt, dst_ref):
        for h in range(MLA_GROUP):
            a = h * MLA_HEAD_PAD
            dst_ref[h] = _dot(q_ref[:, a:a + MLA_HEAD_PAD], kt_ref[kt, a:a + MLA_HEAD_PAD, :])

    def consume(src_ref, kt, masked):
        r0 = pl.multiple_of(kt * tq, tq)
        for h in range(MLA_GROUP):
            a = h * MLA_HEAD_PAD
            s = src_ref[h]
            if masked:
                qc = lax.broadcasted_iota(jnp.int32, (tq, tq), 0) // CHUNK
                kc = lax.broadcasted_iota(jnp.int32, (tq, tq), 1) // CHUNK
                s = jnp.where(kc <= qc, s, -jnp.inf)
            m_prev = m_ref[h]
            m_new = jnp.maximum(m_prev, jnp.max(s, axis=-1, keepdims=True))
            p = jnp.exp2(s - jnp.tile(m_new, (1, reps)))
            corr = jnp.exp2(m_prev - m_new)
            acc_ref[h] = corr * acc_ref[h] + _dot(p.astype(BF16),
                                                  v_ref[pl.ds(r0, tq), a:a + MLA_HEAD_PAD])
            m_ref[h] = m_new

    scores(0, sa_ref)

    def body(j, carry):
        scores(2 * j + 1, sb_ref)
        consume(sa_ref, 2 * j, False)
        scores(2 * j + 2, sa_ref)
        consume(sb_ref, 2 * j + 1, False)
        return carry

    lax.fori_loop(0, i // 2, body, 0)

    def finish():
        ow = MLA_PAIR * MLA_V
        for pr in range(MLA_GROUP // MLA_PAIR):
            o_ref[:, pr * ow:(pr + 1) * ow] = _pair_output(acc_ref[MLA_PAIR * pr],
                                                           acc_ref[MLA_PAIR * pr + 1])

    @pl.when(i % 2 == 0)
    def _():
        consume(sa_ref, i, True)
        finish()

    @pl.when(i % 2 == 1)
    def _():
        scores(i, sb_ref)
        consume(sa_ref, i - 1, False)
        consume(sb_ref, i, True)
        finish()


def _mla_causal(q, kt, v, nb, seq, tq):
    nq = seq // tq
    ngrp = MLA_HEADS // MLA_GROUP
    pw = MLA_GROUP * MLA_HEAD_PAD
    return pl.pallas_call(
        functools.partial(_mla_causal_kernel, tq=tq),
        grid=(nb, ngrp, nq),
        in_specs=[
            pl.BlockSpec((tq, pw), lambda b, p, i: (b * nq + i, p)),
            pl.BlockSpec((nq, pw, tq), lambda b, p, i: (b, p, 0)),
            pl.BlockSpec((seq, pw), lambda b, p, i: (b, p)),
        ],
        out_specs=pl.BlockSpec((tq, MLA_GROUP * MLA_V), lambda b, p, i: (b * nq + i, p)),
        out_shape=jax.ShapeDtypeStruct((nb * seq, MLA_HEADS * MLA_V), BF16),
        scratch_shapes=[pltpu.VMEM((MLA_GROUP, tq, V7X_LANES), F32),
                        pltpu.VMEM((MLA_GROUP, tq, MLA_HEAD_PAD), F32),
                        pltpu.VMEM((MLA_GROUP, tq, tq), F32),
                        pltpu.VMEM((MLA_GROUP, tq, tq), F32)],
        compiler_params=_cparams(("parallel", "parallel", "arbitrary")),
        name="mla_causal",
    )(q, kt, v)


def _mla_cached_kernel(q_ref, cp_ref, rp_ref, cn_ref, mn_ref, wabs_ref, wuv_ref, o_ref):
    qa = jnp.concatenate(
        [_dot(q_ref[:, h * MLA_HEAD_PAD:(h + 1) * MLA_HEAD_PAD], wabs_ref[h]) for h in range(MLA_HEADS)],
        axis=0).astype(BF16)
    qa_lat = qa[:, 0:MLA_KV_RANK]
    qa_misc = qa[:, MLA_KV_RANK:]
    cp = cp_ref[...].astype(BF16)
    cn = cn_ref[...].astype(BF16)
    s1 = _dot_nt(qa_lat, cp) + _dot_nt(qa_misc[:, 0:MLA_ROPE], rp_ref[...].astype(BF16))
    s2 = _dot_nt(qa_lat, cn) + _dot_nt(qa_misc, mn_ref[...].astype(BF16))
    m = jnp.maximum(jnp.max(s1, axis=-1, keepdims=True), jnp.max(s2, axis=-1, keepdims=True))
    p1 = jnp.exp2(s1 - m)
    p2 = jnp.exp2(s2 - m)
    l = jnp.sum(p1, axis=-1, keepdims=True) + jnp.sum(p2, axis=-1, keepdims=True)
    lat = ((_dot(p1.astype(BF16), cp) + _dot(p2.astype(BF16), cn)) / l).astype(BF16)
    seq = q_ref.shape[0]
    y = _dot(lat[0:seq], wuv_ref[0])
    for h in range(1, MLA_HEADS):
        y = y + _dot(lat[h * seq:(h + 1) * seq], wuv_ref[h])
    o_ref[...] = y.astype(BF16)


def _mla_cached(q, ckv_new, misc_new, ckv_past, kr_past, w_abs, w_uvp, layer, nb, seq, past):
    kw = MLA_HEADS * MLA_HEAD_PAD
    vw = MLA_HEADS * MLA_V
    lat_w = MLA_KV_RANK + MISC_W
    return pl.pallas_call(
        _mla_cached_kernel,
        grid=(nb,),
        in_specs=[
            pl.BlockSpec((seq, kw), lambda b: (b, 0)),
            pl.BlockSpec((None, None, past, MLA_KV_RANK), lambda b: (layer, b, 0, 0)),
            pl.BlockSpec((None, None, past, MLA_ROPE), lambda b: (layer, b, 0, 0)),
            pl.BlockSpec((seq, MLA_KV_RANK), lambda b: (b, 0)),
            pl.BlockSpec((seq, MISC_W), lambda b: (b, 0)),
            _resident((None, MLA_HEADS, MLA_HEAD_PAD, lat_w), lambda b: (layer, 0, 0, 0)),
            _resident((None, MLA_HEADS, MLA_KV_RANK, vw), lambda b: (layer, 0, 0, 0)),
        ],
        out_specs=pl.BlockSpec((seq, vw), lambda b: (b, 0)),
        out_shape=jax.ShapeDtypeStruct((nb * seq, vw), BF16),
        compiler_params=_cparams(("parallel",)),
        name="mla_cached",
    )(q, ckv_past, kr_past, ckv_new, misc_new, w_abs, w_uvp)


def _gla_kernel(gq_ref, gk_ref, gv_ref, gr_ref, misc_ref, wa2_ref, ba_ref, ng_ref, hsel_ref,
                hblk_ref, s0_ref, y_ref, sfin_ref, st_ref, sec_ref, fst_ref, same_ref, kbd_ref,
                vbd_ref, sbd_ref, o_ref, *, tt):
    j = pl.program_id(1)
    c = min(CHUNK, tt)
    hc = GLA_HEADS * c
    levels = c.bit_length() - 1

    @pl.when(j == 0)
    def _():
        st_ref[...] = s0_ref[...]
        row = lax.broadcasted_iota(jnp.int32, (c, GLA_K), 0)
        trow = lax.broadcasted_iota(jnp.int32, (c, hc), 0)
        tcol = lax.broadcasted_iota(jnp.int32, (c, hc), 1) % c
        for lv in range(levels):
            m = 1 << lv
            sec_ref[lv] = ((row & m) != 0).astype(F32)
            fst_ref[lv] = ((row & m) == 0).astype(F32)
            same_ref[lv] = ((trow // (2 * m)) == (tcol // (2 * m))).astype(F32)
        kbd_ref[...] = (lax.broadcasted_iota(jnp.int32, (hc, GLA_K), 0) // c
                        == lax.broadcasted_iota(jnp.int32, (hc, GLA_K), 1) // GLA_DK).astype(BF16)
        vbd_ref[...] = (lax.broadcasted_iota(jnp.int32, (hc, GLA_V), 0) // c
                        == lax.broadcasted_iota(jnp.int32, (hc, GLA_V), 1) // GLA_DV).astype(BF16)
        sbd_ref[...] = (lax.broadcasted_iota(jnp.int32, (GLA_V, GLA_K), 0) // GLA_DV
                        == lax.broadcasted_iota(jnp.int32, (GLA_V, GLA_K), 1) // GLA_DK).astype(F32)

    z = _dot(misc_ref[...].astype(BF16), wa2_ref[...]) + ba_ref[...]
    la_all = jax.nn.log_sigmoid(z) * (LOG2E / GLA_GATE_TAU)

    st = st_ref[...]
    for r in range(tt // c):
        rs = slice(r * c, (r + 1) * c)
        q = gq_ref[rs, :] * (GLA_DK ** -0.5)
        k = gk_ref[rs, :]
        v = gv_ref[rs, :]
        la = la_all[rs, :]

        scores = jnp.zeros((c, hc), F32)
        p = la
        g = la
        for lv in range(levels):
            m = 1 << lv
            if m < V7X_SUBLANES:
                sec = sec_ref[lv]
                fst = fst_ref[lv]
                qe = (q * jnp.exp2(p) * sec).astype(BF16)
                ke = (k * jnp.exp2(g - p) * fst).astype(BF16)
                ke_bd = jnp.tile(ke, (GLA_HEADS, 1)) * kbd_ref[...]
                scores = scores + _dot_nt(qe, ke_bd) * same_ref[lv]
                up = sec * pltpu.roll(g, m, 0)
                p = p + up
                g = g + up + fst * pltpu.roll(g, c - m, 0)
            else:
                nb2 = c // (2 * m)

                def halves(a):
                    a4 = a.reshape(nb2, 2, m, a.shape[-1])
                    return a4[:, 0], a4[:, 1]

                def join(lo, hi):
                    return jnp.stack([lo, hi], axis=1).reshape(c, lo.shape[-1])

                p_lo, p_hi = halves(p)
                g_lo, g_hi = halves(g)
                qe_hi = (halves(q)[1] * jnp.exp2(p_hi)).reshape(c // 2, GLA_K).astype(BF16)
                ke_lo = halves(k)[0] * jnp.exp2(g_lo - p_lo)
                ke = join(ke_lo, jnp.zeros_like(ke_lo)).astype(BF16)
                ke_bd = jnp.tile(ke, (GLA_HEADS, 1)) * kbd_ref[...]
                a_hi = _dot_nt(qe_hi, ke_bd) * halves(same_ref[lv])[1].reshape(c // 2, hc)
                s_lo, s_hi = halves(scores)
                scores = join(s_lo, s_hi + a_hi.reshape(nb2, m, hc))
                tot = g_lo + g_hi
                p = join(p_lo, p_hi + g_lo)
                g = join(tot, tot)

        o = _dot_nt((q * jnp.exp2(p)).astype(BF16), st.astype(BF16))
        v_bd = jnp.tile(v.astype(BF16), (GLA_HEADS, 1)) * vbd_ref[...]
        o_ref[rs, :] = o + _dot(scores.astype(BF16), v_bd)

        kdec = (k * jnp.exp2(g - p)).astype(BF16)
        upd = _dot(jnp.transpose(v).astype(BF16), kdec)
        st = st * jnp.exp2(g[0:1, :]) + upd * sbd_ref[...]

    st_ref[...] = st
    sfin_ref[...] = st

    qk = (gq_ref[...] * (GLA_DK ** -0.5) * gk_ref[...]).astype(BF16)
    o = o_ref[...] + _dot(qk, hsel_ref[...]) * gv_ref[...]
    o2 = o * o
    o2_hi = o2.astype(BF16)
    o2_lo = (o2 - o2_hi.astype(F32)).astype(BF16)
    ms = (_dot(o2_hi, hblk_ref[...]) + _dot(o2_lo, hblk_ref[...])) * (1.0 / GLA_DV)
    y = o * lax.rsqrt(ms + EPS) * ng_ref[...] * _silu(gr_ref[...])
    y_ref[...] = y.astype(BF16)


def _gla(gla_in, misc, w_a2, b_a, norm_g, hsel, hblk, s0t, layer, nb, seq, tt):
    nt = seq // tt
    row = lambda b, j: (b * nt + j, 0)
    c = min(CHUNK, tt)
    hc = GLA_HEADS * c
    levels = c.bit_length() - 1
    return pl.pallas_call(
        functools.partial(_gla_kernel, tt=tt),
        grid=(nb, nt),
        in_specs=[
            pl.BlockSpec((tt, GLA_K), lambda b, j: (b * nt + j, 0)),
            pl.BlockSpec((tt, GLA_K), lambda b, j: (b * nt + j, 1)),
            pl.BlockSpec((tt, GLA_V), lambda b, j: (b * nt + j, 1)),
            pl.BlockSpec((tt, GLA_V), lambda b, j: (b * nt + j, 2)),
            pl.BlockSpec((tt, MISC_W), row),
            _resident((None, MISC_W, GLA_K), lambda b, j: (layer, 0, 0)),
            _resident((None, 1, GLA_K), lambda b, j: (layer, 0, 0)),
            _resident((None, 1, GLA_V), lambda b, j: (layer, 0, 0)),
            _resident((GLA_K, GLA_V), lambda b, j: (0, 0)),
            _resident((GLA_V, GLA_V), lambda b, j: (0, 0)),
            pl.BlockSpec((None, GLA_V, GLA_K), lambda b, j: (b, 0, 0)),
        ],
        out_specs=[pl.BlockSpec((tt, GLA_V), row),
                   pl.BlockSpec((None, GLA_V, GLA_K), lambda b, j: (b, 0, 0))],
        out_shape=(jax.ShapeDtypeStruct((nb * seq, GLA_V), BF16),
                   jax.ShapeDtypeStruct((nb, GLA_V, GLA_K), F32)),
        scratch_shapes=[pltpu.VMEM((GLA_V, GLA_K), F32),
                        pltpu.VMEM((levels, c, GLA_K), F32),
                        pltpu.VMEM((levels, c, GLA_K), F32),
                        pltpu.VMEM((levels, c, hc), F32),
                        pltpu.VMEM((hc, GLA_K), BF16),
                        pltpu.VMEM((hc, GLA_V), BF16),
                        pltpu.VMEM((GLA_V, GLA_K), F32),
                        pltpu.VMEM((tt, GLA_V), F32)],
        compiler_params=_cparams(("parallel", "arbitrary")),
        name="gla",
    )(gla_in, gla_in, gla_in, gla_in, misc, w_a2, b_a, norm_g, hsel, hblk, s0t)


def _merge_kernel(x_ref, yp_ref, ym_ref, yg_ref, wg_ref, wbp_ref, wbm_ref, wbg_ref, wo_ref,
                  g_ref, b_ref, o_ref):
    x = x_ref[...]
    xb = x.astype(BF16)
    m = jax.nn.sigmoid(_dot(xb, wg_ref[:, 0:D_MODEL])) * _dot(yp_ref[...], wbp_ref[...])
    m = m + jax.nn.sigmoid(_dot(xb, wg_ref[:, D_MODEL:2 * D_MODEL])) * _dot(ym_ref[...], wbm_ref[...])
    m = m + jax.nn.sigmoid(_dot(xb, wg_ref[:, 2 * D_MODEL:3 * D_MODEL])) * _dot(yg_ref[...], wbg_ref[...])
    mb = m.astype(BF16)
    for rs in _row_blocks(x_ref.shape[0]):
        y = _dot(mb[rs, :], wo_ref[...])
        o_ref[rs, :] = _layer_norm(ALPHA * x_ref[rs, :] + y, g_ref[...], b_ref[...])


def _merge(x, y_pool, y_mla, y_gla, w_gates, w_bp, w_bm, w_bg, w_out, ln_g, ln_b, layer, tm):
    t = x.shape[0]
    row = lambda i: (i, 0)
    lay = lambda i: (layer, 0, 0)
    return pl.pallas_call(
        _merge_kernel,
        grid=(t // tm,),
        in_specs=[
            pl.BlockSpec((tm, D_MODEL), row),
            pl.BlockSpec((tm, POOL_WIDTH), row),
            pl.BlockSpec((tm, MLA_HEADS * MLA_V), row),
            pl.BlockSpec((tm, GLA_V), row),
            _resident((None, D_MODEL, 3 * D_MODEL), lay),
            _resident((None, POOL_WIDTH, D_MODEL), lay),
            _resident((None, MLA_HEADS * MLA_V, D_MODEL), lay),
            _resident((None, GLA_V, D_MODEL), lay),
            _resident((None, D_MODEL, D_MODEL), lay),
            _resident((None, None, 1, D_MODEL), lambda i: (layer, 1, 0, 0)),
            _resident((None, None, 1, D_MODEL), lambda i: (layer, 1, 0, 0)),
        ],
        out_specs=pl.BlockSpec((tm, D_MODEL), row),
        out_shape=jax.ShapeDtypeStruct((t, D_MODEL), F32),
        compiler_params=_cparams(("parallel",)),
        name="merge",
    )(x, y_pool, y_mla, y_gla, w_gates, w_bp, w_bm, w_bg, w_out, ln_g, ln_b)


def _memkv_kernel(m_ref, wk_ref, wv_ref, k_ref, v_ref, kb_ref, vb_ref):
    mb = m_ref[...].astype(BF16)
    k = _dot(mb, wk_ref[...])
    v = _dot(mb, wv_ref[...])
    kb_ref[...] = k.astype(BF16)
    vb_ref[...] = v.astype(BF16)
    for h in range(MEM_HEADS):
        a = h * MEM_HEAD_DIM
        k_ref[:, h, :] = k[:, a:a + MEM_HEAD_DIM]
        v_ref[:, h, :] = v[:, a:a + MEM_HEAD_DIM]


def _memkv(mem, w_k, w_v, tm):
    rows = mem.shape[0]
    nl = w_k.shape[0]
    split = pl.BlockSpec((None, tm, MEM_HEADS, MEM_HEAD_DIM), lambda l, i: (l, i, 0, 0))
    flat = pl.BlockSpec((None, tm, D_MODEL), lambda l, i: (l, i, 0))
    return pl.pallas_call(
        _memkv_kernel,
        grid=(nl, rows // tm),
        in_specs=[
            pl.BlockSpec((tm, D_MODEL), lambda l, i: (i, 0)),
            pl.BlockSpec((None, D_MODEL, D_MODEL), lambda l, i: (l, 0, 0)),
            pl.BlockSpec((None, D_MODEL, D_MODEL), lambda l, i: (l, 0, 0)),
        ],
        out_specs=[split, split, flat, flat],
        out_shape=(jax.ShapeDtypeStruct((nl, rows, MEM_HEADS, MEM_HEAD_DIM), F32),
                   jax.ShapeDtypeStruct((nl, rows, MEM_HEADS, MEM_HEAD_DIM), F32),
                   jax.ShapeDtypeStruct((nl, rows, D_MODEL), BF16),
                   jax.ShapeDtypeStruct((nl, rows, D_MODEL), BF16)),
        compiler_params=_cparams(("parallel", "parallel")),
        name="memkv",
    )(mem, w_k, w_v)


def _memattn_kernel(x_ref, k_ref, v_ref, wq_ref, wo_ref, g_ref, b_ref, o_ref):
    x = x_ref[...]
    q = (_dot(x.astype(BF16), wq_ref[...]) * (MEM_HEAD_DIM ** -0.5 * LOG2E)).astype(BF16)
    kb = k_ref[...].astype(BF16)
    vb = v_ref[...].astype(BF16)
    outs = []
    for h in range(MEM_HEADS):
        a = h * MEM_HEAD_DIM
        s = _dot_nt(q[:, a:a + MEM_HEAD_DIM], kb[:, a:a + MEM_HEAD_DIM])
        p = jnp.exp2(s - jnp.max(s, axis=-1, keepdims=True))
        inv_l = 1.0 / jnp.sum(p, axis=-1, keepdims=True)
        outs.append(_dot(p.astype(BF16), vb[:, a:a + MEM_HEAD_DIM]) * inv_l)
    o = jnp.concatenate(outs, axis=-1).astype(BF16)
    for rs in _row_blocks(x_ref.shape[0]):
        y = _dot(o[rs, :], wo_ref[...])
        o_ref[rs, :] = _layer_norm(ALPHA * x_ref[rs, :] + y, g_ref[...], b_ref[...])


def _memattn(x, mem_k, mem_v, kv_layer, w_q, w_o, ln_g, ln_b, layer, nb, seq, tm):
    nt = seq // tm
    n_mem = mem_k.shape[2]
    lay = lambda b, i: (layer, 0, 0)
    return pl.pallas_call(
        _memattn_kernel,
        grid=(nb, nt),
        in_specs=[
            pl.BlockSpec((tm, D_MODEL), lambda b, i: (b * nt + i, 0)),
            pl.BlockSpec((None, None, n_mem, D_MODEL), lambda b, i: (kv_layer, b, 0, 0)),
            pl.BlockSpec((None, None, n_mem, D_MODEL), lambda b, i: (kv_layer, b, 0, 0)),
            _resident((None, D_MODEL, D_MODEL), lay),
            _resident((None, D_MODEL, D_MODEL), lay),
            _resident((None, None, 1, D_MODEL), lambda b, i: (layer, 2, 0, 0)),
            _resident((None, None, 1, D_MODEL), lambda b, i: (layer, 2, 0, 0)),
        ],
        out_specs=pl.BlockSpec((tm, D_MODEL), lambda b, i: (b * nt + i, 0)),
        out_shape=jax.ShapeDtypeStruct((nb * seq, D_MODEL), F32),
        compiler_params=_cparams(("parallel", "parallel")),
        name="memattn",
    )(x, mem_k, mem_v, w_q, w_o, ln_g, ln_b)


def _pack_weights(w_in, pool_w, mla_w_uq, mla_w_uk, mla_w_uv, gla_w_a2):
    nl = w_in.shape[0]
    splits = (POOL_WIDTH, MLA_Q_RANK, MLA_KV_RANK, MLA_ROPE, GLA_K, GLA_K, GLA_V,
              GLA_GATE_RANK, GLA_V, 3 * D_MODEL)
    offs = [0]
    for s in splits:
        offs.append(offs[-1] + s)
    col = lambda n: w_in[:, :, offs[n]:offs[n + 1]]
    half = MLA_ROPE // 2
    kr = col(3)
    kr_sw = jnp.concatenate([kr[..., half:], kr[..., :half]], -1)
    zeros = lambda n: jnp.zeros((nl, D_MODEL, n), w_in.dtype)
    w_a = jnp.concatenate(
        [col(0), col(1), col(2),
         kr, col(7), zeros(MISC_W - MLA_ROPE - GLA_GATE_RANK),
         kr_sw, zeros(MISC_W - MLA_ROPE),
         col(4), col(5), col(6), col(8)], -1).astype(BF16)
    w_gates = col(9).astype(BF16)

    uq = mla_w_uq.reshape(nl, MLA_Q_RANK, MLA_HEADS, MLA_NOPE + MLA_ROPE)
    uq_n, uq_r = uq[..., :MLA_NOPE], uq[..., MLA_NOPE:]
    uq_rs = jnp.concatenate([uq_r[..., half:], uq_r[..., :half]], -1)
    padq = jnp.zeros((nl, MLA_Q_RANK, MLA_HEADS, MLA_HEAD_PAD - MLA_NOPE - MLA_ROPE), uq.dtype)
    q_main = jnp.concatenate([uq_n, uq_r, padq], -1)
    q_swap = jnp.concatenate([jnp.zeros_like(uq_n), uq_rs, padq], -1)
    kw = MLA_HEADS * MLA_HEAD_PAD
    w_q2 = jnp.concatenate([q_main.reshape(nl, MLA_Q_RANK, kw),
                            q_swap.reshape(nl, MLA_Q_RANK, kw)], -1).astype(BF16)

    padk = jnp.zeros((nl, MLA_KV_RANK, MLA_HEADS, MLA_HEAD_PAD - MLA_NOPE), mla_w_uk.dtype)
    k_lat = jnp.concatenate([mla_w_uk, padk], -1).reshape(nl, MLA_KV_RANK, kw)
    place = jnp.zeros((MISC_W, MLA_HEADS, MLA_HEAD_PAD), F32)
    idx = jnp.arange(MLA_ROPE)
    place = place.at[idx, :, MLA_NOPE + idx].set(1.0).reshape(MISC_W, kw)
    k_misc = jnp.broadcast_to(place, (nl, MISC_W, kw))
    w_k = jnp.concatenate([k_lat, k_misc], 1).astype(BF16)
    w_kt = jnp.swapaxes(w_k, 1, 2)
    uv = mla_w_uv.reshape(nl, MLA_KV_RANK, MLA_HEADS // MLA_PAIR, MLA_PAIR, MLA_V)
    padv = jnp.zeros_like(uv[:, :, :, 0])
    w_v = jnp.stack([jnp.concatenate([uv[:, :, :, 0], padv], -1),
                     jnp.concatenate([padv, uv[:, :, :, 1]], -1)], 3)
    w_v = w_v.reshape(nl, MLA_KV_RANK, kw).astype(BF16)

    uk_t = jnp.transpose(mla_w_uk, (0, 2, 3, 1))
    a_lat = jnp.concatenate(
        [uk_t, jnp.zeros((nl, MLA_HEADS, MLA_HEAD_PAD - MLA_NOPE, MLA_KV_RANK), uk_t.dtype)], 2)
    a_misc = jnp.zeros((MLA_HEAD_PAD, MISC_W), F32).at[MLA_NOPE + idx, idx].set(1.0)
    a_misc = jnp.broadcast_to(a_misc, (nl, MLA_HEADS, MLA_HEAD_PAD, MISC_W))
    w_abs = jnp.concatenate([a_lat, a_misc], -1).astype(BF16)
    eye_h = jnp.eye(MLA_HEADS, dtype=mla_w_uv.dtype)
    uv_h = jnp.transpose(mla_w_uv, (0, 2, 1, 3))
    w_uvp = (uv_h[:, :, :, None, :] * eye_h[None, :, None, :, None]).reshape(
        nl, MLA_HEADS, MLA_KV_RANK, MLA_HEADS * MLA_V).astype(BF16)

    ng = len(POOL_WINDOWS)
    eye = jnp.eye(ng, dtype=pool_w.dtype)
    pool_bd = (pool_w[:, :, :, None, :] * eye[None, :, None, :, None]).reshape(
        nl, POOL_WIDTH, POOL_WIDTH).astype(BF16)

    w_a2 = jnp.zeros((nl, MISC_W, GLA_K), gla_w_a2.dtype)
    w_a2 = w_a2.at[:, MLA_ROPE:MLA_ROPE + GLA_GATE_RANK, :].set(gla_w_a2).astype(BF16)
    return w_a, w_gates, w_q2, w_k, w_kt, w_v, w_abs, w_uvp, pool_bd, w_a2


def _rope_table(pos):
    half = MLA_ROPE // 2
    inv = ROPE_BASE ** (-jnp.arange(half, dtype=F32) / half)
    ang = pos.astype(F32)[:, None] * inv
    cos, sin = jnp.cos(ang), jnp.sin(ang)
    n = pos.shape[0]
    one = jnp.ones((n, GLA_GATE_RANK), F32)
    cm = jnp.concatenate([cos, cos, one, jnp.zeros((n, MISC_W - MLA_ROPE - GLA_GATE_RANK), F32)], -1)
    sm = jnp.concatenate([-sin, sin, jnp.zeros((n, MISC_W - MLA_ROPE), F32)], -1)
    padq = jnp.zeros((n, MLA_HEAD_PAD - MLA_NOPE - MLA_ROPE), F32)
    qs = MLA_SCALE * LOG2E
    cq = jnp.concatenate([jnp.ones((n, MLA_NOPE), F32), cos, cos, padq], -1) * qs
    sq = jnp.concatenate([jnp.zeros((n, MLA_NOPE), F32), -sin, sin, padq], -1) * qs
    return jnp.concatenate([cm, sm, cq, sq], -1)


def _state_to_tiles(s):
    eye = jnp.eye(GLA_HEADS, dtype=s.dtype)
    st = jnp.swapaxes(s, 2, 3)
    bd = st[:, :, :, None, :] * eye[None, :, None, :, None]
    return bd.reshape(s.shape[0], GLA_V, GLA_K)


def _tiles_to_state(t):
    b = t.shape[0]
    t5 = t.reshape(b, GLA_HEADS, GLA_DV, GLA_HEADS, GLA_DK)
    idx = jnp.arange(GLA_HEADS)
    diag = t5[:, idx, :, idx, :]
    return jnp.transpose(diag, (1, 0, 3, 2))


def _tile(n, pref):
    return pref if n % pref == 0 else n


def kernel(x_prompt, x_sample, mem_prompt, cache_pool, cache_mla_ckv, cache_mla_krope, state_gla,
           cache_mem_k, cache_mem_v, ln_g, ln_b, w_ffn1_in, w_ffn1_out, w_in, pool_w, pool_scale,
           mla_q_norm, mla_kv_norm, mla_w_uq, mla_w_uk, mla_w_uv, gla_w_a2, gla_b_a, gla_norm,
           w_branch_pool, w_branch_mla, w_branch_gla, w_out, mem_w_q, mem_w_k, mem_w_v, mem_w_o,
           w_ffn2_in, w_ffn2_out):
    nl = w_in.shape[0]
    bp, sp, _ = x_prompt.shape
    bs, ss, _ = x_sample.shape
    past = cache_mla_ckv.shape[2]
    n_mem = mem_prompt.shape[1]

    w_a, w_gates, w_q2, w_k, w_kt, w_v, w_abs, w_uvp, pool_bd, w_a2 = _pack_weights(
        w_in, pool_w, mla_w_uq, mla_w_uk, mla_w_uv, gla_w_a2)
    bf = lambda w: w.astype(BF16)
    ffn_w = ((bf(w_ffn1_in), bf(w_ffn1_out)), (bf(w_ffn2_in), bf(w_ffn2_out)))
    w_bp, w_bm, w_bg, w_o = bf(w_branch_pool), bf(w_branch_mla), bf(w_branch_gla), bf(w_out)
    mw_q, mw_k, mw_v, mw_o = bf(mem_w_q), bf(mem_w_k), bf(mem_w_v), bf(mem_w_o)
    lng = ln_g.reshape(nl, 4, 1, D_MODEL)
    lnb = ln_b.reshape(nl, 4, 1, D_MODEL)
    qnorm = mla_q_norm.reshape(nl, 1, MLA_Q_RANK)
    kvnorm = mla_kv_norm.reshape(nl, 1, MLA_KV_RANK)
    pscale = pool_scale.reshape(nl, 1, POOL_WIDTH)
    b_a = gla_b_a.reshape(nl, 1, GLA_K)
    norm_g = jnp.tile(gla_norm, (1, GLA_HEADS)).reshape(nl, 1, GLA_V)
    hsel = (jnp.arange(GLA_K)[:, None] // GLA_DK == jnp.arange(GLA_V)[None, :] // GLA_DV).astype(BF16)
    hblk = (jnp.arange(GLA_V)[:, None] // GLA_DV == jnp.arange(GLA_V)[None, :] // GLA_DV).astype(BF16)

    tmem = _tile(bp * n_mem, 512)
    mem_k_out, mem_v_out, mem_k, mem_v = _memkv(
        mem_prompt.reshape(bp * n_mem, D_MODEL), mw_k, mw_v, tmem)
    mem_k = mem_k.reshape(nl, bp, n_mem, D_MODEL)
    mem_v = mem_v.reshape(nl, bp, n_mem, D_MODEL)
    smem_k = cache_mem_k.reshape(nl, bs, n_mem, D_MODEL)
    smem_v = cache_mem_v.reshape(nl, bs, n_mem, D_MODEL)

    groups = {
        "p": dict(nb=bp, seq=sp, pos0=0, tm=_tile(sp, 1024)),
        "s": dict(nb=bs, seq=ss, pos0=past, tm=_tile(bs * ss, 512)),
    }
    for gname, gr in groups.items():
        seq, tm = gr["seq"], gr["tm"]
        pos = gr["pos0"] + jnp.arange(seq)
        tab = _rope_table(pos)
        if tm <= seq:
            gr["tab"], gr["tab_period"] = tab, seq // tm
        else:
            gr["tab"], gr["tab_period"] = jnp.tile(tab, (tm // seq, 1)), 1
        gr["tseq"] = _tile(seq, 512)
        gr["tgla"] = _tile(seq, 1024)
        gr["tpool"] = _tile(seq, 2048)
        gr["ttok"] = _tile(gr["nb"] * seq, 1024)
        gr["tmem"] = _tile(seq, 1024)

    xs = {"p": x_prompt.reshape(bp * sp, D_MODEL), "s": x_sample.reshape(bs * ss, D_MODEL)}
    zero_hist = jnp.zeros((bp, POOL_PAD, POOL_WIDTH), F32)
    zero_state = jnp.zeros((bp, GLA_V, GLA_K), F32)
    outs = {n: {"pool": [], "ckv": [], "kr": [], "gla": []} for n in groups}

    for l in range(nl):
        for gname, gr in groups.items():
            nb, seq, tm = gr["nb"], gr["seq"], gr["tm"]
            x = xs[gname]
            x = _ffn(x, ffn_w[0][0], ffn_w[0][1], lng, lnb, l, 0, gr["ttok"])
            k_tile = gr["tseq"] if gname == "p" else None
            u_pool, ckv, misc, q, k, v, gla_in = _inproj(
                x, gr["tab"], w_a, w_q2, w_k if k_tile is None else w_kt, w_v, qnorm, kvnorm,
                l, tm, gr["tab_period"], k_tile)
            if gname == "p":
                hist = None
                hist16 = zero_hist
                s0t = zero_state
            else:
                hist = cache_pool[l]
                hist16 = jnp.concatenate(
                    [jnp.zeros((nb, POOL_PAD - POOL_HIST, POOL_WIDTH), F32), hist], 1)
                s0t = _state_to_tiles(state_gla[l])
            y_pool = _pool(u_pool, hist16, pool_bd, pscale, l, nb, seq, gr["tpool"], gr["pos0"])
            if gname == "p":
                y_mla = _mla_causal(q, k, v, nb, seq, gr["tseq"])
            else:
                y_mla = _mla_cached(q, ckv, misc, cache_mla_ckv, cache_mla_krope, w_abs, w_uvp,
                                    l, nb, seq, past)
            y_gla, s_fin = _gla(gla_in, misc, w_a2, b_a, norm_g, hsel, hblk, s0t, l, nb, seq, gr["tgla"])
            x = _merge(x, y_pool, y_mla, y_gla, w_gates, w_bp, w_bm, w_bg, w_o, lng, lnb, l,
                       gr["ttok"])
            if gname == "p":
                x = _memattn(x, mem_k, mem_v, l, mw_q, mw_o, lng, lnb, l, nb, seq, gr["tmem"])
            else:
                x = _memattn(x, smem_k, smem_v, l, mw_q, mw_o, lng, lnb, l, nb, seq, gr["tmem"])
            x = _ffn(x, ffn_w[1][0], ffn_w[1][1], lng, lnb, l, 3, gr["ttok"])
            xs[gname] = x

            u3 = u_pool.reshape(nb, seq, POOL_WIDTH)
            if hist is None:
                hist = jnp.zeros((nb, POOL_HIST, POOL_WIDTH), F32)
            new_pool = jnp.concatenate([hist, u3], 1)[:, -POOL_HIST:]
            o = outs[gname]
            o["pool"].append(new_pool)
            o["ckv"].append(ckv.reshape(nb, seq, MLA_KV_RANK))
            o["kr"].append(misc[:, :MLA_ROPE].reshape(nb, seq, MLA_ROPE))
            o["gla"].append(_tiles_to_state(s_fin))

    st = lambda n, key: jnp.stack(outs[n][key], 0)
    return (xs["p"].reshape(bp, sp, D_MODEL), xs["s"].reshape(bs, ss, D_MODEL),
            st("p", "pool"), st("s", "pool"),
            st("p", "ckv"), st("s", "ckv"),
            st("p", "kr"), st("s", "kr"),
            st("p", "gla"), st("s", "gla"),
            mem_k_out.reshape(nl, bp, n_mem, MEM_HEADS, MEM_HEAD_DIM),
            mem_v_out.reshape(nl, bp, n_mem, MEM_HEADS, MEM_HEAD_DIM))
```

```python
import functools
import math

import jax
import jax.numpy as jnp
from jax import lax
from jax.experimental import pallas as pl
from jax.experimental.pallas import tpu as pltpu

F32 = jnp.float32
BF16 = jnp.bfloat16

D_MODEL = 1024
DEPTH = 4
CHUNK = 64
EPS = 1e-5
ALPHA = (2 * DEPTH) ** 0.25

POOL_WINDOWS = (2, 4, 8, 16)
POOL_GROUP = D_MODEL // 16
POOL_WIDTH = len(POOL_WINDOWS) * POOL_GROUP
POOL_HIST = max(POOL_WINDOWS) - 1

MLA_HEADS = 8
MLA_Q_RANK = D_MODEL // 4
MLA_KV_RANK = D_MODEL // 4
MLA_NOPE = 64
MLA_ROPE = 32
MLA_V = 64
MLA_SCALE = (MLA_NOPE + MLA_ROPE) ** -0.5
ROPE_BASE = 10000.0
MLA_HEAD_PAD = 128
MLA_PAIR = 2
MLA_GROUP = 4
V_ONE_EVEN = MLA_V
V_ONE_ODD = MLA_HEAD_PAD
LOG2E = math.log2(math.e)

GLA_HEADS = 4
GLA_DK = 32
GLA_DV = 64
GLA_GATE_RANK = 16
GLA_GATE_TAU = 16.0
GLA_K = GLA_HEADS * GLA_DK
GLA_V = GLA_HEADS * GLA_DV

MEM_HEADS = 4
MEM_HEAD_DIM = D_MODEL // MEM_HEADS
D_FF = 2816

V7X_LANES = 128
V7X_SUBLANES = 8
V7X_VMEM_BYTES = 64 * 1024 * 1024
VMEM_LIMIT = V7X_VMEM_BYTES - 8 * 1024 * 1024

A_POOL = 0
A_QLAT = 256
A_CKV = 512
A_M1 = 768
A_M2 = 896
A_GLA = 1024
A_COLS = 1792
MISC_W = 128
GLA_IN_W = 2 * GLA_K + 2 * GLA_V


def _cparams(sem):
    return pltpu.CompilerParams(dimension_semantics=sem, vmem_limit_bytes=VMEM_LIMIT)


def _resident(shape, index_map):
    return pl.BlockSpec(shape, index_map, pipeline_mode=pl.Buffered(1))


def _layer_norm(y, g, b):
    mu = jnp.mean(y, axis=-1, keepdims=True)
    d = y - mu
    var = jnp.mean(d * d, axis=-1, keepdims=True)
    return d * lax.rsqrt(var + EPS) * g + b


def _rms_norm(y, g):
    return y * lax.rsqrt(jnp.mean(y * y, axis=-1, keepdims=True) + EPS) * g


def _dot(a, b):
    return jnp.dot(a, b, preferred_element_type=F32)


def _dot_nt(a, b):
    return lax.dot_general(a, b, (((1,), (1,)), ((), ())), preferred_element_type=F32)


def _silu(x):
    return x * jax.nn.sigmoid(x)


ROW_SUB = 256


def _row_blocks(rows):
    sub = min(ROW_SUB, rows)
    return [slice(r, r + sub) for r in range(0, rows, sub)]


FFN_CHUNKS = ((0, 1024), (1024, 1024), (2048, 768))


def _ffn_kernel(x_ref, win_ref, wout_ref, g_ref, b_ref, o_ref, act_ref):
    x = x_ref[...]
    xb = x.astype(BF16)
    for c0, cw in FFN_CHUNKS:
        gate = _dot(xb, win_ref[:, c0:c0 + cw])
        up = _dot(xb, win_ref[:, D_FF + c0:D_FF + c0 + cw])
        act_ref[:, c0:c0 + cw] = (_silu(gate) * up).astype(BF16)
    for rs in _row_blocks(x_ref.shape[0]):
        y = _dot(act_ref[rs, :], wout_ref[...])
        o_ref[rs, :] = _layer_norm(ALPHA * x_ref[rs, :] + 0.5 * y, g_ref[...], b_ref[...])


def _ffn(x, w_in, w_out, ln_g, ln_b, layer, which, tm):
    t = x.shape[0]
    return pl.pallas_call(
        _ffn_kernel,
        grid=(t // tm,),
        in_specs=[
            pl.BlockSpec((tm, D_MODEL), lambda i: (i, 0)),
            _resident((None, D_MODEL, 2 * D_FF), lambda i: (layer, 0, 0)),
            _resident((None, D_FF, D_MODEL), lambda i: (layer, 0, 0)),
            _resident((None, None, 1, D_MODEL), lambda i: (layer, which, 0, 0)),
            _resident((None, None, 1, D_MODEL), lambda i: (layer, which, 0, 0)),
        ],
        out_specs=pl.BlockSpec((tm, D_MODEL), lambda i: (i, 0)),
        out_shape=jax.ShapeDtypeStruct((t, D_MODEL), F32),
        scratch_shapes=[pltpu.VMEM((tm, D_FF), BF16)],
        compiler_params=_cparams(("parallel",)),
        name="ffn",
    )(x, w_in, w_out, ln_g, ln_b)


def _v_with_ones(vv):
    lane = lax.broadcasted_iota(jnp.int32, vv.shape, 1) % (MLA_PAIR * MLA_HEAD_PAD)
    return jnp.where((lane == V_ONE_EVEN) | (lane == V_ONE_ODD), 1.0, vv).astype(BF16)


def _inproj_kernel(x_ref, tab_ref, wa_ref, wq_ref, wk_ref, wv_ref, qn_ref, kvn_ref,
                   pool_ref, ckv_ref, misc_ref, q_ref, k_ref, v_ref, gla_ref, *, k_transposed):
    xb = x_ref[...].astype(BF16)
    h = _dot(xb, wa_ref[:, 0:A_GLA])
    gla_ref[...] = _dot(xb, wa_ref[:, A_GLA:A_COLS])
    pool_ref[...] = h[:, A_POOL:A_POOL + POOL_WIDTH]

    cm = tab_ref[:, 0:128]
    sm = tab_ref[:, 128:256]
    misc = h[:, A_M1:A_M1 + MISC_W] * cm + h[:, A_M2:A_M2 + MISC_W] * sm
    misc_ref[...] = misc

    ckv = _rms_norm(h[:, A_CKV:A_CKV + MLA_KV_RANK], kvn_ref[...])
    ckv_ref[...] = ckv
    ckvb = ckv.astype(BF16)
    lat = jnp.concatenate([ckvb, misc.astype(BF16)], axis=-1)
    if k_transposed:
        kt = _dot_nt(wk_ref[...], lat).astype(BF16)
        k_tile = k_ref.shape[-1]
        for c in range(k_ref.shape[0]):
            k_ref[c] = kt[:, c * k_tile:(c + 1) * k_tile]
    else:
        k_ref[...] = _dot(lat, wk_ref[...]).astype(BF16)
    v_ref[...] = _v_with_ones(_dot(ckvb, wv_ref[...]))

    qn = _rms_norm(h[:, A_QLAT:A_QLAT + MLA_Q_RANK], qn_ref[...]).astype(BF16)
    q2 = _dot(qn, wq_ref[...])
    kw = MLA_HEADS * MLA_HEAD_PAD
    cq = tab_ref[:, 256:384]
    sq = tab_ref[:, 384:512]
    for hd in range(MLA_HEADS):
        a = hd * MLA_HEAD_PAD
        q_ref[:, a:a + MLA_HEAD_PAD] = (
            q2[:, a:a + MLA_HEAD_PAD] * cq + q2[:, kw + a:kw + a + MLA_HEAD_PAD] * sq).astype(BF16)


def _inproj(x, tab, w_a, w_q2, w_k, w_v, qnorm, kvnorm, layer, tm, tab_period, k_tile):
    t = x.shape[0]
    kw = MLA_HEADS * MLA_HEAD_PAD
    lat_w = MLA_KV_RANK + MISC_W
    row = lambda i: (i, 0)
    k_transposed = k_tile is not None
    if k_transposed:
        k_shape = jax.ShapeDtypeStruct((t // k_tile, kw, k_tile), BF16)
        k_spec = pl.BlockSpec((tm // k_tile, kw, k_tile), lambda i: (i, 0, 0))
        wk_spec = _resident((None, kw, lat_w), lambda i: (layer, 0, 0))
    else:
        k_shape = jax.ShapeDtypeStruct((t, kw), BF16)
        k_spec = pl.BlockSpec((tm, kw), row)
        wk_spec = _resident((None, lat_w, kw), lambda i: (layer, 0, 0))
    out_shapes = (
        jax.ShapeDtypeStruct((t, POOL_WIDTH), F32),
        jax.ShapeDtypeStruct((t, MLA_KV_RANK), F32),
        jax.ShapeDtypeStruct((t, MISC_W), F32),
        jax.ShapeDtypeStruct((t, kw), BF16),
        k_shape,
        jax.ShapeDtypeStruct((t, kw), BF16),
        jax.ShapeDtypeStruct((t, GLA_IN_W), F32),
    )
    out_specs = [pl.BlockSpec((tm, s.shape[-1]), row) for s in out_shapes]
    out_specs[4] = k_spec
    return pl.pallas_call(
        functools.partial(_inproj_kernel, k_transposed=k_transposed),
        grid=(t // tm,),
        in_specs=[
            pl.BlockSpec((tm, D_MODEL), row),
            pl.BlockSpec((tm, 512), lambda i: (i % tab_period, 0)),
            _resident((None, D_MODEL, A_COLS), lambda i: (layer, 0, 0)),
            _resident((None, MLA_Q_RANK, 2 * kw), lambda i: (layer, 0, 0)),
            wk_spec,
            _resident((None, MLA_KV_RANK, kw), lambda i: (layer, 0, 0)),
            _resident((None, 1, MLA_Q_RANK), lambda i: (layer, 0, 0)),
            _resident((None, 1, MLA_KV_RANK), lambda i: (layer, 0, 0)),
        ],
        out_specs=out_specs,
        out_shape=out_shapes,
        compiler_params=_cparams(("parallel",)),
        name="inproj",
    )(x, tab, w_a, w_q2, w_k, w_v, qnorm, kvnorm)


POOL_PAD = 16


def _pool_kernel(u_ref, hist_ref, w_ref, scale_ref, y_ref, ext_ref, *, tp, pos0):
    j = pl.program_id(1)

    @pl.when(j == 0)
    def _():
        ext_ref[0:POOL_PAD, :] = hist_ref[...]

    tok = u_ref[...]
    ext_ref[POOL_PAD:POOL_PAD + tp, :] = tok
    acc = ext_ref[...]
    sums = {}
    w = 1
    while w < max(POOL_WINDOWS):
        acc = acc + pltpu.roll(acc, w, 0)
        w *= 2
        if w in POOL_WINDOWS:
            sums[w] = acc[POOL_PAD:POOL_PAD + tp, :]
    lane = lax.broadcasted_iota(jnp.int32, (tp, POOL_WIDTH), 1)
    pos = pos0 + j * tp + lax.broadcasted_iota(jnp.int32, (tp, POOL_WIDTH), 0)
    total = None
    width = None
    for gi, w in enumerate(POOL_WINDOWS):
        in_group = lane >= gi * POOL_GROUP
        total = sums[w] if total is None else jnp.where(in_group, sums[w], total)
        width = jnp.full_like(pos, w) if width is None else jnp.where(in_group, w, width)
    cnt = jnp.minimum(pos + 1, width).astype(F32)
    pooled = total / cnt - tok
    y = _dot(pooled.astype(BF16), w_ref[...]) * scale_ref[...]
    y_ref[...] = y.astype(BF16)
    ext_ref[0:POOL_PAD, :] = ext_ref[tp:tp + POOL_PAD, :]


def _pool(u, hist16, w_bd, scale, layer, nb, seq, tp, pos0):
    nt = seq // tp
    return pl.pallas_call(
        functools.partial(_pool_kernel, tp=tp, pos0=pos0),
        grid=(nb, nt),
        in_specs=[
            pl.BlockSpec((tp, POOL_WIDTH), lambda b, j: (b * nt + j, 0)),
            pl.BlockSpec((None, POOL_PAD, POOL_WIDTH), lambda b, j: (b, 0, 0)),
            _resident((None, POOL_WIDTH, POOL_WIDTH), lambda b, j: (layer, 0, 0)),
            _resident((None, 1, POOL_WIDTH), lambda b, j: (layer, 0, 0)),
        ],
        out_specs=pl.BlockSpec((tp, POOL_WIDTH), lambda b, j: (b * nt + j, 0)),
        out_shape=jax.ShapeDtypeStruct((nb * seq, POOL_WIDTH), BF16),
        scratch_shapes=[pltpu.VMEM((POOL_PAD + tp, POOL_WIDTH), F32)],
        compiler_params=_cparams(("parallel", "arbitrary")),
        name="pool",
    )(u, hist16, w_bd, scale)


def _pair_output(acc0, acc1):
    lane = lax.broadcasted_iota(jnp.int32, acc0.shape, 1)
    o0 = acc0 / acc0[:, V_ONE_EVEN:V_ONE_EVEN + 1]
    o1 = acc1 / acc1[:, 0:1]
    return jnp.where(lane < MLA_V, o0, o1).astype(BF16)


def _mla_causal_kernel(q_ref, kt_ref, v_ref, o_ref, m_ref, acc_ref, sa_ref, sb_ref, *, tq):
    i = pl.program_id(2)
    m_ref[...] = jnp.full(m_ref.shape, -jnp.inf, F32)
    acc_ref[...] = jnp.zeros(acc_ref.shape, F32)
    reps = tq // V7X_LANES

    def scores(kt, dst_ref):
        for h in range(MLA_GROUP):
            a = h * MLA_HEAD_PAD
            dst_ref[h] = _dot(q_ref[:, a:a + MLA_HEAD_PAD], kt_ref[kt, a:a + MLA_HEAD_PAD, :])

    def consume(src_ref, kt, masked):
        r0 = pl.multiple_of(kt * tq, tq)
        for h in range(MLA_GROUP):
            a = h * MLA_HEAD_PAD
            s = src_ref[h]
            if masked:
                qc = lax.broadcasted_iota(jnp.int32, (tq, tq), 0) // CHUNK
                kc = lax.broadcasted_iota(jnp.int32, (tq, tq), 1) // CHUNK
                s = jnp.where(kc <= qc, s, -jnp.inf)
            m_prev = m_ref[h]
            m_new = jnp.maximum(m_prev, jnp.max(s, axis=-1, keepdims=True))
            p = jnp.exp2(s - jnp.tile(m_new, (1, reps)))
            corr = jnp.exp2(m_prev - m_new)
            acc_ref[h] = corr * acc_ref[h] + _dot(p.astype(BF16),
                                                  v_ref[pl.ds(r0, tq), a:a + MLA_HEAD_PAD])
            m_ref[h] = m_new

    scores(0, sa_ref)

    def body(j, carry):
        scores(2 * j + 1, sb_ref)
        consume(sa_ref, 2 * j, False)
        scores(2 * j + 2, sa_ref)
        consume(sb_ref, 2 * j + 1, False)
        return carry

    lax.fori_loop(0, i // 2, body, 0)

    def finish():
        ow = MLA_PAIR * MLA_V
        for pr in range(MLA_GROUP // MLA_PAIR):
            o_ref[:, pr * ow:(pr + 1) * ow] = _pair_output(acc_ref[MLA_PAIR * pr],
                                                           acc_ref[MLA_PAIR * pr + 1])

    @pl.when(i % 2 == 0)
    def _():
        consume(sa_ref, i, True)
        finish()

    @pl.when(i % 2 == 1)
    def _():
        scores(i, sb_ref)
        consume(sa_ref, i - 1, False)
        consume(sb_ref, i, True)
        finish()


def _mla_causal(q, kt, v, nb, seq, tq):
    nq = seq // tq
    ngrp = MLA_HEADS // MLA_GROUP
    pw = MLA_GROUP * MLA_HEAD_PAD
    return pl.pallas_call(
        functools.partial(_mla_causal_kernel, tq=tq),
        grid=(nb, ngrp, nq),
        in_specs=[
            pl.BlockSpec((tq, pw), lambda b, p, i: (b * nq + i, p)),
            pl.BlockSpec((nq, pw, tq), lambda b, p, i: (b, p, 0)),
            pl.BlockSpec((seq, pw), lambda b, p, i: (b, p)),
        ],
        out_specs=pl.BlockSpec((tq, MLA_GROUP * MLA_V), lambda b, p, i: (b * nq + i, p)),
        out_shape=jax.ShapeDtypeStruct((nb * seq, MLA_HEADS * MLA_V), BF16),
        scratch_shapes=[pltpu.VMEM((MLA_GROUP, tq, V7X_LANES), F32),
                        pltpu.VMEM((MLA_GROUP, tq, MLA_HEAD_PAD), F32),
                        pltpu.VMEM((MLA_GROUP, tq, tq), F32),
                        pltpu.VMEM((MLA_GROUP, tq, tq), F32)],
        compiler_params=_cparams(("parallel", "parallel", "arbitrary")),
        name="mla_causal",
    )(q, kt, v)


def _mla_cached_kernel(q_ref, cp_ref, rp_ref, cn_ref, mn_ref, wabs_ref, wuv_ref, o_ref):
    qa = jnp.concatenate(
        [_dot(q_ref[:, h * MLA_HEAD_PAD:(h + 1) * MLA_HEAD_PAD], wabs_ref[h]) for h in range(MLA_HEADS)],
        axis=0).astype(BF16)
    qa_lat = qa[:, 0:MLA_KV_RANK]
    qa_misc = qa[:, MLA_KV_RANK:]
    cp = cp_ref[...].astype(BF16)
    cn = cn_ref[...].astype(BF16)
    s1 = _dot_nt(qa_lat, cp) + _dot_nt(qa_misc[:, 0:MLA_ROPE], rp_ref[...].astype(BF16))
    s2 = _dot_nt(qa_lat, cn) + _dot_nt(qa_misc, mn_ref[...].astype(BF16))
    m = jnp.maximum(jnp.max(s1, axis=-1, keepdims=True), jnp.max(s2, axis=-1, keepdims=True))
    p1 = jnp.exp2(s1 - m)
    p2 = jnp.exp2(s2 - m)
    l = jnp.sum(p1, axis=-1, keepdims=True) + jnp.sum(p2, axis=-1, keepdims=True)
    lat = ((_dot(p1.astype(BF16), cp) + _dot(p2.astype(BF16), cn)) / l).astype(BF16)
    seq = q_ref.shape[0]
    y = _dot(lat[0:seq], wuv_ref[0])
    for h in range(1, MLA_HEADS):
        y = y + _dot(lat[h * seq:(h + 1) * seq], wuv_ref[h])
    o_ref[...] = y.astype(BF16)


def _mla_cached(q, ckv_new, misc_new, ckv_past, kr_past, w_abs, w_uvp, layer, nb, seq, past):
    kw = MLA_HEADS * MLA_HEAD_PAD
    vw = MLA_HEADS * MLA_V
    lat_w = MLA_KV_RANK + MISC_W
    return pl.pallas_call(
        _mla_cached_kernel,
        grid=(nb,),
        in_specs=[
            pl.BlockSpec((seq, kw), lambda b: (b, 0)),
            pl.BlockSpec((None, None, past, MLA_KV_RANK), lambda b: (layer, b, 0, 0)),
            pl.BlockSpec((None, None, past, MLA_ROPE), lambda b: (layer, b, 0, 0)),
            pl.BlockSpec((seq, MLA_KV_RANK), lambda b: (b, 0)),
            pl.BlockSpec((seq, MISC_W), lambda b: (b, 0)),
            _resident((None, MLA_HEADS, MLA_HEAD_PAD, lat_w), lambda b: (layer, 0, 0, 0)),
            _resident((None, MLA_HEADS, MLA_KV_RANK, vw), lambda b: (layer, 0, 0, 0)),
        ],
        out_specs=pl.BlockSpec((seq, vw), lambda b: (b, 0)),
        out_shape=jax.ShapeDtypeStruct((nb * seq, vw), BF16),
        compiler_params=_cparams(("parallel",)),
        name="mla_cached",
    )(q, ckv_past, kr_past, ckv_new, misc_new, w_abs, w_uvp)


def _gla_kernel(gq_ref, gk_ref, gv_ref, gr_ref, misc_ref, wa2_ref, ba_ref, ng_ref, hsel_ref,
                hblk_ref, s0_ref, y_ref, sfin_ref, st_ref, sec_ref, fst_ref, same_ref, kbd_ref,
                vbd_ref, sbd_ref, o_ref, *, tt):
    j = pl.program_id(1)
    c = min(CHUNK, tt)
    hc = GLA_HEADS * c
    levels = c.bit_length() - 1

    @pl.when(j == 0)
    def _():
        st_ref[...] = s0_ref[...]
        row = lax.broadcasted_iota(jnp.int32, (c, GLA_K), 0)
        trow = lax.broadcasted_iota(jnp.int32, (c, hc), 0)
        tcol = lax.broadcasted_iota(jnp.int32, (c, hc), 1) % c
        for lv in range(levels):
            m = 1 << lv
            sec_ref[lv] = ((row & m) != 0).astype(F32)
            fst_ref[lv] = ((row & m) == 0).astype(F32)
            same_ref[lv] = ((trow // (2 * m)) == (tcol // (2 * m))).astype(F32)
        kbd_ref[...] = (lax.broadcasted_iota(jnp.int32, (hc, GLA_K), 0) // c
                        == lax.broadcasted_iota(jnp.int32, (hc, GLA_K), 1) // GLA_DK).astype(BF16)
        vbd_ref[...] = (lax.broadcasted_iota(jnp.int32, (hc, GLA_V), 0) // c
                        == lax.broadcasted_iota(jnp.int32, (hc, GLA_V), 1) // GLA_DV).astype(BF16)
        sbd_ref[...] = (lax.broadcasted_iota(jnp.int32, (GLA_V, GLA_K), 0) // GLA_DV
                        == lax.broadcasted_iota(jnp.int32, (GLA_V, GLA_K), 1) // GLA_DK).astype(F32)

    z = _dot(misc_ref[...].astype(BF16), wa2_ref[...]) + ba_ref[...]
    la_all = jax.nn.log_sigmoid(z) * (LOG2E / GLA_GATE_TAU)

    st = st_ref[...]
    for r in range(tt // c):
        rs = slice(r * c, (r + 1) * c)
        q = gq_ref[rs, :] * (GLA_DK ** -0.5)
        k = gk_ref[rs, :]
        v = gv_ref[rs, :]
        la = la_all[rs, :]

        scores = jnp.zeros((c, hc), F32)
        p = la
        g = la
        for lv in range(levels):
            m = 1 << lv
            if m < V7X_SUBLANES:
                sec = sec_ref[lv]
                fst = fst_ref[lv]
                qe = (q * jnp.exp2(p) * sec).astype(BF16)
                ke = (k * jnp.exp2(g - p) * fst).astype(BF16)
                ke_bd = jnp.tile(ke, (GLA_HEADS, 1)) * kbd_ref[...]
                scores = scores + _dot_nt(qe, ke_bd) * same_ref[lv]
                up = sec * pltpu.roll(g, m, 0)
                p = p + up
                g = g + up + fst * pltpu.roll(g, c - m, 0)
            else:
                nb2 = c // (2 * m)

                def halves(a):
                    a4 = a.reshape(nb2, 2, m, a.shape[-1])
                    return a4[:, 0], a4[:, 1]

                def join(lo, hi):
                    return jnp.stack([lo, hi], axis=1).reshape(c, lo.shape[-1])

                p_lo, p_hi = halves(p)
                g_lo, g_hi = halves(g)
                qe_hi = (halves(q)[1] * jnp.exp2(p_hi)).reshape(c // 2, GLA_K).astype(BF16)
                ke_lo = halves(k)[0] * jnp.exp2(g_lo - p_lo)
                ke = join(ke_lo, jnp.zeros_like(ke_lo)).astype(BF16)
                ke_bd = jnp.tile(ke, (GLA_HEADS, 1)) * kbd_ref[...]
                a_hi = _dot_nt(qe_hi, ke_bd) * halves(same_ref[lv])[1].reshape(c // 2, hc)
                s_lo, s_hi = halves(scores)
                scores = join(s_lo, s_hi + a_hi.reshape(nb2, m, hc))
                tot = g_lo + g_hi
                p = join(p_lo, p_hi + g_lo)
                g = join(tot, tot)

        o = _dot_nt((q * jnp.exp2(p)).astype(BF16), st.astype(BF16))
        v_bd = jnp.tile(v.astype(BF16), (GLA_HEADS, 1)) * vbd_ref[...]
        o_ref[rs, :] = o + _dot(scores.astype(BF16), v_bd)

        kdec = (k * jnp.exp2(g - p)).astype(BF16)
        upd = _dot(jnp.transpose(v).astype(BF16), kdec)
        st = st * jnp.exp2(g[0:1, :]) + upd * sbd_ref[...]

    st_ref[...] = st
    sfin_ref[...] = st

    qk = (gq_ref[...] * (GLA_DK ** -0.5) * gk_ref[...]).astype(BF16)
    o = o_ref[...] + _dot(qk, hsel_ref[...]) * gv_ref[...]
    o2 = o * o
    o2_hi = o2.astype(BF16)
    o2_lo = (o2 - o2_hi.astype(F32)).astype(BF16)
    ms = (_dot(o2_hi, hblk_ref[...]) + _dot(o2_lo, hblk_ref[...])) * (1.0 / GLA_DV)
    y = o * lax.rsqrt(ms + EPS) * ng_ref[...] * _silu(gr_ref[...])
    y_ref[...] = y.astype(BF16)


def _gla(gla_in, misc, w_a2, b_a, norm_g, hsel, hblk, s0t, layer, nb, seq, tt):
    nt = seq // tt
    row = lambda b, j: (b * nt + j, 0)
    c = min(CHUNK, tt)
    hc = GLA_HEADS * c
    levels = c.bit_length() - 1
    return pl.pallas_call(
        functools.partial(_gla_kernel, tt=tt),
        grid=(nb, nt),
        in_specs=[
            pl.BlockSpec((tt, GLA_K), lambda b, j: (b * nt + j, 0)),
            pl.BlockSpec((tt, GLA_K), lambda b, j: (b * nt + j, 1)),
            pl.BlockSpec((tt, GLA_V), lambda b, j: (b * nt + j, 1)),
            pl.BlockSpec((tt, GLA_V), lambda b, j: (b * nt + j, 2)),
            pl.BlockSpec((tt, MISC_W), row),
            _resident((None, MISC_W, GLA_K), lambda b, j: (layer, 0, 0)),
            _resident((None, 1, GLA_K), lambda b, j: (layer, 0, 0)),
            _resident((None, 1, GLA_V), lambda b, j: (layer, 0, 0)),
            _resident((GLA_K, GLA_V), lambda b, j: (0, 0)),
            _resident((GLA_V, GLA_V), lambda b, j: (0, 0)),
            pl.BlockSpec((None, GLA_V, GLA_K), lambda b, j: (b, 0, 0)),
        ],
        out_specs=[pl.BlockSpec((tt, GLA_V), row),
                   pl.BlockSpec((None, GLA_V, GLA_K), lambda b, j: (b, 0, 0))],
        out_shape=(jax.ShapeDtypeStruct((nb * seq, GLA_V), BF16),
                   jax.ShapeDtypeStruct((nb, GLA_V, GLA_K), F32)),
        scratch_shapes=[pltpu.VMEM((GLA_V, GLA_K), F32),
                        pltpu.VMEM((levels, c, GLA_K), F32),
                        pltpu.VMEM((levels, c, GLA_K), F32),
                        pltpu.VMEM((levels, c, hc), F32),
                        pltpu.VMEM((hc, GLA_K), BF16),
                        pltpu.VMEM((hc, GLA_V), BF16),
                        pltpu.VMEM((GLA_V, GLA_K), F32),
                        pltpu.VMEM((tt, GLA_V), F32)],
        compiler_params=_cparams(("parallel", "arbitrary")),
        name="gla",
    )(gla_in, gla_in, gla_in, gla_in, misc, w_a2, b_a, norm_g, hsel, hblk, s0t)


def _merge_kernel(x_ref, yp_ref, ym_ref, yg_ref, wg_ref, wbp_ref, wbm_ref, wbg_ref, wo_ref,
                  g_ref, b_ref, o_ref):
    for rs in _row_blocks(x_ref.shape[0]):
        x = x_ref[rs, :]
        xb = x.astype(BF16)
        m = jax.nn.sigmoid(_dot(xb, wg_ref[:, 0:D_MODEL])) * _dot(yp_ref[rs, :], wbp_ref[...])
        m = m + (jax.nn.sigmoid(_dot(xb, wg_ref[:, D_MODEL:2 * D_MODEL]))
                 * _dot(ym_ref[rs, :], wbm_ref[...]))
        m = m + (jax.nn.sigmoid(_dot(xb, wg_ref[:, 2 * D_MODEL:3 * D_MODEL]))
                 * _dot(yg_ref[rs, :], wbg_ref[...]))
        y = _dot(m.astype(BF16), wo_ref[...])
        o_ref[rs, :] = _layer_norm(ALPHA * x + y, g_ref[...], b_ref[...])


def _merge(x, y_pool, y_mla, y_gla, w_gates, w_bp, w_bm, w_bg, w_out, ln_g, ln_b, layer, tm):
    t = x.shape[0]
    row = lambda i: (i, 0)
    lay = lambda i: (layer, 0, 0)
    return pl.pallas_call(
        _merge_kernel,
        grid=(t // tm,),
        in_specs=[
            pl.BlockSpec((tm, D_MODEL), row),
            pl.BlockSpec((tm, POOL_WIDTH), row),
            pl.BlockSpec((tm, MLA_HEADS * MLA_V), row),
            pl.BlockSpec((tm, GLA_V), row),
            _resident((None, D_MODEL, 3 * D_MODEL), lay),
            _resident((None, POOL_WIDTH, D_MODEL), lay),
            _resident((None, MLA_HEADS * MLA_V, D_MODEL), lay),
            _resident((None, GLA_V, D_MODEL), lay),
            _resident((None, D_MODEL, D_MODEL), lay),
            _resident((None, None, 1, D_MODEL), lambda i: (layer, 1, 0, 0)),
            _resident((None, None, 1, D_MODEL), lambda i: (layer, 1, 0, 0)),
        ],
        out_specs=pl.BlockSpec((tm, D_MODEL), row),
        out_shape=jax.ShapeDtypeStruct((t, D_MODEL), F32),
        compiler_params=_cparams(("parallel",)),
        name="merge",
    )(x, y_pool, y_mla, y_gla, w_gates, w_bp, w_bm, w_bg, w_out, ln_g, ln_b)


def _memkv_kernel(m_ref, wk_ref, wv_ref, k_ref, v_ref, kb_ref, vb_ref):
    mb = m_ref[...].astype(BF16)
    k = _dot(mb, wk_ref[...])
    v = _dot(mb, wv_ref[...])
    kb_ref[...] = k.astype(BF16)
    vb_ref[...] = v.astype(BF16)
    for h in range(MEM_HEADS):
        a = h * MEM_HEAD_DIM
        k_ref[:, h, :] = k[:, a:a + MEM_HEAD_DIM]
        v_ref[:, h, :] = v[:, a:a + MEM_HEAD_DIM]


def _memkv(mem, w_k, w_v, tm):
    rows = mem.shape[0]
    nl = w_k.shape[0]
    split = pl.BlockSpec((None, tm, MEM_HEADS, MEM_HEAD_DIM), lambda l, i: (l, i, 0, 0))
    flat = pl.BlockSpec((None, tm, D_MODEL), lambda l, i: (l, i, 0))
    return pl.pallas_call(
        _memkv_kernel,
        grid=(nl, rows // tm),
        in_specs=[
            pl.BlockSpec((tm, D_MODEL), lambda l, i: (i, 0)),
            pl.BlockSpec((None, D_MODEL, D_MODEL), lambda l, i: (l, 0, 0)),
            pl.BlockSpec((None, D_MODEL, D_MODEL), lambda l, i: (l, 0, 0)),
        ],
        out_specs=[split, split, flat, flat],
        out_shape=(jax.ShapeDtypeStruct((nl, rows, MEM_HEADS, MEM_HEAD_DIM), F32),
                   jax.ShapeDtypeStruct((nl, rows, MEM_HEADS, MEM_HEAD_DIM), F32),
                   jax.ShapeDtypeStruct((nl, rows, D_MODEL), BF16),
                   jax.ShapeDtypeStruct((nl, rows, D_MODEL), BF16)),
        compiler_params=_cparams(("parallel", "parallel")),
        name="memkv",
    )(mem, w_k, w_v)


def _memattn_kernel(x_ref, k_ref, v_ref, wq_ref, wo_ref, g_ref, b_ref, o_ref):
    x = x_ref[...]
    q = (_dot(x.astype(BF16), wq_ref[...]) * (MEM_HEAD_DIM ** -0.5 * LOG2E)).astype(BF16)
    kb = k_ref[...].astype(BF16)
    vb = v_ref[...].astype(BF16)
    outs = []
    for h in range(MEM_HEADS):
        a = h * MEM_HEAD_DIM
        s = _dot_nt(q[:, a:a + MEM_HEAD_DIM], kb[:, a:a + MEM_HEAD_DIM])
        p = jnp.exp2(s - jnp.max(s, axis=-1, keepdims=True))
        inv_l = 1.0 / jnp.sum(p, axis=-1, keepdims=True)
        outs.append(_dot(p.astype(BF16), vb[:, a:a + MEM_HEAD_DIM]) * inv_l)
    o = jnp.concatenate(outs, axis=-1).astype(BF16)
    for rs in _row_blocks(x_ref.shape[0]):
        y = _dot(o[rs, :], wo_ref[...])
        o_ref[rs, :] = _layer_norm(ALPHA * x_ref[rs, :] + y, g_ref[...], b_ref[...])


def _memattn(x, mem_k, mem_v, kv_layer, w_q, w_o, ln_g, ln_b, layer, nb, seq, tm):
    nt = seq // tm
    n_mem = mem_k.shape[2]
    lay = lambda b, i: (layer, 0, 0)
    return pl.pallas_call(
        _memattn_kernel,
        grid=(nb, nt),
        in_specs=[
            pl.BlockSpec((tm, D_MODEL), lambda b, i: (b * nt + i, 0)),
            pl.BlockSpec((None, None, n_mem, D_MODEL), lambda b, i: (kv_layer, b, 0, 0)),
            pl.BlockSpec((None, None, n_mem, D_MODEL), lambda b, i: (kv_layer, b, 0, 0)),
            _resident((None, D_MODEL, D_MODEL), lay),
            _resident((None, D_MODEL, D_MODEL), lay),
            _resident((None, None, 1, D_MODEL), lambda b, i: (layer, 2, 0, 0)),
            _resident((None, None, 1, D_MODEL), lambda b, i: (layer, 2, 0, 0)),
        ],
        out_specs=pl.BlockSpec((tm, D_MODEL), lambda b, i: (b * nt + i, 0)),
        out_shape=jax.ShapeDtypeStruct((nb * seq, D_MODEL), F32),
        compiler_params=_cparams(("parallel", "parallel")),
        name="memattn",
    )(x, mem_k, mem_v, w_q, w_o, ln_g, ln_b)


def _pack_weights(w_in, pool_w, mla_w_uq, mla_w_uk, mla_w_uv, gla_w_a2):
    nl = w_in.shape[0]
    splits = (POOL_WIDTH, MLA_Q_RANK, MLA_KV_RANK, MLA_ROPE, GLA_K, GLA_K, GLA_V,
              GLA_GATE_RANK, GLA_V, 3 * D_MODEL)
    offs = [0]
    for s in splits:
        offs.append(offs[-1] + s)
    col = lambda n: w_in[:, :, offs[n]:offs[n + 1]]
    half = MLA_ROPE // 2
    kr = col(3)
    kr_sw = jnp.concatenate([kr[..., half:], kr[..., :half]], -1)
    zeros = lambda n: jnp.zeros((nl, D_MODEL, n), w_in.dtype)
    w_a = jnp.concatenate(
        [col(0), col(1), col(2),
         kr, col(7), zeros(MISC_W - MLA_ROPE - GLA_GATE_RANK),
         kr_sw, zeros(MISC_W - MLA_ROPE),
         col(4), col(5), col(6), col(8)], -1).astype(BF16)
    w_gates = col(9).astype(BF16)

    uq = mla_w_uq.reshape(nl, MLA_Q_RANK, MLA_HEADS, MLA_NOPE + MLA_ROPE)
    uq_n, uq_r = uq[..., :MLA_NOPE], uq[..., MLA_NOPE:]
    uq_rs = jnp.concatenate([uq_r[..., half:], uq_r[..., :half]], -1)
    padq = jnp.zeros((nl, MLA_Q_RANK, MLA_HEADS, MLA_HEAD_PAD - MLA_NOPE - MLA_ROPE), uq.dtype)
    q_main = jnp.concatenate([uq_n, uq_r, padq], -1)
    q_swap = jnp.concatenate([jnp.zeros_like(uq_n), uq_rs, padq], -1)
    kw = MLA_HEADS * MLA_HEAD_PAD
    w_q2 = jnp.concatenate([q_main.reshape(nl, MLA_Q_RANK, kw),
                            q_swap.reshape(nl, MLA_Q_RANK, kw)], -1).astype(BF16)

    padk = jnp.zeros((nl, MLA_KV_RANK, MLA_HEADS, MLA_HEAD_PAD - MLA_NOPE), mla_w_uk.dtype)
    k_lat = jnp.concatenate([mla_w_uk, padk], -1).reshape(nl, MLA_KV_RANK, kw)
    place = jnp.zeros((MISC_W, MLA_HEADS, MLA_HEAD_PAD), F32)
    idx = jnp.arange(MLA_ROPE)
    place = place.at[idx, :, MLA_NOPE + idx].set(1.0).reshape(MISC_W, kw)
    k_misc = jnp.broadcast_to(place, (nl, MISC_W, kw))
    w_k = jnp.concatenate([k_lat, k_misc], 1).astype(BF16)
    w_kt = jnp.swapaxes(w_k, 1, 2)
    uv = mla_w_uv.reshape(nl, MLA_KV_RANK, MLA_HEADS // MLA_PAIR, MLA_PAIR, MLA_V)
    padv = jnp.zeros_like(uv[:, :, :, 0])
    w_v = jnp.stack([jnp.concatenate([uv[:, :, :, 0], padv], -1),
                     jnp.concatenate([padv, uv[:, :, :, 1]], -1)], 3)
    w_v = w_v.reshape(nl, MLA_KV_RANK, kw).astype(BF16)

    uk_t = jnp.transpose(mla_w_uk, (0, 2, 3, 1))
    a_lat = jnp.concatenate(
        [uk_t, jnp.zeros((nl, MLA_HEADS, MLA_HEAD_PAD - MLA_NOPE, MLA_KV_RANK), uk_t.dtype)], 2)
    a_misc = jnp.zeros((MLA_HEAD_PAD, MISC_W), F32).at[MLA_NOPE + idx, idx].set(1.0)
    a_misc = jnp.broadcast_to(a_misc, (nl, MLA_HEADS, MLA_HEAD_PAD, MISC_W))
    w_abs = jnp.concatenate([a_lat, a_misc], -1).astype(BF16)
    eye_h = jnp.eye(MLA_HEADS, dtype=mla_w_uv.dtype)
    uv_h = jnp.transpose(mla_w_uv, (0, 2, 1, 3))
    w_uvp = (uv_h[:, :, :, None, :] * eye_h[None, :, None, :, None]).reshape(
        nl, MLA_HEADS, MLA_KV_RANK, MLA_HEADS * MLA_V).astype(BF16)

    ng = len(POOL_WINDOWS)
    eye = jnp.eye(ng, dtype=pool_w.dtype)
    pool_bd = (pool_w[:, :, :, None, :] * eye[None, :, None, :, None]).reshape(
        nl, POOL_WIDTH, POOL_WIDTH).astype(BF16)

    w_a2 = jnp.zeros((nl, MISC_W, GLA_K), gla_w_a2.dtype)
    w_a2 = w_a2.at[:, MLA_ROPE:MLA_ROPE + GLA_GATE_RANK, :].set(gla_w_a2).astype(BF16)
    return w_a, w_gates, w_q2, w_k, w_kt, w_v, w_abs, w_uvp, pool_bd, w_a2


def _rope_table(pos):
    half = MLA_ROPE // 2
    inv = ROPE_BASE ** (-jnp.arange(half, dtype=F32) / half)
    ang = pos.astype(F32)[:, None] * inv
    cos, sin = jnp.cos(ang), jnp.sin(ang)
    n = pos.shape[0]
    one = jnp.ones((n, GLA_GATE_RANK), F32)
    cm = jnp.concatenate([cos, cos, one, jnp.zeros((n, MISC_W - MLA_ROPE - GLA_GATE_RANK), F32)], -1)
    sm = jnp.concatenate([-sin, sin, jnp.zeros((n, MISC_W - MLA_ROPE), F32)], -1)
    padq = jnp.zeros((n, MLA_HEAD_PAD - MLA_NOPE - MLA_ROPE), F32)
    qs = MLA_SCALE * LOG2E
    cq = jnp.concatenate([jnp.ones((n, MLA_NOPE), F32), cos, cos, padq], -1) * qs
    sq = jnp.concatenate([jnp.zeros((n, MLA_NOPE), F32), -sin, sin, padq], -1) * qs
    return jnp.concatenate([cm, sm, cq, sq], -1)


def _state_to_tiles(s):
    eye = jnp.eye(GLA_HEADS, dtype=s.dtype)
    st = jnp.swapaxes(s, 2, 3)
    bd = st[:, :, :, None, :] * eye[None, :, None, :, None]
    return bd.reshape(s.shape[0], GLA_V, GLA_K)


def _tiles_to_state(t):
    b = t.shape[0]
    t5 = t.reshape(b, GLA_HEADS, GLA_DV, GLA_HEADS, GLA_DK)
    idx = jnp.arange(GLA_HEADS)
    diag = t5[:, idx, :, idx, :]
    return jnp.transpose(diag, (1, 0, 3, 2))


def _tile(n, pref):
    return pref if n % pref == 0 else n


def kernel(x_prompt, x_sample, mem_prompt, cache_pool, cache_mla_ckv, cache_mla_krope, state_gla,
           cache_mem_k, cache_mem_v, ln_g, ln_b, w_ffn1_in, w_ffn1_out, w_in, pool_w, pool_scale,
           mla_q_norm, mla_kv_norm, mla_w_uq, mla_w_uk, mla_w_uv, gla_w_a2, gla_b_a, gla_norm,
           w_branch_pool, w_branch_mla, w_branch_gla, w_out, mem_w_q, mem_w_k, mem_w_v, mem_w_o,
           w_ffn2_in, w_ffn2_out):
    nl = w_in.shape[0]
    bp, sp, _ = x_prompt.shape
    bs, ss, _ = x_sample.shape
    past = cache_mla_ckv.shape[2]
    n_mem = mem_prompt.shape[1]

    w_a, w_gates, w_q2, w_k, w_kt, w_v, w_abs, w_uvp, pool_bd, w_a2 = _pack_weights(
        w_in, pool_w, mla_w_uq, mla_w_uk, mla_w_uv, gla_w_a2)
    bf = lambda w: w.astype(BF16)
    ffn_w = ((bf(w_ffn1_in), bf(w_ffn1_out)), (bf(w_ffn2_in), bf(w_ffn2_out)))
    w_bp, w_bm, w_bg, w_o = bf(w_branch_pool), bf(w_branch_mla), bf(w_branch_gla), bf(w_out)
    mw_q, mw_k, mw_v, mw_o = bf(mem_w_q), bf(mem_w_k), bf(mem_w_v), bf(mem_w_o)
    lng = ln_g.reshape(nl, 4, 1, D_MODEL)
    lnb = ln_b.reshape(nl, 4, 1, D_MODEL)
    qnorm = mla_q_norm.reshape(nl, 1, MLA_Q_RANK)
    kvnorm = mla_kv_norm.reshape(nl, 1, MLA_KV_RANK)
    pscale = pool_scale.reshape(nl, 1, POOL_WIDTH)
    b_a = gla_b_a.reshape(nl, 1, GLA_K)
    norm_g = jnp.tile(gla_norm, (1, GLA_HEADS)).reshape(nl, 1, GLA_V)
    hsel = (jnp.arange(GLA_K)[:, None] // GLA_DK == jnp.arange(GLA_V)[None, :] // GLA_DV).astype(BF16)
    hblk = (jnp.arange(GLA_V)[:, None] // GLA_DV == jnp.arange(GLA_V)[None, :] // GLA_DV).astype(BF16)

    tmem = _tile(bp * n_mem, 512)
    mem_k_out, mem_v_out, mem_k, mem_v = _memkv(
        mem_prompt.reshape(bp * n_mem, D_MODEL), mw_k, mw_v, tmem)
    mem_k = mem_k.reshape(nl, bp, n_mem, D_MODEL)
    mem_v = mem_v.reshape(nl, bp, n_mem, D_MODEL)
    smem_k = cache_mem_k.reshape(nl, bs, n_mem, D_MODEL)
    smem_v = cache_mem_v.reshape(nl, bs, n_mem, D_MODEL)

    groups = {
        "p": dict(nb=bp, seq=sp, pos0=0, tm=_tile(sp, 1024)),
        "s": dict(nb=bs, seq=ss, pos0=past, tm=_tile(bs * ss, 512)),
    }
    for gname, gr in groups.items():
        seq, tm = gr["seq"], gr["tm"]
        pos = gr["pos0"] + jnp.arange(seq)
        tab = _rope_table(pos)
        if tm <= seq:
            gr["tab"], gr["tab_period"] = tab, seq // tm
        else:
            gr["tab"], gr["tab_period"] = jnp.tile(tab, (tm // seq, 1)), 1
        gr["tseq"] = _tile(seq, 512)
        gr["tgla"] = _tile(seq, 1024)
        gr["tpool"] = _tile(seq, 2048)
        gr["ttok"] = _tile(gr["nb"] * seq, 1024)
        gr["tmem"] = _tile(seq, 1024)

    xs = {"p": x_prompt.reshape(bp * sp, D_MODEL), "s": x_sample.reshape(bs * ss, D_MODEL)}
    zero_hist = jnp.zeros((bp, POOL_PAD, POOL_WIDTH), F32)
    zero_state = jnp.zeros((bp, GLA_V, GLA_K), F32)
    outs = {n: {"pool": [], "ckv": [], "kr": [], "gla": []} for n in groups}

    for l in range(nl):
        for gname, gr in groups.items():
            nb, seq, tm = gr["nb"], gr["seq"], gr["tm"]
            x = xs[gname]
            x = _ffn(x, ffn_w[0][0], ffn_w[0][1], lng, lnb, l, 0, gr["ttok"])
            k_tile = gr["tseq"] if gname == "p" else None
            u_pool, ckv, misc, q, k, v, gla_in = _inproj(
                x, gr["tab"], w_a, w_q2, w_k if k_tile is None else w_kt, w_v, qnorm, kvnorm,
                l, tm, gr["tab_period"], k_tile)
            if gname == "p":
                hist = None
                hist16 = zero_hist
                s0t = zero_state
            else:
                hist = cache_pool[l]
                hist16 = jnp.concatenate(
                    [jnp.zeros((nb, POOL_PAD - POOL_HIST, POOL_WIDTH), F32), hist], 1)
                s0t = _state_to_tiles(state_gla[l])
            y_pool = _pool(u_pool, hist16, pool_bd, pscale, l, nb, seq, gr["tpool"], gr["pos0"])
            if gname == "p":
                y_mla = _mla_causal(q, k, v, nb, seq, gr["tseq"])
            else:
                y_mla = _mla_cached(q, ckv, misc, cache_mla_ckv, cache_mla_krope, w_abs, w_uvp,
                                    l, nb, seq, past)
            y_gla, s_fin = _gla(gla_in, misc, w_a2, b_a, norm_g, hsel, hblk, s0t, l, nb, seq, gr["tgla"])
            x = _merge(x, y_pool, y_mla, y_gla, w_gates, w_bp, w_bm, w_bg, w_o, lng, lnb, l,
                       gr["ttok"])
            if gname == "p":
                x = _memattn(x, mem_k, mem_v, l, mw_q, mw_o, lng, lnb, l, nb, seq, gr["tmem"])
            else:
                x = _memattn(x, smem_k, smem_v, l, mw_q, mw_o, lng, lnb, l, nb, seq, gr["tmem"])
            x = _ffn(x, ffn_w[1][0], ffn_w[1][1], lng, lnb, l, 3, gr["ttok"])
            xs[gname] = x

            u3 = u_pool.reshape(nb, seq, POOL_WIDTH)
            if hist is None:
                hist = jnp.zeros((nb, POOL_HIST, POOL_WIDTH), F32)
            new_pool = jnp.concatenate([hist, u3], 1)[:, -POOL_HIST:]
            o = outs[gname]
            o["pool"].append(new_pool)
            o["ckv"].append(ckv.reshape(nb, seq, MLA_KV_RANK))
            o["kr"].append(misc[:, :MLA_ROPE].reshape(nb, seq, MLA_ROPE))
            o["gla"].append(_tiles_to_state(s_fin))

    st = lambda n, key: jnp.stack(outs[n][key], 0)
    return (xs["p"].reshape(bp, sp, D_MODEL), xs["s"].reshape(bs, ss, D_MODEL),
            st("p", "pool"), st("s", "pool"),
            st("p", "ckv"), st("s", "ckv"),
            st("p", "kr"), st("s", "kr"),
            st("p", "gla"), st("s", "gla"),
            mem_k_out.reshape(nl, bp, n_mem, MEM_HEADS, MEM_HEAD_DIM),
            mem_v_out.reshape(nl, bp, n_mem, MEM_HEADS, MEM_HEAD_DIM))
```

```python
import functools
import math

import jax
import jax.numpy as jnp
from jax import lax
from jax.experimental import pallas as pl
from jax.experimental.pallas import tpu as pltpu

F32 = jnp.float32
BF16 = jnp.bfloat16

D_MODEL = 1024
DEPTH = 4
CHUNK = 64
EPS = 1e-5
ALPHA = (2 * DEPTH) ** 0.25

POOL_WINDOWS = (2, 4, 8, 16)
POOL_GROUP = D_MODEL // 16
POOL_WIDTH = len(POOL_WINDOWS) * POOL_GROUP
POOL_HIST = max(POOL_WINDOWS) - 1

MLA_HEADS = 8
MLA_Q_RANK = D_MODEL // 4
MLA_KV_RANK = D_MODEL // 4
MLA_NOPE = 64
MLA_ROPE = 32
MLA_V = 64
MLA_SCALE = (MLA_NOPE + MLA_ROPE) ** -0.5
ROPE_BASE = 10000.0
MLA_HEAD_PAD = 128
MLA_PAIR = 2
MLA_GROUP = 4
V_ONE_EVEN = MLA_V
V_ONE_ODD = MLA_HEAD_PAD
LOG2E = math.log2(math.e)

GLA_HEADS = 4
GLA_DK = 32
GLA_DV = 64
GLA_GATE_RANK = 16
GLA_GATE_TAU = 16.0
GLA_K = GLA_HEADS * GLA_DK
GLA_V = GLA_HEADS * GLA_DV

MEM_HEADS = 4
MEM_HEAD_DIM = D_MODEL // MEM_HEADS
D_FF = 2816

V7X_LANES = 128
V7X_SUBLANES = 8
V7X_VMEM_BYTES = 64 * 1024 * 1024
VMEM_LIMIT = V7X_VMEM_BYTES - 8 * 1024 * 1024

A_POOL = 0
A_QLAT = 256
A_CKV = 512
A_M1 = 768
A_M2 = 896
A_GLA = 1024
A_COLS = 1792
MISC_W = 128
GLA_IN_W = 2 * GLA_K + 2 * GLA_V
ROPE_TAB_W = 2 * MISC_W + 2 * MLA_HEAD_PAD


def _cparams(sem):
    return pltpu.CompilerParams(dimension_semantics=sem, vmem_limit_bytes=VMEM_LIMIT)


def _resident(shape, index_map):
    return pl.BlockSpec(shape, index_map, pipeline_mode=pl.Buffered(1))


def _layer_norm(y, g, b):
    mu = jnp.mean(y, axis=-1, keepdims=True)
    d = y - mu
    var = jnp.mean(d * d, axis=-1, keepdims=True)
    return d * lax.rsqrt(var + EPS) * g + b


def _rms_norm(y, g):
    return y * lax.rsqrt(jnp.mean(y * y, axis=-1, keepdims=True) + EPS) * g


def _dot(a, b):
    return jnp.dot(a, b, preferred_element_type=F32)


def _dot_nt(a, b):
    return lax.dot_general(a, b, (((1,), (1,)), ((), ())), preferred_element_type=F32)


def _silu(x):
    return x * jax.nn.sigmoid(x)


ROW_TILE = 1024
KEY_TILE = 512
POOL_TILE = 2048
ROW_SUB = 256


def _row_blocks(rows):
    sub = min(ROW_SUB, rows)
    return [slice(r, r + sub) for r in range(0, rows, sub)]


FFN_CHUNKS = ((0, 1024), (1024, 1024), (2048, 768))


def _ffn_kernel(x_ref, win_ref, wout_ref, g_ref, b_ref, o_ref, act_ref):
    x = x_ref[...]
    xb = x.astype(BF16)
    for c0, cw in FFN_CHUNKS:
        gate = _dot(xb, win_ref[:, c0:c0 + cw])
        up = _dot(xb, win_ref[:, D_FF + c0:D_FF + c0 + cw])
        act_ref[:, c0:c0 + cw] = (_silu(gate) * up).astype(BF16)
    for rs in _row_blocks(x_ref.shape[0]):
        y = _dot(act_ref[rs, :], wout_ref[...])
        o_ref[rs, :] = _layer_norm(ALPHA * x_ref[rs, :] + 0.5 * y, g_ref[...], b_ref[...])


def _ffn(x, w_in, w_out, ln_g, ln_b, layer, which, tm):
    t = x.shape[0]
    return pl.pallas_call(
        _ffn_kernel,
        grid=(t // tm,),
        in_specs=[
            pl.BlockSpec((tm, D_MODEL), lambda i: (i, 0)),
            _resident((None, D_MODEL, 2 * D_FF), lambda i: (layer, 0, 0)),
            _resident((None, D_FF, D_MODEL), lambda i: (layer, 0, 0)),
            _resident((None, None, 1, D_MODEL), lambda i: (layer, which, 0, 0)),
            _resident((None, None, 1, D_MODEL), lambda i: (layer, which, 0, 0)),
        ],
        out_specs=pl.BlockSpec((tm, D_MODEL), lambda i: (i, 0)),
        out_shape=jax.ShapeDtypeStruct((t, D_MODEL), F32),
        scratch_shapes=[pltpu.VMEM((tm, D_FF), BF16)],
        compiler_params=_cparams(("parallel",)),
        name="ffn",
    )(x, w_in, w_out, ln_g, ln_b)


def _v_with_ones(vv):
    lane = lax.broadcasted_iota(jnp.int32, vv.shape, 1) % (MLA_PAIR * MLA_HEAD_PAD)
    return jnp.where((lane == V_ONE_EVEN) | (lane == V_ONE_ODD), 1.0, vv).astype(BF16)


def _inproj_kernel(x_ref, tab_ref, wa_ref, wq_ref, wk_ref, wv_ref, qn_ref, kvn_ref,
                   pool_ref, ckv_ref, misc_ref, q_ref, k_ref, v_ref, gla_ref, *, k_transposed):
    xb = x_ref[...].astype(BF16)
    h = _dot(xb, wa_ref[:, 0:A_GLA])
    gla_ref[...] = _dot(xb, wa_ref[:, A_GLA:A_COLS])
    pool_ref[...] = h[:, A_POOL:A_POOL + POOL_WIDTH]

    cm = tab_ref[:, 0:MISC_W]
    sm = tab_ref[:, MISC_W:2 * MISC_W]
    misc = h[:, A_M1:A_M1 + MISC_W] * cm + h[:, A_M2:A_M2 + MISC_W] * sm
    misc_ref[...] = misc

    ckv = _rms_norm(h[:, A_CKV:A_CKV + MLA_KV_RANK], kvn_ref[...])
    ckv_ref[...] = ckv
    ckvb = ckv.astype(BF16)
    lat = jnp.concatenate([ckvb, misc.astype(BF16)], axis=-1)
    if k_transposed:
        kt = _dot_nt(wk_ref[...], lat).astype(BF16)
        k_tile = k_ref.shape[-1]
        for c in range(k_ref.shape[0]):
            k_ref[c] = kt[:, c * k_tile:(c + 1) * k_tile]
    else:
        k_ref[...] = _dot(lat, wk_ref[...]).astype(BF16)
    v_ref[...] = _v_with_ones(_dot(ckvb, wv_ref[...]))

    qn = _rms_norm(h[:, A_QLAT:A_QLAT + MLA_Q_RANK], qn_ref[...]).astype(BF16)
    q2 = _dot(qn, wq_ref[...])
    kw = MLA_HEADS * MLA_HEAD_PAD
    cq = tab_ref[:, 2 * MISC_W:2 * MISC_W + MLA_HEAD_PAD]
    sq = tab_ref[:, 2 * MISC_W + MLA_HEAD_PAD:ROPE_TAB_W]
    for hd in range(MLA_HEADS):
        a = hd * MLA_HEAD_PAD
        q_ref[:, a:a + MLA_HEAD_PAD] = (
            q2[:, a:a + MLA_HEAD_PAD] * cq + q2[:, kw + a:kw + a + MLA_HEAD_PAD] * sq).astype(BF16)


def _inproj(x, tab, w_a, w_q2, w_k, w_v, qnorm, kvnorm, layer, tm, tab_period, k_tile):
    t = x.shape[0]
    kw = MLA_HEADS * MLA_HEAD_PAD
    lat_w = MLA_KV_RANK + MISC_W
    row = lambda i: (i, 0)
    k_transposed = k_tile is not None
    if k_transposed:
        k_shape = jax.ShapeDtypeStruct((t // k_tile, kw, k_tile), BF16)
        k_spec = pl.BlockSpec((tm // k_tile, kw, k_tile), lambda i: (i, 0, 0))
        wk_spec = _resident((None, kw, lat_w), lambda i: (layer, 0, 0))
    else:
        k_shape = jax.ShapeDtypeStruct((t, kw), BF16)
        k_spec = pl.BlockSpec((tm, kw), row)
        wk_spec = _resident((None, lat_w, kw), lambda i: (layer, 0, 0))
    out_shapes = (
        jax.ShapeDtypeStruct((t, POOL_WIDTH), F32),
        jax.ShapeDtypeStruct((t, MLA_KV_RANK), F32),
        jax.ShapeDtypeStruct((t, MISC_W), F32),
        jax.ShapeDtypeStruct((t, kw), BF16),
        k_shape,
        jax.ShapeDtypeStruct((t, kw), BF16),
        jax.ShapeDtypeStruct((t, GLA_IN_W), F32),
    )
    out_specs = [pl.BlockSpec((tm, s.shape[-1]), row) for s in out_shapes]
    out_specs[4] = k_spec
    return pl.pallas_call(
        functools.partial(_inproj_kernel, k_transposed=k_transposed),
        grid=(t // tm,),
        in_specs=[
            pl.BlockSpec((tm, D_MODEL), row),
            pl.BlockSpec((tm, ROPE_TAB_W), lambda i: (i % tab_period, 0)),
            _resident((None, D_MODEL, A_COLS), lambda i: (layer, 0, 0)),
            _resident((None, MLA_Q_RANK, 2 * kw), lambda i: (layer, 0, 0)),
            wk_spec,
            _resident((None, MLA_KV_RANK, kw), lambda i: (layer, 0, 0)),
            _resident((None, 1, MLA_Q_RANK), lambda i: (layer, 0, 0)),
            _resident((None, 1, MLA_KV_RANK), lambda i: (layer, 0, 0)),
        ],
        out_specs=out_specs,
        out_shape=out_shapes,
        compiler_params=_cparams(("parallel",)),
        name="inproj",
    )(x, tab, w_a, w_q2, w_k, w_v, qnorm, kvnorm)


POOL_PAD = 16


def _pool_kernel(u_ref, hist_ref, w_ref, scale_ref, y_ref, ext_ref, *, tp, pos0):
    j = pl.program_id(1)

    @pl.when(j == 0)
    def _():
        ext_ref[0:POOL_PAD, :] = hist_ref[...]

    tok = u_ref[...]
    ext_ref[POOL_PAD:POOL_PAD + tp, :] = tok
    acc = ext_ref[...]
    sums = {}
    w = 1
    while w < max(POOL_WINDOWS):
        acc = acc + pltpu.roll(acc, w, 0)
        w *= 2
        if w in POOL_WINDOWS:
            sums[w] = acc[POOL_PAD:POOL_PAD + tp, :]
    lane = lax.broadcasted_iota(jnp.int32, (tp, POOL_WIDTH), 1)
    pos = pos0 + j * tp + lax.broadcasted_iota(jnp.int32, (tp, POOL_WIDTH), 0)
    total = None
    width = None
    for gi, w in enumerate(POOL_WINDOWS):
        in_group = lane >= gi * POOL_GROUP
        total = sums[w] if total is None else jnp.where(in_group, sums[w], total)
        width = jnp.full_like(pos, w) if width is None else jnp.where(in_group, w, width)
    cnt = jnp.minimum(pos + 1, width).astype(F32)
    pooled = total / cnt - tok
    y = _dot(pooled.astype(BF16), w_ref[...]) * scale_ref[...]
    y_ref[...] = y.astype(BF16)
    ext_ref[0:POOL_PAD, :] = ext_ref[tp:tp + POOL_PAD, :]


def _pool(u, hist16, w_bd, scale, layer, nb, seq, tp, pos0):
    nt = seq // tp
    return pl.pallas_call(
        functools.partial(_pool_kernel, tp=tp, pos0=pos0),
        grid=(nb, nt),
        in_specs=[
            pl.BlockSpec((tp, POOL_WIDTH), lambda b, j: (b * nt + j, 0)),
            pl.BlockSpec((None, POOL_PAD, POOL_WIDTH), lambda b, j: (b, 0, 0)),
            _resident((None, POOL_WIDTH, POOL_WIDTH), lambda b, j: (layer, 0, 0)),
            _resident((None, 1, POOL_WIDTH), lambda b, j: (layer, 0, 0)),
        ],
        out_specs=pl.BlockSpec((tp, POOL_WIDTH), lambda b, j: (b * nt + j, 0)),
        out_shape=jax.ShapeDtypeStruct((nb * seq, POOL_WIDTH), BF16),
        scratch_shapes=[pltpu.VMEM((POOL_PAD + tp, POOL_WIDTH), F32)],
        compiler_params=_cparams(("parallel", "arbitrary")),
        name="pool",
    )(u, hist16, w_bd, scale)


def _pair_output(acc0, acc1):
    lane = lax.broadcasted_iota(jnp.int32, acc0.shape, 1)
    o0 = acc0 / acc0[:, V_ONE_EVEN:V_ONE_EVEN + 1]
    o1 = acc1 / acc1[:, 0:1]
    return jnp.where(lane < MLA_V, o0, o1).astype(BF16)


def _mla_causal_kernel(q_ref, kt_ref, v_ref, o_ref, m_ref, acc_ref, sa_ref, sb_ref, *, tq):
    i = pl.program_id(2)
    m_ref[...] = jnp.full(m_ref.shape, -jnp.inf, F32)
    acc_ref[...] = jnp.zeros(acc_ref.shape, F32)
    reps = tq // V7X_LANES

    def scores(kt, dst_ref):
        for h in range(MLA_GROUP):
            a = h * MLA_HEAD_PAD
            dst_ref[h] = _dot(q_ref[:, a:a + MLA_HEAD_PAD], kt_ref[kt, a:a + MLA_HEAD_PAD, :])

    def consume(src_ref, kt, masked):
        r0 = pl.multiple_of(kt * tq, tq)
        for h in range(MLA_GROUP):
            a = h * MLA_HEAD_PAD
            s = src_ref[h]
            if masked:
                qc = lax.broadcasted_iota(jnp.int32, (tq, tq), 0) // CHUNK
                kc = lax.broadcasted_iota(jnp.int32, (tq, tq), 1) // CHUNK
                s = jnp.where(kc <= qc, s, -jnp.inf)
            m_prev = m_ref[h]
            m_new = jnp.maximum(m_prev, jnp.max(s, axis=-1, keepdims=True))
            p = jnp.exp2(s - jnp.tile(m_new, (1, reps)))
            corr = jnp.exp2(m_prev - m_new)
            acc_ref[h] = corr * acc_ref[h] + _dot(p.astype(BF16),
                                                  v_ref[pl.ds(r0, tq), a:a + MLA_HEAD_PAD])
            m_ref[h] = m_new

    scores(0, sa_ref)

    def body(j, carry):
        scores(2 * j + 1, sb_ref)
        consume(sa_ref, 2 * j, False)
        scores(2 * j + 2, sa_ref)
        consume(sb_ref, 2 * j + 1, False)
        return carry

    lax.fori_loop(0, i // 2, body, 0)

    def finish():
        ow = MLA_PAIR * MLA_V
        for pr in range(MLA_GROUP // MLA_PAIR):
            o_ref[:, pr * ow:(pr + 1) * ow] = _pair_output(acc_ref[MLA_PAIR * pr],
                                                           acc_ref[MLA_PAIR * pr + 1])

    @pl.when(i % 2 == 0)
    def _():
        consume(sa_ref, i, True)
        finish()

    @pl.when(i % 2 == 1)
    def _():
        scores(i, sb_ref)
        consume(sa_ref, i - 1, False)
        consume(sb_ref, i, True)
        finish()


def _mla_causal(q, kt, v, nb, seq, tq):
    nq = seq // tq
    ngrp = MLA_HEADS // MLA_GROUP
    pw = MLA_GROUP * MLA_HEAD_PAD
    return pl.pallas_call(
        functools.partial(_mla_causal_kernel, tq=tq),
        grid=(nb, ngrp, nq),
        in_specs=[
            pl.BlockSpec((tq, pw), lambda b, p, i: (b * nq + i, p)),
            pl.BlockSpec((nq, pw, tq), lambda b, p, i: (b, p, 0)),
            pl.BlockSpec((seq, pw), lambda b, p, i: (b, p)),
        ],
        out_specs=pl.BlockSpec((tq, MLA_GROUP * MLA_V), lambda b, p, i: (b * nq + i, p)),
        out_shape=jax.ShapeDtypeStruct((nb * seq, MLA_HEADS * MLA_V), BF16),
        scratch_shapes=[pltpu.VMEM((MLA_GROUP, tq, V7X_LANES), F32),
                        pltpu.VMEM((MLA_GROUP, tq, MLA_HEAD_PAD), F32),
                        pltpu.VMEM((MLA_GROUP, tq, tq), F32),
                        pltpu.VMEM((MLA_GROUP, tq, tq), F32)],
        compiler_params=_cparams(("parallel", "parallel", "arbitrary")),
        name="mla_causal",
    )(q, kt, v)


def _mla_cached_kernel(q_ref, cp_ref, rp_ref, cn_ref, mn_ref, wabs_ref, wuv_ref, o_ref):
    qa = jnp.concatenate(
        [_dot(q_ref[:, h * MLA_HEAD_PAD:(h + 1) * MLA_HEAD_PAD], wabs_ref[h]) for h in range(MLA_HEADS)],
        axis=0).astype(BF16)
    qa_lat = qa[:, 0:MLA_KV_RANK]
    qa_misc = qa[:, MLA_KV_RANK:]
    cp = cp_ref[...].astype(BF16)
    cn = cn_ref[...].astype(BF16)
    s1 = _dot_nt(qa_lat, cp) + _dot_nt(qa_misc[:, 0:MLA_ROPE], rp_ref[...].astype(BF16))
    s2 = _dot_nt(qa_lat, cn) + _dot_nt(qa_misc, mn_ref[...].astype(BF16))
    m = jnp.maximum(jnp.max(s1, axis=-1, keepdims=True), jnp.max(s2, axis=-1, keepdims=True))
    p1 = jnp.exp2(s1 - m)
    p2 = jnp.exp2(s2 - m)
    l = jnp.sum(p1, axis=-1, keepdims=True) + jnp.sum(p2, axis=-1, keepdims=True)
    lat = ((_dot(p1.astype(BF16), cp) + _dot(p2.astype(BF16), cn)) / l).astype(BF16)
    seq = q_ref.shape[0]
    y = _dot(lat[0:seq], wuv_ref[0])
    for h in range(1, MLA_HEADS):
        y = y + _dot(lat[h * seq:(h + 1) * seq], wuv_ref[h])
    o_ref[...] = y.astype(BF16)


def _mla_cached(q, ckv_new, misc_new, ckv_past, kr_past, w_abs, w_uvp, layer, nb, seq, past):
    kw = MLA_HEADS * MLA_HEAD_PAD
    vw = MLA_HEADS * MLA_V
    lat_w = MLA_KV_RANK + MISC_W
    return pl.pallas_call(
        _mla_cached_kernel,
        grid=(nb,),
        in_specs=[
            pl.BlockSpec((seq, kw), lambda b: (b, 0)),
            pl.BlockSpec((None, None, past, MLA_KV_RANK), lambda b: (layer, b, 0, 0)),
            pl.BlockSpec((None, None, past, MLA_ROPE), lambda b: (layer, b, 0, 0)),
            pl.BlockSpec((seq, MLA_KV_RANK), lambda b: (b, 0)),
            pl.BlockSpec((seq, MISC_W), lambda b: (b, 0)),
            _resident((None, MLA_HEADS, MLA_HEAD_PAD, lat_w), lambda b: (layer, 0, 0, 0)),
            _resident((None, MLA_HEADS, MLA_KV_RANK, vw), lambda b: (layer, 0, 0, 0)),
        ],
        out_specs=pl.BlockSpec((seq, vw), lambda b: (b, 0)),
        out_shape=jax.ShapeDtypeStruct((nb * seq, vw), BF16),
        compiler_params=_cparams(("parallel",)),
        name="mla_cached",
    )(q, ckv_past, kr_past, ckv_new, misc_new, w_abs, w_uvp)


def _gla_kernel(gq_ref, gk_ref, gv_ref, gr_ref, misc_ref, wa2_ref, ba_ref, ng_ref, hsel_ref,
                hblk_ref, s0_ref, y_ref, sfin_ref, st_ref, sec_ref, fst_ref, same_ref, kbd_ref,
                vbd_ref, sbd_ref, o_ref, *, tt):
    j = pl.program_id(1)
    c = min(CHUNK, tt)
    hc = GLA_HEADS * c
    levels = c.bit_length() - 1

    @pl.when(j == 0)
    def _():
        st_ref[...] = s0_ref[...]
        row = lax.broadcasted_iota(jnp.int32, (c, GLA_K), 0)
        trow = lax.broadcasted_iota(jnp.int32, (c, hc), 0)
        tcol = lax.broadcasted_iota(jnp.int32, (c, hc), 1) % c
        for lv in range(levels):
            m = 1 << lv
            sec_ref[lv] = ((row & m) != 0).astype(F32)
            fst_ref[lv] = ((row & m) == 0).astype(F32)
            same_ref[lv] = ((trow // (2 * m)) == (tcol // (2 * m))).astype(F32)
        kbd_ref[...] = (lax.broadcasted_iota(jnp.int32, (hc, GLA_K), 0) // c
                        == lax.broadcasted_iota(jnp.int32, (hc, GLA_K), 1) // GLA_DK).astype(BF16)
        vbd_ref[...] = (lax.broadcasted_iota(jnp.int32, (hc, GLA_V), 0) // c
                        == lax.broadcasted_iota(jnp.int32, (hc, GLA_V), 1) // GLA_DV).astype(BF16)
        sbd_ref[...] = (lax.broadcasted_iota(jnp.int32, (GLA_V, GLA_K), 0) // GLA_DV
                        == lax.broadcasted_iota(jnp.int32, (GLA_V, GLA_K), 1) // GLA_DK).astype(F32)

    z = _dot(misc_ref[...].astype(BF16), wa2_ref[...]) + ba_ref[...]
    la_all = jax.nn.log_sigmoid(z) * (LOG2E / GLA_GATE_TAU)

    st = st_ref[...]
    for r in range(tt // c):
        rs = slice(r * c, (r + 1) * c)
        q = gq_ref[rs, :] * (GLA_DK ** -0.5)
        k = gk_ref[rs, :]
        v = gv_ref[rs, :]
        la = la_all[rs, :]

        scores = jnp.zeros((c, hc), F32)
        p = la
        g = la
        for lv in range(levels):
            m = 1 << lv
            if m < V7X_SUBLANES:
                sec = sec_ref[lv]
                fst = fst_ref[lv]
                qe = (q * jnp.exp2(p) * sec).astype(BF16)
                ke = (k * jnp.exp2(g - p) * fst).astype(BF16)
                ke_bd = jnp.tile(ke, (GLA_HEADS, 1)) * kbd_ref[...]
                scores = scores + _dot_nt(qe, ke_bd) * same_ref[lv]
                up = sec * pltpu.roll(g, m, 0)
                p = p + up
                g = g + up + fst * pltpu.roll(g, c - m, 0)
            else:
                nb2 = c // (2 * m)

                def halves(a):
                    a4 = a.reshape(nb2, 2, m, a.shape[-1])
                    return a4[:, 0], a4[:, 1]

                def join(lo, hi):
                    return jnp.stack([lo, hi], axis=1).reshape(c, lo.shape[-1])

                p_lo, p_hi = halves(p)
                g_lo, g_hi = halves(g)
                qe_hi = (halves(q)[1] * jnp.exp2(p_hi)).reshape(c // 2, GLA_K).astype(BF16)
                ke_lo = halves(k)[0] * jnp.exp2(g_lo - p_lo)
                ke = join(ke_lo, jnp.zeros_like(ke_lo)).astype(BF16)
                ke_bd = jnp.tile(ke, (GLA_HEADS, 1)) * kbd_ref[...]
                a_hi = _dot_nt(qe_hi, ke_bd) * halves(same_ref[lv])[1].reshape(c // 2, hc)
                s_lo, s_hi = halves(scores)
                scores = join(s_lo, s_hi + a_hi.reshape(nb2, m, hc))
                tot = g_lo + g_hi
                p = join(p_lo, p_hi + g_lo)
                g = join(tot, tot)

        o = _dot_nt((q * jnp.exp2(p)).astype(BF16), st.astype(BF16))
        v_bd = jnp.tile(v.astype(BF16), (GLA_HEADS, 1)) * vbd_ref[...]
        o_ref[rs, :] = o + _dot(scores.astype(BF16), v_bd)

        kdec = (k * jnp.exp2(g - p)).astype(BF16)
        upd = _dot(jnp.transpose(v).astype(BF16), kdec)
        st = st * jnp.exp2(g[0:1, :]) + upd * sbd_ref[...]

    st_ref[...] = st
    sfin_ref[...] = st

    qk = (gq_ref[...] * (GLA_DK ** -0.5) * gk_ref[...]).astype(BF16)
    o = o_ref[...] + _dot(qk, hsel_ref[...]) * gv_ref[...]
    o2 = o * o
    o2_hi = o2.astype(BF16)
    o2_lo = (o2 - o2_hi.astype(F32)).astype(BF16)
    ms = (_dot(o2_hi, hblk_ref[...]) + _dot(o2_lo, hblk_ref[...])) * (1.0 / GLA_DV)
    y = o * lax.rsqrt(ms + EPS) * ng_ref[...] * _silu(gr_ref[...])
    y_ref[...] = y.astype(BF16)


def _gla(gla_in, misc, w_a2, b_a, norm_g, hsel, hblk, s0t, layer, nb, seq, tt):
    nt = seq // tt
    row = lambda b, j: (b * nt + j, 0)
    c = min(CHUNK, tt)
    hc = GLA_HEADS * c
    levels = c.bit_length() - 1
    return pl.pallas_call(
        functools.partial(_gla_kernel, tt=tt),
        grid=(nb, nt),
        in_specs=[
            pl.BlockSpec((tt, GLA_K), lambda b, j: (b * nt + j, 0)),
            pl.BlockSpec((tt, GLA_K), lambda b, j: (b * nt + j, 1)),
            pl.BlockSpec((tt, GLA_V), lambda b, j: (b * nt + j, 1)),
            pl.BlockSpec((tt, GLA_V), lambda b, j: (b * nt + j, 2)),
            pl.BlockSpec((tt, MISC_W), row),
            _resident((None, MISC_W, GLA_K), lambda b, j: (layer, 0, 0)),
            _resident((None, 1, GLA_K), lambda b, j: (layer, 0, 0)),
            _resident((None, 1, GLA_V), lambda b, j: (layer, 0, 0)),
            _resident((GLA_K, GLA_V), lambda b, j: (0, 0)),
            _resident((GLA_V, GLA_V), lambda b, j: (0, 0)),
            pl.BlockSpec((None, GLA_V, GLA_K), lambda b, j: (b, 0, 0)),
        ],
        out_specs=[pl.BlockSpec((tt, GLA_V), row),
                   pl.BlockSpec((None, GLA_V, GLA_K), lambda b, j: (b, 0, 0))],
        out_shape=(jax.ShapeDtypeStruct((nb * seq, GLA_V), BF16),
                   jax.ShapeDtypeStruct((nb, GLA_V, GLA_K), F32)),
        scratch_shapes=[pltpu.VMEM((GLA_V, GLA_K), F32),
                        pltpu.VMEM((levels, c, GLA_K), F32),
                        pltpu.VMEM((levels, c, GLA_K), F32),
                        pltpu.VMEM((levels, c, hc), F32),
                        pltpu.VMEM((hc, GLA_K), BF16),
                        pltpu.VMEM((hc, GLA_V), BF16),
                        pltpu.VMEM((GLA_V, GLA_K), F32),
                        pltpu.VMEM((tt, GLA_V), F32)],
        compiler_params=_cparams(("parallel", "arbitrary")),
        name="gla",
    )(gla_in, gla_in, gla_in, gla_in, misc, w_a2, b_a, norm_g, hsel, hblk, s0t)


def _merge_kernel(x_ref, yp_ref, ym_ref, yg_ref, wg_ref, wbp_ref, wbm_ref, wbg_ref, wo_ref,
                  g_ref, b_ref, o_ref):
    for rs in _row_blocks(x_ref.shape[0]):
        x = x_ref[rs, :]
        xb = x.astype(BF16)
        m = jax.nn.sigmoid(_dot(xb, wg_ref[:, 0:D_MODEL])) * _dot(yp_ref[rs, :], wbp_ref[...])
        m = m + (jax.nn.sigmoid(_dot(xb, wg_ref[:, D_MODEL:2 * D_MODEL]))
                 * _dot(ym_ref[rs, :], wbm_ref[...]))
        m = m + (jax.nn.sigmoid(_dot(xb, wg_ref[:, 2 * D_MODEL:3 * D_MODEL]))
                 * _dot(yg_ref[rs, :], wbg_ref[...]))
        y = _dot(m.astype(BF16), wo_ref[...])
        o_ref[rs, :] = _layer_norm(ALPHA * x + y, g_ref[...], b_ref[...])


def _merge(x, y_pool, y_mla, y_gla, w_gates, w_bp, w_bm, w_bg, w_out, ln_g, ln_b, layer, tm):
    t = x.shape[0]
    row = lambda i: (i, 0)
    lay = lambda i: (layer, 0, 0)
    return pl.pallas_call(
        _merge_kernel,
        grid=(t // tm,),
        in_specs=[
            pl.BlockSpec((tm, D_MODEL), row),
            pl.BlockSpec((tm, POOL_WIDTH), row),
            pl.BlockSpec((tm, MLA_HEADS * MLA_V), row),
            pl.BlockSpec((tm, GLA_V), row),
            _resident((None, D_MODEL, 3 * D_MODEL), lay),
            _resident((None, POOL_WIDTH, D_MODEL), lay),
            _resident((None, MLA_HEADS * MLA_V, D_MODEL), lay),
            _resident((None, GLA_V, D_MODEL), lay),
            _resident((None, D_MODEL, D_MODEL), lay),
            _resident((None, None, 1, D_MODEL), lambda i: (layer, 1, 0, 0)),
            _resident((None, None, 1, D_MODEL), lambda i: (layer, 1, 0, 0)),
        ],
        out_specs=pl.BlockSpec((tm, D_MODEL), row),
        out_shape=jax.ShapeDtypeStruct((t, D_MODEL), F32),
        compiler_params=_cparams(("parallel",)),
        name="merge",
    )(x, y_pool, y_mla, y_gla, w_gates, w_bp, w_bm, w_bg, w_out, ln_g, ln_b)


def _memkv_kernel(m_ref, wk_ref, wv_ref, k_ref, v_ref, kb_ref, vb_ref):
    mb = m_ref[...].astype(BF16)
    k = _dot(mb, wk_ref[...])
    v = _dot(mb, wv_ref[...])
    kb_ref[...] = k.astype(BF16)
    vb_ref[...] = v.astype(BF16)
    for h in range(MEM_HEADS):
        a = h * MEM_HEAD_DIM
        k_ref[:, h, :] = k[:, a:a + MEM_HEAD_DIM]
        v_ref[:, h, :] = v[:, a:a + MEM_HEAD_DIM]


def _memkv(mem, w_k, w_v, tm):
    rows = mem.shape[0]
    nl = w_k.shape[0]
    split = pl.BlockSpec((None, tm, MEM_HEADS, MEM_HEAD_DIM), lambda l, i: (l, i, 0, 0))
    flat = pl.BlockSpec((None, tm, D_MODEL), lambda l, i: (l, i, 0))
    return pl.pallas_call(
        _memkv_kernel,
        grid=(nl, rows // tm),
        in_specs=[
            pl.BlockSpec((tm, D_MODEL), lambda l, i: (i, 0)),
            pl.BlockSpec((None, D_MODEL, D_MODEL), lambda l, i: (l, 0, 0)),
            pl.BlockSpec((None, D_MODEL, D_MODEL), lambda l, i: (l, 0, 0)),
        ],
        out_specs=[split, split, flat, flat],
        out_shape=(jax.ShapeDtypeStruct((nl, rows, MEM_HEADS, MEM_HEAD_DIM), F32),
                   jax.ShapeDtypeStruct((nl, rows, MEM_HEADS, MEM_HEAD_DIM), F32),
                   jax.ShapeDtypeStruct((nl, rows, D_MODEL), BF16),
                   jax.ShapeDtypeStruct((nl, rows, D_MODEL), BF16)),
        compiler_params=_cparams(("parallel", "parallel")),
        name="memkv",
    )(mem, w_k, w_v)


def _memattn_kernel(x_ref, k_ref, v_ref, wq_ref, wo_ref, g_ref, b_ref, o_ref):
    x = x_ref[...]
    q = (_dot(x.astype(BF16), wq_ref[...]) * (MEM_HEAD_DIM ** -0.5 * LOG2E)).astype(BF16)
    kb = k_ref[...].astype(BF16)
    vb = v_ref[...].astype(BF16)
    outs = []
    for h in range(MEM_HEADS):
        a = h * MEM_HEAD_DIM
        s = _dot_nt(q[:, a:a + MEM_HEAD_DIM], kb[:, a:a + MEM_HEAD_DIM])
        p = jnp.exp2(s - jnp.max(s, axis=-1, keepdims=True))
        inv_l = 1.0 / jnp.sum(p, axis=-1, keepdims=True)
        outs.append(_dot(p.astype(BF16), vb[:, a:a + MEM_HEAD_DIM]) * inv_l)
    o = jnp.concatenate(outs, axis=-1).astype(BF16)
    for rs in _row_blocks(x_ref.shape[0]):
        y = _dot(o[rs, :], wo_ref[...])
        o_ref[rs, :] = _layer_norm(ALPHA * x_ref[rs, :] + y, g_ref[...], b_ref[...])


def _memattn(x, mem_k, mem_v, kv_layer, w_q, w_o, ln_g, ln_b, layer, nb, seq, tm):
    nt = seq // tm
    n_mem = mem_k.shape[2]
    lay = lambda b, i: (layer, 0, 0)
    return pl.pallas_call(
        _memattn_kernel,
        grid=(nb, nt),
        in_specs=[
            pl.BlockSpec((tm, D_MODEL), lambda b, i: (b * nt + i, 0)),
            pl.BlockSpec((None, None, n_mem, D_MODEL), lambda b, i: (kv_layer, b, 0, 0)),
            pl.BlockSpec((None, None, n_mem, D_MODEL), lambda b, i: (kv_layer, b, 0, 0)),
            _resident((None, D_MODEL, D_MODEL), lay),
            _resident((None, D_MODEL, D_MODEL), lay),
            _resident((None, None, 1, D_MODEL), lambda b, i: (layer, 2, 0, 0)),
            _resident((None, None, 1, D_MODEL), lambda b, i: (layer, 2, 0, 0)),
        ],
        out_specs=pl.BlockSpec((tm, D_MODEL), lambda b, i: (b * nt + i, 0)),
        out_shape=jax.ShapeDtypeStruct((nb * seq, D_MODEL), F32),
        compiler_params=_cparams(("parallel", "parallel")),
        name="memattn",
    )(x, mem_k, mem_v, w_q, w_o, ln_g, ln_b)


def _pack_weights(w_in, pool_w, mla_w_uq, mla_w_uk, mla_w_uv, gla_w_a2):
    nl = w_in.shape[0]
    splits = (POOL_WIDTH, MLA_Q_RANK, MLA_KV_RANK, MLA_ROPE, GLA_K, GLA_K, GLA_V,
              GLA_GATE_RANK, GLA_V, 3 * D_MODEL)
    offs = [0]
    for s in splits:
        offs.append(offs[-1] + s)
    col = lambda n: w_in[:, :, offs[n]:offs[n + 1]]
    half = MLA_ROPE // 2
    kr = col(3)
    kr_sw = jnp.concatenate([kr[..., half:], kr[..., :half]], -1)
    zeros = lambda n: jnp.zeros((nl, D_MODEL, n), w_in.dtype)
    w_a = jnp.concatenate(
        [col(0), col(1), col(2),
         kr, col(7), zeros(MISC_W - MLA_ROPE - GLA_GATE_RANK),
         kr_sw, zeros(MISC_W - MLA_ROPE),
         col(4), col(5), col(6), col(8)], -1).astype(BF16)
    w_gates = col(9).astype(BF16)

    uq = mla_w_uq.reshape(nl, MLA_Q_RANK, MLA_HEADS, MLA_NOPE + MLA_ROPE)
    uq_n, uq_r = uq[..., :MLA_NOPE], uq[..., MLA_NOPE:]
    uq_rs = jnp.concatenate([uq_r[..., half:], uq_r[..., :half]], -1)
    padq = jnp.zeros((nl, MLA_Q_RANK, MLA_HEADS, MLA_HEAD_PAD - MLA_NOPE - MLA_ROPE), uq.dtype)
    q_main = jnp.concatenate([uq_n, uq_r, padq], -1)
    q_swap = jnp.concatenate([jnp.zeros_like(uq_n), uq_rs, padq], -1)
    kw = MLA_HEADS * MLA_HEAD_PAD
    w_q2 = jnp.concatenate([q_main.reshape(nl, MLA_Q_RANK, kw),
                            q_swap.reshape(nl, MLA_Q_RANK, kw)], -1).astype(BF16)

    padk = jnp.zeros((nl, MLA_KV_RANK, MLA_HEADS, MLA_HEAD_PAD - MLA_NOPE), mla_w_uk.dtype)
    k_lat = jnp.concatenate([mla_w_uk, padk], -1).reshape(nl, MLA_KV_RANK, kw)
    place = jnp.zeros((MISC_W, MLA_HEADS, MLA_HEAD_PAD), F32)
    idx = jnp.arange(MLA_ROPE)
    place = place.at[idx, :, MLA_NOPE + idx].set(1.0).reshape(MISC_W, kw)
    k_misc = jnp.broadcast_to(place, (nl, MISC_W, kw))
    w_k = jnp.concatenate([k_lat, k_misc], 1).astype(BF16)
    w_kt = jnp.swapaxes(w_k, 1, 2)
    uv = mla_w_uv.reshape(nl, MLA_KV_RANK, MLA_HEADS // MLA_PAIR, MLA_PAIR, MLA_V)
    padv = jnp.zeros_like(uv[:, :, :, 0])
    w_v = jnp.stack([jnp.concatenate([uv[:, :, :, 0], padv], -1),
                     jnp.concatenate([padv, uv[:, :, :, 1]], -1)], 3)
    w_v = w_v.reshape(nl, MLA_KV_RANK, kw).astype(BF16)

    uk_t = jnp.transpose(mla_w_uk, (0, 2, 3, 1))
    a_lat = jnp.concatenate(
        [uk_t, jnp.zeros((nl, MLA_HEADS, MLA_HEAD_PAD - MLA_NOPE, MLA_KV_RANK), uk_t.dtype)], 2)
    a_misc = jnp.zeros((MLA_HEAD_PAD, MISC_W), F32).at[MLA_NOPE + idx, idx].set(1.0)
    a_misc = jnp.broadcast_to(a_misc, (nl, MLA_HEADS, MLA_HEAD_PAD, MISC_W))
    w_abs = jnp.concatenate([a_lat, a_misc], -1).astype(BF16)
    eye_h = jnp.eye(MLA_HEADS, dtype=mla_w_uv.dtype)
    uv_h = jnp.transpose(mla_w_uv, (0, 2, 1, 3))
    w_uvp = (uv_h[:, :, :, None, :] * eye_h[None, :, None, :, None]).reshape(
        nl, MLA_HEADS, MLA_KV_RANK, MLA_HEADS * MLA_V).astype(BF16)

    ng = len(POOL_WINDOWS)
    eye = jnp.eye(ng, dtype=pool_w.dtype)
    pool_bd = (pool_w[:, :, :, None, :] * eye[None, :, None, :, None]).reshape(
        nl, POOL_WIDTH, POOL_WIDTH).astype(BF16)

    w_a2 = jnp.zeros((nl, MISC_W, GLA_K), gla_w_a2.dtype)
    w_a2 = w_a2.at[:, MLA_ROPE:MLA_ROPE + GLA_GATE_RANK, :].set(gla_w_a2).astype(BF16)
    return w_a, w_gates, w_q2, w_k, w_kt, w_v, w_abs, w_uvp, pool_bd, w_a2


def _rope_table(pos):
    half = MLA_ROPE // 2
    inv = ROPE_BASE ** (-jnp.arange(half, dtype=F32) / half)
    ang = pos.astype(F32)[:, None] * inv
    cos, sin = jnp.cos(ang), jnp.sin(ang)
    n = pos.shape[0]
    one = jnp.ones((n, GLA_GATE_RANK), F32)
    cm = jnp.concatenate([cos, cos, one, jnp.zeros((n, MISC_W - MLA_ROPE - GLA_GATE_RANK), F32)], -1)
    sm = jnp.concatenate([-sin, sin, jnp.zeros((n, MISC_W - MLA_ROPE), F32)], -1)
    padq = jnp.zeros((n, MLA_HEAD_PAD - MLA_NOPE - MLA_ROPE), F32)
    qs = MLA_SCALE * LOG2E
    cq = jnp.concatenate([jnp.ones((n, MLA_NOPE), F32), cos, cos, padq], -1) * qs
    sq = jnp.concatenate([jnp.zeros((n, MLA_NOPE), F32), -sin, sin, padq], -1) * qs
    return jnp.concatenate([cm, sm, cq, sq], -1)


def _state_to_tiles(s):
    eye = jnp.eye(GLA_HEADS, dtype=s.dtype)
    st = jnp.swapaxes(s, 2, 3)
    bd = st[:, :, :, None, :] * eye[None, :, None, :, None]
    return bd.reshape(s.shape[0], GLA_V, GLA_K)


def _tiles_to_state(t):
    b = t.shape[0]
    t5 = t.reshape(b, GLA_HEADS, GLA_DV, GLA_HEADS, GLA_DK)
    idx = jnp.arange(GLA_HEADS)
    diag = t5[:, idx, :, idx, :]
    return jnp.transpose(diag, (1, 0, 3, 2))


def _tile(n, pref):
    return pref if n % pref == 0 else n


def kernel(x_prompt, x_sample, mem_prompt, cache_pool, cache_mla_ckv, cache_mla_krope, state_gla,
           cache_mem_k, cache_mem_v, ln_g, ln_b, w_ffn1_in, w_ffn1_out, w_in, pool_w, pool_scale,
           mla_q_norm, mla_kv_norm, mla_w_uq, mla_w_uk, mla_w_uv, gla_w_a2, gla_b_a, gla_norm,
           w_branch_pool, w_branch_mla, w_branch_gla, w_out, mem_w_q, mem_w_k, mem_w_v, mem_w_o,
           w_ffn2_in, w_ffn2_out):
    nl = w_in.shape[0]
    bp, sp, _ = x_prompt.shape
    bs, ss, _ = x_sample.shape
    past = cache_mla_ckv.shape[2]
    n_mem = mem_prompt.shape[1]

    w_a, w_gates, w_q2, w_k, w_kt, w_v, w_abs, w_uvp, pool_bd, w_a2 = _pack_weights(
        w_in, pool_w, mla_w_uq, mla_w_uk, mla_w_uv, gla_w_a2)
    bf = lambda w: w.astype(BF16)
    ffn_w = ((bf(w_ffn1_in), bf(w_ffn1_out)), (bf(w_ffn2_in), bf(w_ffn2_out)))
    w_bp, w_bm, w_bg, w_o = bf(w_branch_pool), bf(w_branch_mla), bf(w_branch_gla), bf(w_out)
    mw_q, mw_k, mw_v, mw_o = bf(mem_w_q), bf(mem_w_k), bf(mem_w_v), bf(mem_w_o)
    lng = ln_g.reshape(nl, 4, 1, D_MODEL)
    lnb = ln_b.reshape(nl, 4, 1, D_MODEL)
    qnorm = mla_q_norm.reshape(nl, 1, MLA_Q_RANK)
    kvnorm = mla_kv_norm.reshape(nl, 1, MLA_KV_RANK)
    pscale = pool_scale.reshape(nl, 1, POOL_WIDTH)
    b_a = gla_b_a.reshape(nl, 1, GLA_K)
    norm_g = jnp.tile(gla_norm, (1, GLA_HEADS)).reshape(nl, 1, GLA_V)
    hsel = (jnp.arange(GLA_K)[:, None] // GLA_DK == jnp.arange(GLA_V)[None, :] // GLA_DV).astype(BF16)
    hblk = (jnp.arange(GLA_V)[:, None] // GLA_DV == jnp.arange(GLA_V)[None, :] // GLA_DV).astype(BF16)

    tmem = _tile(bp * n_mem, KEY_TILE)
    mem_k_out, mem_v_out, mem_k, mem_v = _memkv(
        mem_prompt.reshape(bp * n_mem, D_MODEL), mw_k, mw_v, tmem)
    mem_k = mem_k.reshape(nl, bp, n_mem, D_MODEL)
    mem_v = mem_v.reshape(nl, bp, n_mem, D_MODEL)
    smem_k = cache_mem_k.reshape(nl, bs, n_mem, D_MODEL)
    smem_v = cache_mem_v.reshape(nl, bs, n_mem, D_MODEL)

    groups = {
        "p": dict(nb=bp, seq=sp, pos0=0, tm=_tile(sp, ROW_TILE)),
        "s": dict(nb=bs, seq=ss, pos0=past, tm=_tile(bs * ss, KEY_TILE)),
    }
    for gname, gr in groups.items():
        seq, tm = gr["seq"], gr["tm"]
        pos = gr["pos0"] + jnp.arange(seq)
        tab = _rope_table(pos)
        if tm <= seq:
            gr["tab"], gr["tab_period"] = tab, seq // tm
        else:
            gr["tab"], gr["tab_period"] = jnp.tile(tab, (tm // seq, 1)), 1
        gr["tseq"] = _tile(seq, KEY_TILE)
        gr["tgla"] = _tile(seq, ROW_TILE)
        gr["tpool"] = _tile(seq, POOL_TILE)
        gr["ttok"] = _tile(gr["nb"] * seq, ROW_TILE)
        gr["tmem"] = _tile(seq, ROW_TILE)

    xs = {"p": x_prompt.reshape(bp * sp, D_MODEL), "s": x_sample.reshape(bs * ss, D_MODEL)}
    zero_hist = jnp.zeros((bp, POOL_PAD, POOL_WIDTH), F32)
    zero_state = jnp.zeros((bp, GLA_V, GLA_K), F32)
    outs = {n: {"pool": [], "ckv": [], "kr": [], "gla": []} for n in groups}

    for l in range(nl):
        for gname, gr in groups.items():
            nb, seq, tm = gr["nb"], gr["seq"], gr["tm"]
            x = xs[gname]
            x = _ffn(x, ffn_w[0][0], ffn_w[0][1], lng, lnb, l, 0, gr["ttok"])
            k_tile = gr["tseq"] if gname == "p" else None
            u_pool, ckv, misc, q, k, v, gla_in = _inproj(
                x, gr["tab"], w_a, w_q2, w_k if k_tile is None else w_kt, w_v, qnorm, kvnorm,
                l, tm, gr["tab_period"], k_tile)
            if gname == "p":
                hist = None
                hist16 = zero_hist
                s0t = zero_state
            else:
                hist = cache_pool[l]
                hist16 = jnp.concatenate(
                    [jnp.zeros((nb, POOL_PAD - POOL_HIST, POOL_WIDTH), F32), hist], 1)
                s0t = _state_to_tiles(state_gla[l])
            y_pool = _pool(u_pool, hist16, pool_bd, pscale, l, nb, seq, gr["tpool"], gr["pos0"])
            if gname == "p":
                y_mla = _mla_causal(q, k, v, nb, seq, gr["tseq"])
            else:
                y_mla = _mla_cached(q, ckv, misc, cache_mla_ckv, cache_mla_krope, w_abs, w_uvp,
                                    l, nb, seq, past)
            y_gla, s_fin = _gla(gla_in, misc, w_a2, b_a, norm_g, hsel, hblk, s0t, l, nb, seq, gr["tgla"])
            x = _merge(x, y_pool, y_mla, y_gla, w_gates, w_bp, w_bm, w_bg, w_o, lng, lnb, l,
                       gr["ttok"])
            if gname == "p":
                x = _memattn(x, mem_k, mem_v, l, mw_q, mw_o, lng, lnb, l, nb, seq, gr["tmem"])
            else:
                x = _memattn(x, smem_k, smem_v, l, mw_q, mw_o, lng, lnb, l, nb, seq, gr["tmem"])
            x = _ffn(x, ffn_w[1][0], ffn_w[1][1], lng, lnb, l, 3, gr["ttok"])
            xs[gname] = x

            u3 = u_pool.reshape(nb, seq, POOL_WIDTH)
            if hist is None:
                hist = jnp.zeros((nb, POOL_HIST, POOL_WIDTH), F32)
            new_pool = jnp.concatenate([hist, u3], 1)[:, -POOL_HIST:]
            o = outs[gname]
            o["pool"].append(new_pool)
            o["ckv"].append(ckv.reshape(nb, seq, MLA_KV_RANK))
            o["kr"].append(misc[:, :MLA_ROPE].reshape(nb, seq, MLA_ROPE))
            o["gla"].append(_tiles_to_state(s_fin))

    st = lambda n, key: jnp.stack(outs[n][key], 0)
    return (xs["p"].reshape(bp, sp, D_MODEL), xs["s"].reshape(bs, ss, D_MODEL),
            st("p", "pool"), st("s", "pool"),
            st("p", "ckv"), st("s", "ckv"),
            st("p", "kr"), st("s", "kr"),
            st("p", "gla"), st("s", "gla"),
            mem_k_out.reshape(nl, bp, n_mem, MEM_HEADS, MEM_HEAD_DIM),
            mem_v_out.reshape(nl, bp, n_mem, MEM_HEADS, MEM_HEAD_DIM))
```

```python
import functools
import math

import jax
import jax.numpy as jnp
from jax import lax
from jax.experimental import pallas as pl
from jax.experimental.pallas import tpu as pltpu

F32 = jnp.float32
BF16 = jnp.bfloat16

D_MODEL = 1024
DEPTH = 4
CHUNK = 64
EPS = 1e-5
ALPHA = (2 * DEPTH) ** 0.25

POOL_WINDOWS = (2, 4, 8, 16)
POOL_GROUP = D_MODEL // 16
POOL_WIDTH = len(POOL_WINDOWS) * POOL_GROUP
POOL_HIST = max(POOL_WINDOWS) - 1

MLA_HEADS = 8
MLA_Q_RANK = D_MODEL // 4
MLA_KV_RANK = D_MODEL // 4
MLA_NOPE = 64
MLA_ROPE = 32
MLA_V = 64
MLA_SCALE = (MLA_NOPE + MLA_ROPE) ** -0.5
ROPE_BASE = 10000.0
MLA_HEAD_PAD = 128
MLA_PAIR = 2
MLA_GROUP = 4
V_ONE_EVEN = MLA_V
V_ONE_ODD = MLA_HEAD_PAD
LOG2E = math.log2(math.e)

GLA_HEADS = 4
GLA_DK = 32
GLA_DV = 64
GLA_GATE_RANK = 16
GLA_GATE_TAU = 16.0
GLA_K = GLA_HEADS * GLA_DK
GLA_V = GLA_HEADS * GLA_DV

MEM_HEADS = 4
MEM_HEAD_DIM = D_MODEL // MEM_HEADS
D_FF = 2816

V7X_LANES = 128
V7X_SUBLANES = 8
V7X_VMEM_BYTES = 64 * 1024 * 1024
VMEM_LIMIT = V7X_VMEM_BYTES - 8 * 1024 * 1024

A_POOL = 0
A_QLAT = 256
A_CKV = 512
A_M1 = 768
A_M2 = 896
A_GLA = 1024
A_COLS = 1792
MISC_W = 128
GLA_IN_W = 2 * GLA_K + 2 * GLA_V
ROPE_TAB_W = 2 * MISC_W + 2 * MLA_HEAD_PAD


def _cparams(sem):
    return pltpu.CompilerParams(dimension_semantics=sem, vmem_limit_bytes=VMEM_LIMIT)


def _resident(shape, index_map):
    return pl.BlockSpec(shape, index_map, pipeline_mode=pl.Buffered(1))


def _layer_norm(y, g, b):
    mu = jnp.mean(y, axis=-1, keepdims=True)
    d = y - mu
    var = jnp.mean(d * d, axis=-1, keepdims=True)
    return d * lax.rsqrt(var + EPS) * g + b


def _rms_norm(y, g):
    return y * lax.rsqrt(jnp.mean(y * y, axis=-1, keepdims=True) + EPS) * g


def _dot(a, b):
    return jnp.dot(a, b, preferred_element_type=F32)


def _dot_nt(a, b):
    return lax.dot_general(a, b, (((1,), (1,)), ((), ())), preferred_element_type=F32)


def _sigmoid(x):
    return 0.5 * jnp.tanh(0.5 * x) + 0.5


def _silu(x):
    return x * _sigmoid(x)


ROW_TILE = 1024
KEY_TILE = 512
POOL_TILE = 2048
ROW_SUB = 256


def _row_blocks(rows):
    sub = min(ROW_SUB, rows)
    return [slice(r, r + sub) for r in range(0, rows, sub)]


FFN_CHUNKS = ((0, 1024), (1024, 1024), (2048, 768))


def _ffn_kernel(x_ref, win_ref, wout_ref, g_ref, b_ref, o_ref, act_ref):
    x = x_ref[...]
    xb = x.astype(BF16)
    for c0, cw in FFN_CHUNKS:
        gate = _dot(xb, win_ref[:, c0:c0 + cw])
        up = _dot(xb, win_ref[:, D_FF + c0:D_FF + c0 + cw])
        act_ref[:, c0:c0 + cw] = (_silu(gate) * up).astype(BF16)
    for rs in _row_blocks(x_ref.shape[0]):
        y = _dot(act_ref[rs, :], wout_ref[...])
        o_ref[rs, :] = _layer_norm(ALPHA * x_ref[rs, :] + 0.5 * y, g_ref[...], b_ref[...])


def _ffn(x, w_in, w_out, ln_g, ln_b, layer, which, tm):
    t = x.shape[0]
    return pl.pallas_call(
        _ffn_kernel,
        grid=(t // tm,),
        in_specs=[
            pl.BlockSpec((tm, D_MODEL), lambda i: (i, 0)),
            _resident((None, D_MODEL, 2 * D_FF), lambda i: (layer, 0, 0)),
            _resident((None, D_FF, D_MODEL), lambda i: (layer, 0, 0)),
            _resident((None, None, 1, D_MODEL), lambda i: (layer, which, 0, 0)),
            _resident((None, None, 1, D_MODEL), lambda i: (layer, which, 0, 0)),
        ],
        out_specs=pl.BlockSpec((tm, D_MODEL), lambda i: (i, 0)),
        out_shape=jax.ShapeDtypeStruct((t, D_MODEL), F32),
        scratch_shapes=[pltpu.VMEM((tm, D_FF), BF16)],
        compiler_params=_cparams(("parallel",)),
        name="ffn",
    )(x, w_in, w_out, ln_g, ln_b)


def _v_with_ones(vv):
    lane = lax.broadcasted_iota(jnp.int32, vv.shape, 1) % (MLA_PAIR * MLA_HEAD_PAD)
    return jnp.where((lane == V_ONE_EVEN) | (lane == V_ONE_ODD), 1.0, vv).astype(BF16)


def _inproj_kernel(x_ref, tab_ref, wa_ref, wq_ref, wk_ref, wv_ref, qn_ref, kvn_ref,
                   pool_ref, ckv_ref, misc_ref, q_ref, k_ref, v_ref, gla_ref, *, k_transposed):
    xb = x_ref[...].astype(BF16)
    h = _dot(xb, wa_ref[:, 0:A_GLA])
    gla_ref[...] = _dot(xb, wa_ref[:, A_GLA:A_COLS])
    pool_ref[...] = h[:, A_POOL:A_POOL + POOL_WIDTH]

    cm = tab_ref[:, 0:MISC_W]
    sm = tab_ref[:, MISC_W:2 * MISC_W]
    misc = h[:, A_M1:A_M1 + MISC_W] * cm + h[:, A_M2:A_M2 + MISC_W] * sm
    misc_ref[...] = misc

    ckv = _rms_norm(h[:, A_CKV:A_CKV + MLA_KV_RANK], kvn_ref[...])
    ckv_ref[...] = ckv
    ckvb = ckv.astype(BF16)
    lat = jnp.concatenate([ckvb, misc.astype(BF16)], axis=-1)
    if k_transposed:
        kt = _dot_nt(wk_ref[...], lat).astype(BF16)
        k_tile = k_ref.shape[-1]
        for c in range(k_ref.shape[0]):
            k_ref[c] = kt[:, c * k_tile:(c + 1) * k_tile]
    else:
        k_ref[...] = _dot(lat, wk_ref[...]).astype(BF16)
    v_ref[...] = _v_with_ones(_dot(ckvb, wv_ref[...]))

    qn = _rms_norm(h[:, A_QLAT:A_QLAT + MLA_Q_RANK], qn_ref[...]).astype(BF16)
    q2 = _dot(qn, wq_ref[...])
    kw = MLA_HEADS * MLA_HEAD_PAD
    cq = tab_ref[:, 2 * MISC_W:2 * MISC_W + MLA_HEAD_PAD]
    sq = tab_ref[:, 2 * MISC_W + MLA_HEAD_PAD:ROPE_TAB_W]
    for hd in range(MLA_HEADS):
        a = hd * MLA_HEAD_PAD
        q_ref[:, a:a + MLA_HEAD_PAD] = (
            q2[:, a:a + MLA_HEAD_PAD] * cq + q2[:, kw + a:kw + a + MLA_HEAD_PAD] * sq).astype(BF16)


def _inproj(x, tab, w_a, w_q2, w_k, w_v, qnorm, kvnorm, layer, tm, tab_period, k_tile):
    t = x.shape[0]
    kw = MLA_HEADS * MLA_HEAD_PAD
    lat_w = MLA_KV_RANK + MISC_W
    row = lambda i: (i, 0)
    k_transposed = k_tile is not None
    if k_transposed:
        k_shape = jax.ShapeDtypeStruct((t // k_tile, kw, k_tile), BF16)
        k_spec = pl.BlockSpec((tm // k_tile, kw, k_tile), lambda i: (i, 0, 0))
        wk_spec = _resident((None, kw, lat_w), lambda i: (layer, 0, 0))
    else:
        k_shape = jax.ShapeDtypeStruct((t, kw), BF16)
        k_spec = pl.BlockSpec((tm, kw), row)
        wk_spec = _resident((None, lat_w, kw), lambda i: (layer, 0, 0))
    out_shapes = (
        jax.ShapeDtypeStruct((t, POOL_WIDTH), F32),
        jax.ShapeDtypeStruct((t, MLA_KV_RANK), F32),
        jax.ShapeDtypeStruct((t, MISC_W), F32),
        jax.ShapeDtypeStruct((t, kw), BF16),
        k_shape,
        jax.ShapeDtypeStruct((t, kw), BF16),
        jax.ShapeDtypeStruct((t, GLA_IN_W), F32),
    )
    out_specs = [pl.BlockSpec((tm, s.shape[-1]), row) for s in out_shapes]
    out_specs[4] = k_spec
    return pl.pallas_call(
        functools.partial(_inproj_kernel, k_transposed=k_transposed),
        grid=(t // tm,),
        in_specs=[
            pl.BlockSpec((tm, D_MODEL), row),
            pl.BlockSpec((tm, ROPE_TAB_W), lambda i: (i % tab_period, 0)),
            _resident((None, D_MODEL, A_COLS), lambda i: (layer, 0, 0)),
            _resident((None, MLA_Q_RANK, 2 * kw), lambda i: (layer, 0, 0)),
            wk_spec,
            _resident((None, MLA_KV_RANK, kw), lambda i: (layer, 0, 0)),
            _resident((None, 1, MLA_Q_RANK), lambda i: (layer, 0, 0)),
            _resident((None, 1, MLA_KV_RANK), lambda i: (layer, 0, 0)),
        ],
        out_specs=out_specs,
        out_shape=out_shapes,
        compiler_params=_cparams(("parallel",)),
        name="inproj",
    )(x, tab, w_a, w_q2, w_k, w_v, qnorm, kvnorm)


POOL_PAD = 16


def _pool_kernel(u_ref, hist_ref, w_ref, scale_ref, y_ref, ext_ref, *, tp, pos0):
    j = pl.program_id(1)

    @pl.when(j == 0)
    def _():
        ext_ref[0:POOL_PAD, :] = hist_ref[...]

    tok = u_ref[...]
    ext_ref[POOL_PAD:POOL_PAD + tp, :] = tok
    acc = ext_ref[...]
    sums = {}
    w = 1
    while w < max(POOL_WINDOWS):
        acc = acc + pltpu.roll(acc, w, 0)
        w *= 2
        if w in POOL_WINDOWS:
            sums[w] = acc[POOL_PAD:POOL_PAD + tp, :]
    lane = lax.broadcasted_iota(jnp.int32, (tp, POOL_WIDTH), 1)
    pos = pos0 + j * tp + lax.broadcasted_iota(jnp.int32, (tp, POOL_WIDTH), 0)
    total = None
    width = None
    for gi, w in enumerate(POOL_WINDOWS):
        in_group = lane >= gi * POOL_GROUP
        total = sums[w] if total is None else jnp.where(in_group, sums[w], total)
        width = jnp.full_like(pos, w) if width is None else jnp.where(in_group, w, width)
    cnt = jnp.minimum(pos + 1, width).astype(F32)
    pooled = total / cnt - tok
    y = _dot(pooled.astype(BF16), w_ref[...]) * scale_ref[...]
    y_ref[...] = y.astype(BF16)
    ext_ref[0:POOL_PAD, :] = ext_ref[tp:tp + POOL_PAD, :]


def _pool(u, hist16, w_bd, scale, layer, nb, seq, tp, pos0):
    nt = seq // tp
    return pl.pallas_call(
        functools.partial(_pool_kernel, tp=tp, pos0=pos0),
        grid=(nb, nt),
        in_specs=[
            pl.BlockSpec((tp, POOL_WIDTH), lambda b, j: (b * nt + j, 0)),
            pl.BlockSpec((None, POOL_PAD, POOL_WIDTH), lambda b, j: (b, 0, 0)),
            _resident((None, POOL_WIDTH, POOL_WIDTH), lambda b, j: (layer, 0, 0)),
            _resident((None, 1, POOL_WIDTH), lambda b, j: (layer, 0, 0)),
        ],
        out_specs=pl.BlockSpec((tp, POOL_WIDTH), lambda b, j: (b * nt + j, 0)),
        out_shape=jax.ShapeDtypeStruct((nb * seq, POOL_WIDTH), BF16),
        scratch_shapes=[pltpu.VMEM((POOL_PAD + tp, POOL_WIDTH), F32)],
        compiler_params=_cparams(("parallel", "arbitrary")),
        name="pool",
    )(u, hist16, w_bd, scale)


def _pair_output(acc0, acc1):
    lane = lax.broadcasted_iota(jnp.int32, acc0.shape, 1)
    o0 = acc0 / acc0[:, V_ONE_EVEN:V_ONE_EVEN + 1]
    o1 = acc1 / acc1[:, 0:1]
    return jnp.where(lane < MLA_V, o0, o1).astype(BF16)


def _mla_causal_kernel(q_ref, kt_ref, v_ref, o_ref, m_ref, acc_ref, sa_ref, sb_ref, *, tq):
    i = pl.program_id(2)
    m_ref[...] = jnp.full(m_ref.shape, -jnp.inf, F32)
    acc_ref[...] = jnp.zeros(acc_ref.shape, F32)
    reps = tq // V7X_LANES

    def scores(kt, dst_ref):
        for h in range(MLA_GROUP):
            a = h * MLA_HEAD_PAD
            dst_ref[h] = _dot(q_ref[:, a:a + MLA_HEAD_PAD], kt_ref[kt, a:a + MLA_HEAD_PAD, :])

    def consume(src_ref, kt, masked):
        r0 = pl.multiple_of(kt * tq, tq)
        for h in range(MLA_GROUP):
            a = h * MLA_HEAD_PAD
            s = src_ref[h]
            if masked:
                qc = lax.broadcasted_iota(jnp.int32, (tq, tq), 0) // CHUNK
                kc = lax.broadcasted_iota(jnp.int32, (tq, tq), 1) // CHUNK
                s = jnp.where(kc <= qc, s, -jnp.inf)
            m_prev = m_ref[h]
            m_new = jnp.maximum(m_prev, jnp.max(s, axis=-1, keepdims=True))
            p = jnp.exp2(s - jnp.tile(m_new, (1, reps)))
            corr = jnp.exp2(m_prev - m_new)
            acc_ref[h] = corr * acc_ref[h] + _dot(p.astype(BF16),
                                                  v_ref[pl.ds(r0, tq), a:a + MLA_HEAD_PAD])
            m_ref[h] = m_new

    scores(0, sa_ref)

    def body(j, carry):
        scores(2 * j + 1, sb_ref)
        consume(sa_ref, 2 * j, False)
        scores(2 * j + 2, sa_ref)
        consume(sb_ref, 2 * j + 1, False)
        return carry

    lax.fori_loop(0, i // 2, body, 0)

    def finish():
        ow = MLA_PAIR * MLA_V
        for pr in range(MLA_GROUP // MLA_PAIR):
            o_ref[:, pr * ow:(pr + 1) * ow] = _pair_output(acc_ref[MLA_PAIR * pr],
                                                           acc_ref[MLA_PAIR * pr + 1])

    @pl.when(i % 2 == 0)
    def _():
        consume(sa_ref, i, True)
        finish()

    @pl.when(i % 2 == 1)
    def _():
        scores(i, sb_ref)
        consume(sa_ref, i - 1, False)
        consume(sb_ref, i, True)
        finish()


def _mla_causal(q, kt, v, nb, seq, tq):
    nq = seq // tq
    ngrp = MLA_HEADS // MLA_GROUP
    pw = MLA_GROUP * MLA_HEAD_PAD
    return pl.pallas_call(
        functools.partial(_mla_causal_kernel, tq=tq),
        grid=(nb, ngrp, nq),
        in_specs=[
            pl.BlockSpec((tq, pw), lambda b, p, i: (b * nq + i, p)),
            pl.BlockSpec((nq, pw, tq), lambda b, p, i: (b, p, 0)),
            pl.BlockSpec((seq, pw), lambda b, p, i: (b, p)),
        ],
        out_specs=pl.BlockSpec((tq, MLA_GROUP * MLA_V), lambda b, p, i: (b * nq + i, p)),
        out_shape=jax.ShapeDtypeStruct((nb * seq, MLA_HEADS * MLA_V), BF16),
        scratch_shapes=[pltpu.VMEM((MLA_GROUP, tq, V7X_LANES), F32),
                        pltpu.VMEM((MLA_GROUP, tq, MLA_HEAD_PAD), F32),
                        pltpu.VMEM((MLA_GROUP, tq, tq), F32),
                        pltpu.VMEM((MLA_GROUP, tq, tq), F32)],
        compiler_params=_cparams(("parallel", "parallel", "arbitrary")),
        name="mla_causal",
    )(q, kt, v)


def _mla_cached_kernel(q_ref, cp_ref, rp_ref, cn_ref, mn_ref, wabs_ref, wuv_ref, o_ref):
    qa = jnp.concatenate(
        [_dot(q_ref[:, h * MLA_HEAD_PAD:(h + 1) * MLA_HEAD_PAD], wabs_ref[h]) for h in range(MLA_HEADS)],
        axis=0).astype(BF16)
    qa_lat = qa[:, 0:MLA_KV_RANK]
    qa_misc = qa[:, MLA_KV_RANK:]
    cp = cp_ref[...].astype(BF16)
    cn = cn_ref[...].astype(BF16)
    s1 = _dot_nt(qa_lat, cp) + _dot_nt(qa_misc[:, 0:MLA_ROPE], rp_ref[...].astype(BF16))
    s2 = _dot_nt(qa_lat, cn) + _dot_nt(qa_misc, mn_ref[...].astype(BF16))
    m = jnp.maximum(jnp.max(s1, axis=-1, keepdims=True), jnp.max(s2, axis=-1, keepdims=True))
    p1 = jnp.exp2(s1 - m)
    p2 = jnp.exp2(s2 - m)
    l = jnp.sum(p1, axis=-1, keepdims=True) + jnp.sum(p2, axis=-1, keepdims=True)
    lat = ((_dot(p1.astype(BF16), cp) + _dot(p2.astype(BF16), cn)) / l).astype(BF16)
    seq = q_ref.shape[0]
    y = _dot(lat[0:seq], wuv_ref[0])
    for h in range(1, MLA_HEADS):
        y = y + _dot(lat[h * seq:(h + 1) * seq], wuv_ref[h])
    o_ref[...] = y.astype(BF16)


def _mla_cached(q, ckv_new, misc_new, ckv_past, kr_past, w_abs, w_uvp, layer, nb, seq, past):
    kw = MLA_HEADS * MLA_HEAD_PAD
    vw = MLA_HEADS * MLA_V
    lat_w = MLA_KV_RANK + MISC_W
    return pl.pallas_call(
        _mla_cached_kernel,
        grid=(nb,),
        in_specs=[
            pl.BlockSpec((seq, kw), lambda b: (b, 0)),
            pl.BlockSpec((None, None, past, MLA_KV_RANK), lambda b: (layer, b, 0, 0)),
            pl.BlockSpec((None, None, past, MLA_ROPE), lambda b: (layer, b, 0, 0)),
            pl.BlockSpec((seq, MLA_KV_RANK), lambda b: (b, 0)),
            pl.BlockSpec((seq, MISC_W), lambda b: (b, 0)),
            _resident((None, MLA_HEADS, MLA_HEAD_PAD, lat_w), lambda b: (layer, 0, 0, 0)),
            _resident((None, MLA_HEADS, MLA_KV_RANK, vw), lambda b: (layer, 0, 0, 0)),
        ],
        out_specs=pl.BlockSpec((seq, vw), lambda b: (b, 0)),
        out_shape=jax.ShapeDtypeStruct((nb * seq, vw), BF16),
        compiler_params=_cparams(("parallel",)),
        name="mla_cached",
    )(q, ckv_past, kr_past, ckv_new, misc_new, w_abs, w_uvp)


def _gla_kernel(gq_ref, gk_ref, gv_ref, gr_ref, misc_ref, wa2_ref, ba_ref, ng_ref, hsel_ref,
                hblk_ref, s0_ref, y_ref, sfin_ref, st_ref, sec_ref, fst_ref, same_ref, kbd_ref,
                vbd_ref, sbd_ref, o_ref, *, tt):
    j = pl.program_id(1)
    c = min(CHUNK, tt)
    hc = GLA_HEADS * c
    levels = c.bit_length() - 1

    @pl.when(j == 0)
    def _():
        st_ref[...] = s0_ref[...]
        row = lax.broadcasted_iota(jnp.int32, (c, GLA_K), 0)
        trow = lax.broadcasted_iota(jnp.int32, (c, hc), 0)
        tcol = lax.broadcasted_iota(jnp.int32, (c, hc), 1) % c
        for lv in range(levels):
            m = 1 << lv
            sec_ref[lv] = ((row & m) != 0).astype(F32)
            fst_ref[lv] = ((row & m) == 0).astype(F32)
            same_ref[lv] = ((trow // (2 * m)) == (tcol // (2 * m))).astype(F32)
        kbd_ref[...] = (lax.broadcasted_iota(jnp.int32, (hc, GLA_K), 0) // c
                        == lax.broadcasted_iota(jnp.int32, (hc, GLA_K), 1) // GLA_DK).astype(BF16)
        vbd_ref[...] = (lax.broadcasted_iota(jnp.int32, (hc, GLA_V), 0) // c
                        == lax.broadcasted_iota(jnp.int32, (hc, GLA_V), 1) // GLA_DV).astype(BF16)
        sbd_ref[...] = (lax.broadcasted_iota(jnp.int32, (GLA_V, GLA_K), 0) // GLA_DV
                        == lax.broadcasted_iota(jnp.int32, (GLA_V, GLA_K), 1) // GLA_DK).astype(F32)

    z = _dot(misc_ref[...].astype(BF16), wa2_ref[...]) + ba_ref[...]
    la_all = jax.nn.log_sigmoid(z) * (LOG2E / GLA_GATE_TAU)

    st = st_ref[...]
    for r in range(tt // c):
        rs = slice(r * c, (r + 1) * c)
        q = gq_ref[rs, :] * (GLA_DK ** -0.5)
        k = gk_ref[rs, :]
        v = gv_ref[rs, :]
        la = la_all[rs, :]

        scores = jnp.zeros((c, hc), F32)
        p = la
        g = la
        for lv in range(levels):
            m = 1 << lv
            if m < V7X_SUBLANES:
                sec = sec_ref[lv]
                fst = fst_ref[lv]
                qe = (q * jnp.exp2(p) * sec).astype(BF16)
                ke = (k * jnp.exp2(g - p) * fst).astype(BF16)
                ke_bd = jnp.tile(ke, (GLA_HEADS, 1)) * kbd_ref[...]
                scores = scores + _dot_nt(qe, ke_bd) * same_ref[lv]
                up = sec * pltpu.roll(g, m, 0)
                p = p + up
                g = g + up + fst * pltpu.roll(g, c - m, 0)
            else:
                nb2 = c // (2 * m)

                def halves(a):
                    a4 = a.reshape(nb2, 2, m, a.shape[-1])
                    return a4[:, 0], a4[:, 1]

                def join(lo, hi):
                    return jnp.stack([lo, hi], axis=1).reshape(c, lo.shape[-1])

                p_lo, p_hi = halves(p)
                g_lo, g_hi = halves(g)
                qe_hi = (halves(q)[1] * jnp.exp2(p_hi)).reshape(c // 2, GLA_K).astype(BF16)
                ke_lo = halves(k)[0] * jnp.exp2(g_lo - p_lo)
                ke = join(ke_lo, jnp.zeros_like(ke_lo)).astype(BF16)
                ke_bd = jnp.tile(ke, (GLA_HEADS, 1)) * kbd_ref[...]
                a_hi = _dot_nt(qe_hi, ke_bd) * halves(same_ref[lv])[1].reshape(c // 2, hc)
                s_lo, s_hi = halves(scores)
                scores = join(s_lo, s_hi + a_hi.reshape(nb2, m, hc))
                tot = g_lo + g_hi
                p = join(p_lo, p_hi + g_lo)
                g = join(tot, tot)

        o = _dot_nt((q * jnp.exp2(p)).astype(BF16), st.astype(BF16))
        v_bd = jnp.tile(v.astype(BF16), (GLA_HEADS, 1)) * vbd_ref[...]
        o_ref[rs, :] = o + _dot(scores.astype(BF16), v_bd)

        kdec = (k * jnp.exp2(g - p)).astype(BF16)
        upd = _dot(jnp.transpose(v).astype(BF16), kdec)
        st = st * jnp.exp2(g[0:1, :]) + upd * sbd_ref[...]

    st_ref[...] = st
    sfin_ref[...] = st

    qk = (gq_ref[...] * (GLA_DK ** -0.5) * gk_ref[...]).astype(BF16)
    o = o_ref[...] + _dot(qk, hsel_ref[...]) * gv_ref[...]
    o2 = o * o
    o2_hi = o2.astype(BF16)
    o2_lo = (o2 - o2_hi.astype(F32)).astype(BF16)
    ms = (_dot(o2_hi, hblk_ref[...]) + _dot(o2_lo, hblk_ref[...])) * (1.0 / GLA_DV)
    y = o * lax.rsqrt(ms + EPS) * ng_ref[...] * _silu(gr_ref[...])
    y_ref[...] = y.astype(BF16)


def _gla(gla_in, misc, w_a2, b_a, norm_g, hsel, hblk, s0t, layer, nb, seq, tt):
    nt = seq // tt
    row = lambda b, j: (b * nt + j, 0)
    c = min(CHUNK, tt)
    hc = GLA_HEADS * c
    levels = c.bit_length() - 1
    return pl.pallas_call(
        functools.partial(_gla_kernel, tt=tt),
        grid=(nb, nt),
        in_specs=[
            pl.BlockSpec((tt, GLA_K), lambda b, j: (b * nt + j, 0)),
            pl.BlockSpec((tt, GLA_K), lambda b, j: (b * nt + j, 1)),
            pl.BlockSpec((tt, GLA_V), lambda b, j: (b * nt + j, 1)),
            pl.BlockSpec((tt, GLA_V), lambda b, j: (b * nt + j, 2)),
            pl.BlockSpec((tt, MISC_W), row),
            _resident((None, MISC_W, GLA_K), lambda b, j: (layer, 0, 0)),
            _resident((None, 1, GLA_K), lambda b, j: (layer, 0, 0)),
            _resident((None, 1, GLA_V), lambda b, j: (layer, 0, 0)),
            _resident((GLA_K, GLA_V), lambda b, j: (0, 0)),
            _resident((GLA_V, GLA_V), lambda b, j: (0, 0)),
            pl.BlockSpec((None, GLA_V, GLA_K), lambda b, j: (b, 0, 0)),
        ],
        out_specs=[pl.BlockSpec((tt, GLA_V), row),
                   pl.BlockSpec((None, GLA_V, GLA_K), lambda b, j: (b, 0, 0))],
        out_shape=(jax.ShapeDtypeStruct((nb * seq, GLA_V), BF16),
                   jax.ShapeDtypeStruct((nb, GLA_V, GLA_K), F32)),
        scratch_shapes=[pltpu.VMEM((GLA_V, GLA_K), F32),
                        pltpu.VMEM((levels, c, GLA_K), F32),
                        pltpu.VMEM((levels, c, GLA_K), F32),
                        pltpu.VMEM((levels, c, hc), F32),
                        pltpu.VMEM((hc, GLA_K), BF16),
                        pltpu.VMEM((hc, GLA_V), BF16),
                        pltpu.VMEM((GLA_V, GLA_K), F32),
                        pltpu.VMEM((tt, GLA_V), F32)],
        compiler_params=_cparams(("parallel", "arbitrary")),
        name="gla",
    )(gla_in, gla_in, gla_in, gla_in, misc, w_a2, b_a, norm_g, hsel, hblk, s0t)


def _merge_kernel(x_ref, yp_ref, ym_ref, yg_ref, wg_ref, wbp_ref, wbm_ref, wbg_ref, wo_ref,
                  g_ref, b_ref, o_ref):
    for rs in _row_blocks(x_ref.shape[0]):
        x = x_ref[rs, :]
        xb = x.astype(BF16)
        m = _sigmoid(_dot(xb, wg_ref[:, 0:D_MODEL])) * _dot(yp_ref[rs, :], wbp_ref[...])
        m = m + (_sigmoid(_dot(xb, wg_ref[:, D_MODEL:2 * D_MODEL]))
                 * _dot(ym_ref[rs, :], wbm_ref[...]))
        m = m + (_sigmoid(_dot(xb, wg_ref[:, 2 * D_MODEL:3 * D_MODEL]))
                 * _dot(yg_ref[rs, :], wbg_ref[...]))
        y = _dot(m.astype(BF16), wo_ref[...])
        o_ref[rs, :] = _layer_norm(ALPHA * x + y, g_ref[...], b_ref[...])


def _merge(x, y_pool, y_mla, y_gla, w_gates, w_bp, w_bm, w_bg, w_out, ln_g, ln_b, layer, tm):
    t = x.shape[0]
    row = lambda i: (i, 0)
    lay = lambda i: (layer, 0, 0)
    return pl.pallas_call(
        _merge_kernel,
        grid=(t // tm,),
        in_specs=[
            pl.BlockSpec((tm, D_MODEL), row),
            pl.BlockSpec((tm, POOL_WIDTH), row),
            pl.BlockSpec((tm, MLA_HEADS * MLA_V), row),
            pl.BlockSpec((tm, GLA_V), row),
            _resident((None, D_MODEL, 3 * D_MODEL), lay),
            _resident((None, POOL_WIDTH, D_MODEL), lay),
            _resident((None, MLA_HEADS * MLA_V, D_MODEL), lay),
            _resident((None, GLA_V, D_MODEL), lay),
            _resident((None, D_MODEL, D_MODEL), lay),
            _resident((None, None, 1, D_MODEL), lambda i: (layer, 1, 0, 0)),
            _resident((None, None, 1, D_MODEL), lambda i: (layer, 1, 0, 0)),
        ],
        out_specs=pl.BlockSpec((tm, D_MODEL), row),
        out_shape=jax.ShapeDtypeStruct((t, D_MODEL), F32),
        compiler_params=_cparams(("parallel",)),
        name="merge",
    )(x, y_pool, y_mla, y_gla, w_gates, w_bp, w_bm, w_bg, w_out, ln_g, ln_b)


def _memkv_kernel(m_ref, wk_ref, wv_ref, k_ref, v_ref, kb_ref, vb_ref):
    mb = m_ref[...].astype(BF16)
    k = _dot(mb, wk_ref[...])
    v = _dot(mb, wv_ref[...])
    kb_ref[...] = k.astype(BF16)
    vb_ref[...] = v.astype(BF16)
    for h in range(MEM_HEADS):
        a = h * MEM_HEAD_DIM
        k_ref[:, h, :] = k[:, a:a + MEM_HEAD_DIM]
        v_ref[:, h, :] = v[:, a:a + MEM_HEAD_DIM]


def _memkv(mem, w_k, w_v, tm):
    rows = mem.shape[0]
    nl = w_k.shape[0]
    split = pl.BlockSpec((None, tm, MEM_HEADS, MEM_HEAD_DIM), lambda l, i: (l, i, 0, 0))
    flat = pl.BlockSpec((None, tm, D_MODEL), lambda l, i: (l, i, 0))
    return pl.pallas_call(
        _memkv_kernel,
        grid=(nl, rows // tm),
        in_specs=[
            pl.BlockSpec((tm, D_MODEL), lambda l, i: (i, 0)),
            pl.BlockSpec((None, D_MODEL, D_MODEL), lambda l, i: (l, 0, 0)),
            pl.BlockSpec((None, D_MODEL, D_MODEL), lambda l, i: (l, 0, 0)),
        ],
        out_specs=[split, split, flat, flat],
        out_shape=(jax.ShapeDtypeStruct((nl, rows, MEM_HEADS, MEM_HEAD_DIM), F32),
                   jax.ShapeDtypeStruct((nl, rows, MEM_HEADS, MEM_HEAD_DIM), F32),
                   jax.ShapeDtypeStruct((nl, rows, D_MODEL), BF16),
                   jax.ShapeDtypeStruct((nl, rows, D_MODEL), BF16)),
        compiler_params=_cparams(("parallel", "parallel")),
        name="memkv",
    )(mem, w_k, w_v)


def _memattn_kernel(x_ref, k_ref, v_ref, wq_ref, wo_ref, g_ref, b_ref, o_ref):
    x = x_ref[...]
    q = (_dot(x.astype(BF16), wq_ref[...]) * (MEM_HEAD_DIM ** -0.5 * LOG2E)).astype(BF16)
    kb = k_ref[...].astype(BF16)
    vb = v_ref[...].astype(BF16)
    outs = []
    for h in range(MEM_HEADS):
        a = h * MEM_HEAD_DIM
        s = _dot_nt(q[:, a:a + MEM_HEAD_DIM], kb[:, a:a + MEM_HEAD_DIM])
        p = jnp.exp2(s - jnp.max(s, axis=-1, keepdims=True))
        inv_l = 1.0 / jnp.sum(p, axis=-1, keepdims=True)
        outs.append(_dot(p.astype(BF16), vb[:, a:a + MEM_HEAD_DIM]) * inv_l)
    o = jnp.concatenate(outs, axis=-1).astype(BF16)
    for rs in _row_blocks(x_ref.shape[0]):
        y = _dot(o[rs, :], wo_ref[...])
        o_ref[rs, :] = _layer_norm(ALPHA * x_ref[rs, :] + y, g_ref[...], b_ref[...])


def _memattn(x, mem_k, mem_v, kv_layer, w_q, w_o, ln_g, ln_b, layer, nb, seq, tm):
    nt = seq // tm
    n_mem = mem_k.shape[2]
    lay = lambda b, i: (layer, 0, 0)
    return pl.pallas_call(
        _memattn_kernel,
        grid=(nb, nt),
        in_specs=[
            pl.BlockSpec((tm, D_MODEL), lambda b, i: (b * nt + i, 0)),
            pl.BlockSpec((None, None, n_mem, D_MODEL), lambda b, i: (kv_layer, b, 0, 0)),
            pl.BlockSpec((None, None, n_mem, D_MODEL), lambda b, i: (kv_layer, b, 0, 0)),
            _resident((None, D_MODEL, D_MODEL), lay),
            _resident((None, D_MODEL, D_MODEL), lay),
            _resident((None, None, 1, D_MODEL), lambda b, i: (layer, 2, 0, 0)),
            _resident((None, None, 1, D_MODEL), lambda b, i: (layer, 2, 0, 0)),
        ],
        out_specs=pl.BlockSpec((tm, D_MODEL), lambda b, i: (b * nt + i, 0)),
        out_shape=jax.ShapeDtypeStruct((nb * seq, D_MODEL), F32),
        compiler_params=_cparams(("parallel", "parallel")),
        name="memattn",
    )(x, mem_k, mem_v, w_q, w_o, ln_g, ln_b)


def _pack_weights(w_in, pool_w, mla_w_uq, mla_w_uk, mla_w_uv, gla_w_a2):
    nl = w_in.shape[0]
    splits = (POOL_WIDTH, MLA_Q_RANK, MLA_KV_RANK, MLA_ROPE, GLA_K, GLA_K, GLA_V,
              GLA_GATE_RANK, GLA_V, 3 * D_MODEL)
    offs = [0]
    for s in splits:
        offs.append(offs[-1] + s)
    col = lambda n: w_in[:, :, offs[n]:offs[n + 1]]
    half = MLA_ROPE // 2
    kr = col(3)
    kr_sw = jnp.concatenate([kr[..., half:], kr[..., :half]], -1)
    zeros = lambda n: jnp.zeros((nl, D_MODEL, n), w_in.dtype)
    w_a = jnp.concatenate(
        [col(0), col(1), col(2),
         kr, col(7), zeros(MISC_W - MLA_ROPE - GLA_GATE_RANK),
         kr_sw, zeros(MISC_W - MLA_ROPE),
         col(4), col(5), col(6), col(8)], -1).astype(BF16)
    w_gates = col(9).astype(BF16)

    uq = mla_w_uq.reshape(nl, MLA_Q_RANK, MLA_HEADS, MLA_NOPE + MLA_ROPE)
    uq_n, uq_r = uq[..., :MLA_NOPE], uq[..., MLA_NOPE:]
    uq_rs = jnp.concatenate([uq_r[..., half:], uq_r[..., :half]], -1)
    padq = jnp.zeros((nl, MLA_Q_RANK, MLA_HEADS, MLA_HEAD_PAD - MLA_NOPE - MLA_ROPE), uq.dtype)
    q_main = jnp.concatenate([uq_n, uq_r, padq], -1)
    q_swap = jnp.concatenate([jnp.zeros_like(uq_n), uq_rs, padq], -1)
    kw = MLA_HEADS * MLA_HEAD_PAD
    w_q2 = jnp.concatenate([q_main.reshape(nl, MLA_Q_RANK, kw),
                            q_swap.reshape(nl, MLA_Q_RANK, kw)], -1).astype(BF16)

    padk = jnp.zeros((nl, MLA_KV_RANK, MLA_HEADS, MLA_HEAD_PAD - MLA_NOPE), mla_w_uk.dtype)
    k_lat = jnp.concatenate([mla_w_uk, padk], -1).reshape(nl, MLA_KV_RANK, kw)
    place = jnp.zeros((MISC_W, MLA_HEADS, MLA_HEAD_PAD), F32)
    idx = jnp.arange(MLA_ROPE)
    place = place.at[idx, :, MLA_NOPE + idx].set(1.0).reshape(MISC_W, kw)
    k_misc = jnp.broadcast_to(place, (nl, MISC_W, kw))
    w_k = jnp.concatenate([k_lat, k_misc], 1).astype(BF16)
    w_kt = jnp.swapaxes(w_k, 1, 2)
    uv = mla_w_uv.reshape(nl, MLA_KV_RANK, MLA_HEADS // MLA_PAIR, MLA_PAIR, MLA_V)
    padv = jnp.zeros_like(uv[:, :, :, 0])
    w_v = jnp.stack([jnp.concatenate([uv[:, :, :, 0], padv], -1),
                     jnp.concatenate([padv, uv[:, :, :, 1]], -1)], 3)
    w_v = w_v.reshape(nl, MLA_KV_RANK, kw).astype(BF16)

    uk_t = jnp.transpose(mla_w_uk, (0, 2, 3, 1))
    a_lat = jnp.concatenate(
        [uk_t, jnp.zeros((nl, MLA_HEADS, MLA_HEAD_PAD - MLA_NOPE, MLA_KV_RANK), uk_t.dtype)], 2)
    a_misc = jnp.zeros((MLA_HEAD_PAD, MISC_W), F32).at[MLA_NOPE + idx, idx].set(1.0)
    a_misc = jnp.broadcast_to(a_misc, (nl, MLA_HEADS, MLA_HEAD_PAD, MISC_W))
    w_abs = jnp.concatenate([a_lat, a_misc], -1).astype(BF16)
    eye_h = jnp.eye(MLA_HEADS, dtype=mla_w_uv.dtype)
    uv_h = jnp.transpose(mla_w_uv, (0, 2, 1, 3))
    w_uvp = (uv_h[:, :, :, None, :] * eye_h[None, :, None, :, None]).reshape(
        nl, MLA_HEADS, MLA_KV_RANK, MLA_HEADS * MLA_V).astype(BF16)

    ng = len(POOL_WINDOWS)
    eye = jnp.eye(ng, dtype=pool_w.dtype)
    pool_bd = (pool_w[:, :, :, None, :] * eye[None, :, None, :, None]).reshape(
        nl, POOL_WIDTH, POOL_WIDTH).astype(BF16)

    w_a2 = jnp.zeros((nl, MISC_W, GLA_K), gla_w_a2.dtype)
    w_a2 = w_a2.at[:, MLA_ROPE:MLA_ROPE + GLA_GATE_RANK, :].set(gla_w_a2).astype(BF16)
    return w_a, w_gates, w_q2, w_k, w_kt, w_v, w_abs, w_uvp, pool_bd, w_a2


def _rope_table(pos):
    half = MLA_ROPE // 2
    inv = ROPE_BASE ** (-jnp.arange(half, dtype=F32) / half)
    ang = pos.astype(F32)[:, None] * inv
    cos, sin = jnp.cos(ang), jnp.sin(ang)
    n = pos.shape[0]
    one = jnp.ones((n, GLA_GATE_RANK), F32)
    cm = jnp.concatenate([cos, cos, one, jnp.zeros((n, MISC_W - MLA_ROPE - GLA_GATE_RANK), F32)], -1)
    sm = jnp.concatenate([-sin, sin, jnp.zeros((n, MISC_W - MLA_ROPE), F32)], -1)
    padq = jnp.zeros((n, MLA_HEAD_PAD - MLA_NOPE - MLA_ROPE), F32)
    qs = MLA_SCALE * LOG2E
    cq = jnp.concatenate([jnp.ones((n, MLA_NOPE), F32), cos, cos, padq], -1) * qs
    sq = jnp.concatenate([jnp.zeros((n, MLA_NOPE), F32), -sin, sin, padq], -1) * qs
    return jnp.concatenate([cm, sm, cq, sq], -1)


def _state_to_tiles(s):
    eye = jnp.eye(GLA_HEADS, dtype=s.dtype)
    st = jnp.swapaxes(s, 2, 3)
    bd = st[:, :, :, None, :] * eye[None, :, None, :, None]
    return bd.reshape(s.shape[0], GLA_V, GLA_K)


def _tiles_to_state(t):
    b = t.shape[0]
    t5 = t.reshape(b, GLA_HEADS, GLA_DV, GLA_HEADS, GLA_DK)
    idx = jnp.arange(GLA_HEADS)
    diag = t5[:, idx, :, idx, :]
    return jnp.transpose(diag, (1, 0, 3, 2))


def _tile(n, pref):
    return pref if n % pref == 0 else n


def kernel(x_prompt, x_sample, mem_prompt, cache_pool, cache_mla_ckv, cache_mla_krope, state_gla,
           cache_mem_k, cache_mem_v, ln_g, ln_b, w_ffn1_in, w_ffn1_out, w_in, pool_w, pool_scale,
           mla_q_norm, mla_kv_norm, mla_w_uq, mla_w_uk, mla_w_uv, gla_w_a2, gla_b_a, gla_norm,
           w_branch_pool, w_branch_mla, w_branch_gla, w_out, mem_w_q, mem_w_k, mem_w_v, mem_w_o,
           w_ffn2_in, w_ffn2_out):
    nl = w_in.shape[0]
    bp, sp, _ = x_prompt.shape
    bs, ss, _ = x_sample.shape
    past = cache_mla_ckv.shape[2]
    n_mem = mem_prompt.shape[1]

    w_a, w_gates, w_q2, w_k, w_kt, w_v, w_abs, w_uvp, pool_bd, w_a2 = _pack_weights(
        w_in, pool_w, mla_w_uq, mla_w_uk, mla_w_uv, gla_w_a2)
    bf = lambda w: w.astype(BF16)
    ffn_w = ((bf(w_ffn1_in), bf(w_ffn1_out)), (bf(w_ffn2_in), bf(w_ffn2_out)))
    w_bp, w_bm, w_bg, w_o = bf(w_branch_pool), bf(w_branch_mla), bf(w_branch_gla), bf(w_out)
    mw_q, mw_k, mw_v, mw_o = bf(mem_w_q), bf(mem_w_k), bf(mem_w_v), bf(mem_w_o)
    lng = ln_g.reshape(nl, 4, 1, D_MODEL)
    lnb = ln_b.reshape(nl, 4, 1, D_MODEL)
    qnorm = mla_q_norm.reshape(nl, 1, MLA_Q_RANK)
    kvnorm = mla_kv_norm.reshape(nl, 1, MLA_KV_RANK)
    pscale = pool_scale.reshape(nl, 1, POOL_WIDTH)
    b_a = gla_b_a.reshape(nl, 1, GLA_K)
    norm_g = jnp.tile(gla_norm, (1, GLA_HEADS)).reshape(nl, 1, GLA_V)
    hsel = (jnp.arange(GLA_K)[:, None] // GLA_DK == jnp.arange(GLA_V)[None, :] // GLA_DV).astype(BF16)
    hblk = (jnp.arange(GLA_V)[:, None] // GLA_DV == jnp.arange(GLA_V)[None, :] // GLA_DV).astype(BF16)

    tmem = _tile(bp * n_mem, KEY_TILE)
    mem_k_out, mem_v_out, mem_k, mem_v = _memkv(
        mem_prompt.reshape(bp * n_mem, D_MODEL), mw_k, mw_v, tmem)
    mem_k = mem_k.reshape(nl, bp, n_mem, D_MODEL)
    mem_v = mem_v.reshape(nl, bp, n_mem, D_MODEL)
    smem_k = cache_mem_k.reshape(nl, bs, n_mem, D_MODEL)
    smem_v = cache_mem_v.reshape(nl, bs, n_mem, D_MODEL)

    groups = {
        "p": dict(nb=bp, seq=sp, pos0=0, tm=_tile(sp, ROW_TILE)),
        "s": dict(nb=bs, seq=ss, pos0=past, tm=_tile(bs * ss, KEY_TILE)),
    }
    for gname, gr in groups.items():
        seq, tm = gr["seq"], gr["tm"]
        pos = gr["pos0"] + jnp.arange(seq)
        tab = _rope_table(pos)
        if tm <= seq:
            gr["tab"], gr["tab_period"] = tab, seq // tm
        else:
            gr["tab"], gr["tab_period"] = jnp.tile(tab, (tm // seq, 1)), 1
        gr["tseq"] = _tile(seq, KEY_TILE)
        gr["tgla"] = _tile(seq, ROW_TILE)
        gr["tpool"] = _tile(seq, POOL_TILE)
        gr["ttok"] = _tile(gr["nb"] * seq, ROW_TILE)
        gr["tmem"] = _tile(seq, ROW_TILE)

    xs = {"p": x_prompt.reshape(bp * sp, D_MODEL), "s": x_sample.reshape(bs * ss, D_MODEL)}
    zero_hist = jnp.zeros((bp, POOL_PAD, POOL_WIDTH), F32)
    zero_state = jnp.zeros((bp, GLA_V, GLA_K), F32)
    outs = {n: {"pool": [], "ckv": [], "kr": [], "gla": []} for n in groups}

    for l in range(nl):
        for gname, gr in groups.items():
            nb, seq, tm = gr["nb"], gr["seq"], gr["tm"]
            x = xs[gname]
            x = _ffn(x, ffn_w[0][0], ffn_w[0][1], lng, lnb, l, 0, gr["ttok"])
            k_tile = gr["tseq"] if gname == "p" else None
            u_pool, ckv, misc, q, k, v, gla_in = _inproj(
                x, gr["tab"], w_a, w_q2, w_k if k_tile is None else w_kt, w_v, qnorm, kvnorm,
                l, tm, gr["tab_period"], k_tile)
            if gname == "p":
                hist = None
                hist16 = zero_hist
                s0t = zero_state
            else:
                hist = cache_pool[l]
                hist16 = jnp.concatenate(
                    [jnp.zeros((nb, POOL_PAD - POOL_HIST, POOL_WIDTH), F32), hist], 1)
                s0t = _state_to_tiles(state_gla[l])
            y_pool = _pool(u_pool, hist16, pool_bd, pscale, l, nb, seq, gr["tpool"], gr["pos0"])
            if gname == "p":
                y_mla = _mla_causal(q, k, v, nb, seq, gr["tseq"])
            else:
                y_mla = _mla_cached(q, ckv, misc, cache_mla_ckv, cache_mla_krope, w_abs, w_uvp,
                                    l, nb, seq, past)
            y_gla, s_fin = _gla(gla_in, misc, w_a2, b_a, norm_g, hsel, hblk, s0t, l, nb, seq, gr["tgla"])
            x = _merge(x, y_pool, y_mla, y_gla, w_gates, w_bp, w_bm, w_bg, w_o, lng, lnb, l,
                       gr["ttok"])
            if gname == "p":
                x = _memattn(x, mem_k, mem_v, l, mw_q, mw_o, lng, lnb, l, nb, seq, gr["tmem"])
            else:
                x = _memattn(x, smem_k, smem_v, l, mw_q, mw_o, lng, lnb, l, nb, seq, gr["tmem"])
            x = _ffn(x, ffn_w[1][0], ffn_w[1][1], lng, lnb, l, 3, gr["ttok"])
            xs[gname] = x

            u3 = u_pool.reshape(nb, seq, POOL_WIDTH)
            if hist is None:
                hist = jnp.zeros((nb, POOL_HIST, POOL_WIDTH), F32)
            new_pool = jnp.concatenate([hist, u3], 1)[:, -POOL_HIST:]
            o = outs[gname]
            o["pool"].append(new_pool)
            o["ckv"].append(ckv.reshape(nb, seq, MLA_KV_RANK))
            o["kr"].append(misc[:, :MLA_ROPE].reshape(nb, seq, MLA_ROPE))
            o["gla"].append(_tiles_to_state(s_fin))

    st = lambda n, key: jnp.stack(outs[n][key], 0)
    return (xs["p"].reshape(bp, sp, D_MODEL), xs["s"].reshape(bs, ss, D_MODEL),
            st("p", "pool"), st("s", "pool"),
            st("p", "ckv"), st("s", "ckv"),
            st("p", "kr"), st("s", "kr"),
            st("p", "gla"), st("s", "gla"),
            mem_k_out.reshape(nl, bp, n_mem, MEM_HEADS, MEM_HEAD_DIM),
            mem_v_out.reshape(nl, bp, n_mem, MEM_HEADS, MEM_HEAD_DIM))
```
